```python
import math
import jax
import jax.numpy as jnp
from jax import lax
import numpy as np

D_MODEL = 1024
BATCH = 8
SEQ = 4096
DEPTH = 1

CHUNK = 64
EPS = 1e-6
A_HEADS = 16
A_HEAD_DIM = 64
A_WIDTH = A_HEADS * A_HEAD_DIM
LEFT_CHUNKS = 8
BAND = (LEFT_CHUNKS + 1) * CHUNK
REL_CLIP = 128
SSM_INNER = 2 * D_MODEL
SSM_HEAD_DIM = 64
SSM_HEADS = SSM_INNER // SSM_HEAD_DIM
SSM_GROUPS = 4
SSM_HPG = SSM_HEADS // SSM_GROUPS
SSM_STATE = 128
CONV_WIDTH = 4
CONV_CH = SSM_INNER + 2 * SSM_GROUPS * SSM_STATE
DT_MIN = 0.001
DT_MAX = 0.1
MEM_LEN = 256
X_HEADS = 4
X_HEAD_DIM = 256
X_WIDTH = X_HEADS * X_HEAD_DIM
N_BRANCH = 3
IN_SPLITS = (
    A_WIDTH,
    2 * A_WIDTH,
    3 * A_WIDTH,
    3 * A_WIDTH + SSM_INNER,
    3 * A_WIDTH + SSM_INNER + CONV_CH,
    3 * A_WIDTH + SSM_INNER + CONV_CH + SSM_HEADS,
    3 * A_WIDTH + SSM_INNER + CONV_CH + SSM_HEADS + X_WIDTH,
)
IN_COLS = 3 * A_WIDTH + SSM_INNER + CONV_CH + SSM_HEADS + X_WIDTH + N_BRANCH * D_MODEL
N_EXPERTS = 32
TOP_K = 4
D_EXPERT = D_MODEL
SWIGLU_LIMIT = 7.0
SWIGLU_ALPHA = 1.702
MOE_BLOCK = 128

kernel_name = "hybrid_chunk_attn_ssd_memxattn_moe"


def rms_norm(x, gain):
    xf = x.astype(jnp.float32)
    y = xf * lax.rsqrt(jnp.mean(xf * xf, axis=-1, keepdims=True) + EPS)
    return (y * gain.astype(jnp.float32)).astype(x.dtype)


def chunk_attention(q, k, v, q_gain, k_gain, rel_bias):
    b, s = q.shape[0], q.shape[1]
    n_chunks = s // CHUNK
    left = LEFT_CHUNKS * CHUNK
    q = rms_norm(q.reshape(b, s, A_HEADS, A_HEAD_DIM), q_gain)
    k = rms_norm(k.reshape(b, s, A_HEADS, A_HEAD_DIM), k_gain)
    v = v.reshape(b, s, A_HEADS, A_HEAD_DIM)
    kp = jnp.pad(k, ((0, 0), (left, 0), (0, 0), (0, 0)))
    vp = jnp.pad(v, ((0, 0), (left, 0), (0, 0), (0, 0)))
    rel = jnp.arange(CHUNK)[:, None] + left - jnp.arange(BAND)[None, :]
    bias = rel_bias[:, jnp.clip(rel, -REL_CLIP, REL_CLIP) + REL_CLIP].astype(jnp.float32)
    qc = q.reshape(b, n_chunks, CHUNK, A_HEADS, A_HEAD_DIM).transpose(1, 0, 2, 3, 4)
    scale = A_HEAD_DIM ** -0.5

    def one_chunk(args):
        qb, c = args
        start = c * CHUNK
        kb = lax.dynamic_slice_in_dim(kp, start, BAND, axis=1)
        vb = lax.dynamic_slice_in_dim(vp, start, BAND, axis=1)
        sc = jnp.einsum("bqhd,bkhd->bhqk", qb, kb).astype(jnp.float32) * scale + bias
        valid = (start - left + jnp.arange(BAND)) >= 0
        sc = jnp.where(valid, sc, -jnp.inf)
        p = jax.nn.softmax(sc, axis=-1).astype(vb.dtype)
        return jnp.einsum("bhqk,bkhd->bqhd", p, vb)

    out = lax.map(one_chunk, (qc, jnp.arange(n_chunks)))
    return out.transpose(1, 0, 2, 3, 4).reshape(b, s, A_WIDTH)


def segsum(a):
    n = a.shape[-1]
    cs = jnp.cumsum(a, axis=-1)
    diff = cs[..., :, None] - cs[..., None, :]
    return jnp.where(jnp.tril(jnp.ones((n, n), dtype=bool)), diff, -jnp.inf)


def mamba2_branch(z, xbc, dt_raw, conv_w, conv_b, dt_bias, a_log, d_skip, norm_gain):
    f32 = jnp.float32
    b, s = xbc.shape[0], xbc.shape[1]
    n_chunks = s // CHUNK
    xpad = jnp.pad(xbc, ((0, 0), (CONV_WIDTH - 1, 0), (0, 0)))
    conv = conv_b + xpad[:, 0:s] * conv_w[0]
    for i in range(1, CONV_WIDTH):
        conv = conv + xpad[:, i:i + s] * conv_w[i]
    xbc = jax.nn.silu(conv)
    xs, bm, cm = jnp.split(xbc, [SSM_INNER, SSM_INNER + SSM_GROUPS * SSM_STATE], axis=-1)
    x6 = xs.astype(f32).reshape(b, n_chunks, CHUNK, SSM_GROUPS, SSM_HPG, SSM_HEAD_DIM)
    bc = bm.astype(f32).reshape(b, n_chunks, CHUNK, SSM_GROUPS, SSM_STATE)
    cc = cm.astype(f32).reshape(b, n_chunks, CHUNK, SSM_GROUPS, SSM_STATE)
    dt = jax.nn.softplus(dt_raw.astype(f32) + dt_bias.astype(f32))
    dt5 = dt.reshape(b, n_chunks, CHUNK, SSM_GROUPS, SSM_HPG)
    a_head = -jnp.exp(a_log.astype(f32)).reshape(SSM_GROUPS, SSM_HPG)
    a = (dt5 * a_head).transpose(0, 1, 3, 4, 2)
    a_cs = jnp.cumsum(a, axis=-1)
    xdt = x6 * dt5[..., None]
    decay_in = jnp.exp(segsum(a))
    cb = jnp.einsum("bclgn,bcsgn->bcgls", cc, bc)
    y_diag = jnp.einsum("bcgls,bcgrls,bcsgrp->bclgrp", cb, decay_in, xdt)
    decay_to_end = jnp.exp(a_cs[..., -1:] - a_cs)
    states = jnp.einsum("bclgn,bcgrl,bclgrp->bcgrpn", bc, decay_to_end, xdt)
    chunk_decay = jnp.exp(a_cs[..., -1])

    def step(state, inp):
        dec, st = inp
        return dec[..., None, None] * state + st, state

    init = jnp.zeros((b, SSM_GROUPS, SSM_HPG, SSM_HEAD_DIM, SSM_STATE), f32)
    _, prev = lax.scan(step, init, (jnp.moveaxis(chunk_decay, 1, 0), jnp.moveaxis(states, 1, 0)))
    prev = jnp.moveaxis(prev, 0, 1)
    y_off = jnp.einsum("bclgn,bcgrpn,bcgrl->bclgrp", cc, prev, jnp.exp(a_cs))
    y = (y_diag + y_off).reshape(b, s, SSM_HEADS, SSM_HEAD_DIM)
    y = y + d_skip.astype(f32)[:, None] * xs.astype(f32).reshape(b, s, SSM_HEADS, SSM_HEAD_DIM)
    yz = y.reshape(b, s, SSM_INNER) * jax.nn.silu(z.astype(f32))
    yz = rms_norm(yz.reshape(b, s, SSM_GROUPS, SSM_INNER // SSM_GROUPS),
                  norm_gain.reshape(SSM_GROUPS, SSM_INNER // SSM_GROUPS))
    return yz.reshape(b, s, SSM_INNER).astype(z.dtype)


def memory_attention(q, mem, mem_gain, w_mem_kv, q_gain, k_gain):
    b, s = q.shape[0], q.shape[1]
    m = mem.shape[1]
    kv = rms_norm(mem, mem_gain) @ w_mem_kv
    k, v = jnp.split(kv, 2, axis=-1)
    q = rms_norm(q.reshape(b, s, X_HEADS, X_HEAD_DIM), q_gain)
    k = rms_norm(k.reshape(b, m, X_HEADS, X_HEAD_DIM), k_gain)
    v = v.reshape(b, m, X_HEADS, X_HEAD_DIM)
    sc = jnp.einsum("bshd,bmhd->bhsm", q, k).astype(jnp.float32) * (X_HEAD_DIM ** -0.5)
    p = jax.nn.softmax(sc, axis=-1).astype(v.dtype)
    return jnp.einsum("bhsm,bmhd->bshd", p, v).reshape(b, s, X_WIDTH)


def moe_ffn(h, w_router, b_router, w_gate, b_gate, w_up, b_up, w_down, b_down):
    b, s, d = h.shape
    t = b * s
    xt = h.reshape(t, d)
    logits = (xt @ w_router + b_router).astype(jnp.float32)
    top_val, top_idx = lax.top_k(logits, TOP_K)
    wts = jax.nn.softmax(top_val, axis=-1)
    n_assign = t * TOP_K
    e_flat = top_idx.reshape(-1)
    tok_flat = jnp.arange(n_assign, dtype=jnp.int32) // TOP_K
    w_flat = wts.reshape(-1)
    order = jnp.argsort(e_flat)
    e_s, tok_s, w_s = e_flat[order], tok_flat[order], w_flat[order]
    counts = jnp.bincount(e_flat, length=N_EXPERTS)
    starts = jnp.cumsum(counts) - counts
    padded = (counts + MOE_BLOCK - 1) // MOE_BLOCK * MOE_BLOCK
    pad_ends = jnp.cumsum(padded)
    pad_starts = pad_ends - padded
    dest = pad_starts[e_s] + jnp.arange(n_assign, dtype=jnp.int32) - starts[e_s]
    n_rows = n_assign + N_EXPERTS * MOE_BLOCK
    row_tok = jnp.full((n_rows,), t, jnp.int32).at[dest].set(tok_s)
    row_w = jnp.zeros((n_rows,), jnp.float32).at[dest].set(w_s)
    n_blocks = n_rows // MOE_BLOCK
    block_start = jnp.arange(n_blocks, dtype=jnp.int32) * MOE_BLOCK
    block_exp = jnp.minimum(jnp.searchsorted(pad_ends, block_start, side="right"), N_EXPERTS - 1)
    x_pad = jnp.concatenate([xt, jnp.zeros((1, d), xt.dtype)], axis=0)
    xb = x_pad[row_tok].reshape(n_blocks, MOE_BLOCK, d)

    def expert_block(args):
        xblk, e = args
        g = xblk @ w_gate[e] + b_gate[e]
        u = xblk @ w_up[e] + b_up[e]
        g = jnp.minimum(g, SWIGLU_LIMIT)
        u = jnp.clip(u, -SWIGLU_LIMIT, SWIGLU_LIMIT)
        act = (u + 1.0) * g * jax.nn.sigmoid(SWIGLU_ALPHA * g)
        return act @ w_down[e] + b_down[e]

    yb = lax.map(expert_block, (xb, block_exp)).reshape(n_rows, d)
    y = jax.ops.segment_sum(yb * row_w[:, None].astype(yb.dtype), row_tok, num_segments=t + 1)[:t]
    return y.reshape(b, s, d)


def setup_inputs(seed: int = 0) -> dict:
    key = jax.random.key(seed)
    ks = jax.random.split(key, 32)
    f32 = jnp.float32
    L = DEPTH

    def normal(k, shape, scale):
        return jax.random.normal(k, shape, f32) * scale

    def gain(k, shape):
        return 1.0 + 0.05 * jax.random.normal(k, shape, f32)

    dt = jnp.exp(jax.random.uniform(ks[8], (L, SSM_HEADS), f32, math.log(DT_MIN), math.log(DT_MAX)))
    dt_bias = dt + jnp.log(-jnp.expm1(-dt))
    a_log = jnp.log(jax.random.uniform(ks[9], (L, SSM_HEADS), f32, 1.0, 16.0))
    return {
        "x": normal(ks[0], (BATCH, SEQ, D_MODEL), 1.0),
        "mem": normal(ks[1], (BATCH, MEM_LEN, D_MODEL), 1.0),
        "norm_mix": gain(ks[2], (L, D_MODEL)),
        "w_in": normal(ks[3], (L, D_MODEL, IN_COLS), D_MODEL ** -0.5),
        "a_q_gain": gain(ks[4], (L, A_HEAD_DIM)),
        "a_k_gain": gain(ks[5], (L, A_HEAD_DIM)),
        "a_rel_bias": normal(ks[6], (L, A_HEADS, 2 * REL_CLIP + 1), 0.1),
        "conv_w": normal(ks[7], (L, CONV_WIDTH, CONV_CH), CONV_WIDTH ** -0.5),
        "conv_b": normal(ks[10], (L, CONV_CH), 0.01),
        "dt_bias": dt_bias,
        "a_log": a_log,
        "d_skip": gain(ks[11], (L, SSM_HEADS)),
        "ssm_norm": gain(ks[12], (L, SSM_INNER)),
        "norm_mem": gain(ks[13], (L, D_MODEL)),
        "w_mem_kv": normal(ks[14], (L, D_MODEL, 2 * X_WIDTH), D_MODEL ** -0.5),
        "x_q_gain": gain(ks[15], (L, X_HEAD_DIM)),
        "x_k_gain": gain(ks[16], (L, X_HEAD_DIM)),
        "w_br_a": normal(ks[17], (L, A_WIDTH, D_MODEL), A_WIDTH ** -0.5),
        "w_br_b": normal(ks[18], (L, SSM_INNER, D_MODEL), SSM_INNER ** -0.5),
        "w_br_c": normal(ks[19], (L, X_WIDTH, D_MODEL), X_WIDTH ** -0.5),
        "w_out": normal(ks[20], (L, D_MODEL, D_MODEL), D_MODEL ** -0.5),
        "norm_ffn": gain(ks[21], (L, D_MODEL)),
        "w_router": normal(ks[22], (L, D_MODEL, N_EXPERTS), D_MODEL ** -0.5),
        "b_router": normal(ks[23], (L, N_EXPERTS), 0.01),
        "w_gate": normal(ks[24], (L, N_EXPERTS, D_MODEL, D_EXPERT), D_MODEL ** -0.5),
        "b_gate": normal(ks[25], (L, N_EXPERTS, D_EXPERT), 0.01),
        "w_up": normal(ks[26], (L, N_EXPERTS, D_MODEL, D_EXPERT), D_MODEL ** -0.5),
        "b_up": normal(ks[27], (L, N_EXPERTS, D_EXPERT), 0.01),
        "w_down": normal(ks[28], (L, N_EXPERTS, D_EXPERT, D_MODEL), D_EXPERT ** -0.5),
        "b_down": normal(ks[29], (L, N_EXPERTS, D_MODEL), 0.01),
    }


def reference(x, mem, norm_mix, w_in, a_q_gain, a_k_gain, a_rel_bias, conv_w, conv_b, dt_bias, a_log,
              d_skip, ssm_norm, norm_mem, w_mem_kv, x_q_gain, x_k_gain, w_br_a, w_br_b, w_br_c, w_out,
              norm_ffn, w_router, b_router, w_gate, b_gate, w_up, b_up, w_down, b_down):
    b, s, d = x.shape
    for l in range(DEPTH):
        h = rms_norm(x, norm_mix[l])
        proj = h @ w_in[l]
        q_a, k_a, v_a, z, xbc, dt_raw, q_x, gate_logits = jnp.split(proj, IN_SPLITS, axis=-1)
        y_a = chunk_attention(q_a, k_a, v_a, a_q_gain[l], a_k_gain[l], a_rel_bias[l])
        y_b = mamba2_branch(z, xbc, dt_raw, conv_w[l], conv_b[l], dt_bias[l], a_log[l], d_skip[l], ssm_norm[l])
        y_c = memory_attention(q_x, mem, norm_mem[l], w_mem_kv[l], x_q_gain[l], x_k_gain[l])
        gates = jax.nn.sigmoid(gate_logits.astype(jnp.float32)).astype(x.dtype).reshape(b, s, N_BRANCH, d)
        merged = (gates[:, :, 0] * (y_a @ w_br_a[l])
                  + gates[:, :, 1] * (y_b @ w_br_b[l])
                  + gates[:, :, 2] * (y_c @ w_br_c[l]))
        x = x + merged @ w_out[l]
        x = x + moe_ffn(rms_norm(x, norm_ffn[l]), w_router[l], b_router[l], w_gate[l], b_gate[l],
                        w_up[l], b_up[l], w_down[l], b_down[l])
    return x
```

```python
import functools

import jax
import jax.numpy as jnp
from jax import lax
from jax.experimental import pallas as pl
from jax.experimental.pallas import tpu as pltpu

F32 = jnp.float32
BF16 = jnp.bfloat16
HIGHEST = lax.Precision.HIGHEST

V7X_LANES = 128
V7X_SUBLANES = 8
V7X_VMEM_LIMIT_BYTES = 56 * 1024 * 1024

EPS = 1e-6
NEG = -1e30

CHUNK = 64
A_HEADS = 16
A_HEAD_DIM = 64
LEFT_CHUNKS = 8
REL_CLIP = 128
SSM_HEADS = 32
SSM_HEAD_DIM = 64
SSM_GROUPS = 4
SSM_STATE = 128
CONV_WIDTH = 4
X_HEADS = 4
X_HEAD_DIM = 256
N_EXPERTS = 32
TOP_K = 4
SWIGLU_LIMIT = 7.0
SWIGLU_ALPHA = 1.702

ATTN_TQ = 256
SSD_L = 256
MOE_ROWS = 256


def _cparams(*sem):
    return pltpu.CompilerParams(dimension_semantics=sem, vmem_limit_bytes=V7X_VMEM_LIMIT_BYTES)


def _split_bf16(v):
    hi = v.astype(BF16)
    lo = (v - hi.astype(F32)).astype(BF16)
    return hi, lo


def _in_proj_kernel(x_ref, nw_ref, w_ref, wdt_ref, ind_a_ref, ind_at_ref, ind_x_ref, ind_xt_ref,
                    gains_ref, o_ref, dt_ref, h_ref, *, roles):
    j = pl.program_id(1)

    @pl.when(j == 0)
    def _():
        x = x_ref[...]
        ms = jnp.mean(x * x, axis=-1, keepdims=True)
        hb = (x * lax.rsqrt(ms + EPS) * nw_ref[...]).astype(BF16)
        h_ref[...] = hb
        dt_ref[...] = jnp.dot(hb, wdt_ref[...], preferred_element_type=F32)

    acc = jnp.dot(h_ref[...], w_ref[...], preferred_element_type=F32)

    def head_norm(ind_ref, indt_ref, head_dim, gain_row):
        s = jnp.dot((acc * acc).astype(BF16), ind_ref[...], preferred_element_type=F32)
        r = lax.rsqrt(s * (1.0 / head_dim) + EPS)
        r_hi, r_lo = _split_bf16(r)
        rexp = (jnp.dot(r_hi, indt_ref[...], preferred_element_type=F32)
                + jnp.dot(r_lo, indt_ref[...], preferred_element_type=F32))
        return acc * rexp * gain_row

    def cond_for(role):
        c = None
        for jj, r in enumerate(roles):
            if r == role:
                c = (j == jj) if c is None else jnp.logical_or(c, j == jj)
        return c

    for role in sorted(set(roles)):
        @pl.when(cond_for(role))
        def _(role=role):
            if role == "qa":
                out = head_norm(ind_a_ref, ind_at_ref, A_HEAD_DIM, gains_ref[0:1, :])
            elif role == "ka":
                out = head_norm(ind_a_ref, ind_at_ref, A_HEAD_DIM, gains_ref[1:2, :])
            elif role == "qx":
                out = head_norm(ind_x_ref, ind_xt_ref, X_HEAD_DIM, gains_ref[2:3, :])
            elif role == "sig":
                out = jax.nn.sigmoid(acc)
            else:
                out = acc
            o_ref[...] = out.astype(o_ref.dtype)


def _in_proj(x2d, norm_w, w_main, w_dt, ind_a, ind_at, ind_x, ind_xt, gains, roles, tm, tn):
    t, d = x2d.shape
    n = w_main.shape[1]
    assert t % tm == 0 and n % tn == 0 and len(roles) == n // tn
    const = lambda i, j: (0, 0)
    return pl.pallas_call(
        functools.partial(_in_proj_kernel, roles=roles),
        grid=(t // tm, n // tn),
        in_specs=[
            pl.BlockSpec((tm, d), lambda i, j: (i, 0)),
            pl.BlockSpec((1, d), const),
            pl.BlockSpec((d, tn), lambda i, j: (0, j)),
            pl.BlockSpec((d, V7X_LANES), const),
            pl.BlockSpec(ind_a.shape, const),
            pl.BlockSpec(ind_at.shape, const),
            pl.BlockSpec(ind_x.shape, const),
            pl.BlockSpec(ind_xt.shape, const),
            pl.BlockSpec(gains.shape, const),
        ],
        out_specs=[
            pl.BlockSpec((tm, tn), lambda i, j: (i, j)),
            pl.BlockSpec((tm, V7X_LANES), lambda i, j: (i, 0)),
        ],
        out_shape=[
            jax.ShapeDtypeStruct((t, n), BF16),
            jax.ShapeDtypeStruct((t, V7X_LANES), F32),
        ],
        scratch_shapes=[pltpu.VMEM((tm, d), BF16)],
        compiler_params=_cparams("parallel", "arbitrary"),
        name="in_proj",
    )(x2d, norm_w, w_main, w_dt, ind_a, ind_at, ind_x, ind_xt, gains)


def _attn_kernel(q_ref, *refs, tq, nprev):
    k_refs = refs[:nprev + 1]
    v_refs = refs[nprev + 1:2 * nprev + 2]
    bias_ref, o_ref = refs[2 * nprev + 2:]
    qb = pl.program_id(2)
    q2 = q_ref[...]
    kk = jnp.concatenate([r[...] for r in k_refs], axis=0)
    vv = jnp.concatenate([r[...] for r in v_refs], axis=0)
    nk = (nprev + 1) * tq
    lane = lax.broadcasted_iota(jnp.int32, (1, V7X_LANES), 1)
    col = lax.broadcasted_iota(jnp.int32, (1, nk), 1)
    before_start = col < (nprev - qb) * tq
    outs = []
    for hh in range(2):
        sel = (lane < A_HEAD_DIM) if hh == 0 else (lane >= A_HEAD_DIM)
        qm = jnp.where(sel, q2, jnp.zeros_like(q2))
        s = lax.dot_general(qm, kk, (((1,), (1,)), ((), ())), preferred_element_type=F32)
        s = jnp.where(before_start, NEG, s + bias_ref[hh])
        m = jnp.max(s, axis=-1, keepdims=True)
        p = jnp.exp(s - m)
        l = jnp.sum(p, axis=-1, keepdims=True)
        o = jnp.dot(p.astype(BF16), vv, preferred_element_type=F32)
        outs.append(o / l)
    o_ref[...] = jnp.where(lane < A_HEAD_DIM, outs[0], outs[1]).astype(o_ref.dtype)


def _attention(proj3, bias, q_col, k_col, v_col):
    b, s, _ = proj3.shape
    tq = ATTN_TQ
    left = LEFT_CHUNKS * CHUNK
    assert left % tq == 0 and s % tq == 0
    nprev = left // tq
    npairs = A_HEADS // 2

    def kv_spec(col0, back):
        return pl.BlockSpec((None, tq, V7X_LANES),
                            lambda hp, bi, qi: (bi, jnp.maximum(qi - back, 0), col0 + hp))

    in_specs = [pl.BlockSpec((None, tq, V7X_LANES), lambda hp, bi, qi: (bi, qi, q_col + hp))]
    in_specs += [kv_spec(k_col, nprev - i) for i in range(nprev + 1)]
    in_specs += [kv_spec(v_col, nprev - i) for i in range(nprev + 1)]
    in_specs += [pl.BlockSpec((2, tq, left + tq), lambda hp, bi, qi: (hp, 0, 0))]
    return pl.pallas_call(
        functools.partial(_attn_kernel, tq=tq, nprev=nprev),
        grid=(npairs, b, s // tq),
        in_specs=in_specs,
        out_specs=pl.BlockSpec((None, tq, V7X_LANES), lambda hp, bi, qi: (bi, qi, hp)),
        out_shape=jax.ShapeDtypeStruct((b, s, A_HEADS * A_HEAD_DIM), BF16),
        compiler_params=_cparams("parallel", "parallel", "parallel"),
        name="chunk_attn",
    )(proj3, *([proj3] * (2 * nprev + 2)), bias)


def _attn_bias(rel_bias):
    tq = ATTN_TQ
    left = LEFT_CHUNKS * CHUNK
    i = jnp.arange(tq)[:, None]
    j = jnp.arange(left + tq)[None, :]
    rel = i + left - j
    bias = rel_bias[:, jnp.clip(rel, -REL_CLIP, REL_CLIP) + REL_CLIP].astype(F32)
    qc, kc = i // CHUNK, j // CHUNK
    in_band = (kc >= qc) & (kc <= qc + LEFT_CHUNKS)
    return jnp.where(in_band[None], bias, NEG)


def _ssd_kernel(z_ref, x_ref, bc_ref, dt_ref, cwx_ref, cbx_ref, cwbc_ref, cbbc_ref, dtb_ref,
                aneg_ref, dskip_ref, gain_ref, e_ref, o_ref,
                xf_ref, bcf_ref, st_ref, xs_ref, y_ref, *, L):
    c = pl.program_id(1)
    inner = SSM_HEADS * SSM_HEAD_DIM
    gw = inner // SSM_GROUPS
    gs = SSM_GROUPS * SSM_STATE
    tail = V7X_SUBLANES

    @pl.when(c == 0)
    def _():
        xf_ref[0:tail, :] = jnp.zeros((tail, inner), F32)
        bcf_ref[0:tail, :] = jnp.zeros((tail, 2 * gs), F32)
        st_ref[...] = jnp.zeros_like(st_ref)

    xf_ref[tail:, :] = x_ref[...].astype(F32)
    bcf_ref[tail:, :] = bc_ref[...].astype(F32)

    def conv_silu(src_ref, w_ref, b_ref, c0, c1):
        acc = b_ref[:, c0:c1] + w_ref[CONV_WIDTH - 1:CONV_WIDTH, c0:c1] * src_ref[tail:tail + L, c0:c1]
        for k in range(1, CONV_WIDTH):
            acc = acc + (w_ref[CONV_WIDTH - 1 - k:CONV_WIDTH - k, c0:c1]
                         * src_ref[tail - k:tail - k + L, c0:c1])
        return acc * jax.nn.sigmoid(acc)

    for g in range(SSM_GROUPS):
        xs_ref[:, g * gw:(g + 1) * gw] = conv_silu(xf_ref, cwx_ref, cbx_ref, g * gw, (g + 1) * gw)
    bmat = conv_silu(bcf_ref, cwbc_ref, cbbc_ref, 0, gs)
    cmat = conv_silu(bcf_ref, cwbc_ref, cbbc_ref, gs, 2 * gs)
    xf_ref[0:tail, :] = xf_ref[L:L + tail, :]
    bcf_ref[0:tail, :] = bcf_ref[L:L + tail, :]

    pre = dt_ref[...] + dtb_ref[...]
    dt = jnp.maximum(pre, 0.0) + jnp.log1p(jnp.exp(-jnp.abs(pre)))
    a = dt * aneg_ref[...]
    row = lax.broadcasted_iota(jnp.int32, (L, L), 0)
    colm = lax.broadcasted_iota(jnp.int32, (L, L), 1)
    lower = colm <= row
    tri = jnp.where(lower, 1.0, 0.0).astype(F32)
    cs = jnp.dot(tri, a, precision=HIGHEST, preferred_element_type=F32)
    cs_t = cs.T
    dt_t = dt.T
    cs_last = cs[L - 1:L, :]
    w_state = dt * jnp.exp(cs_last - cs)
    e_cs = jnp.exp(cs)
    chunk_decay = jnp.broadcast_to(jnp.exp(cs_last), (tail, V7X_LANES))
    stacked = jnp.concatenate([w_state, e_cs, chunk_decay], axis=0)
    s_hi, s_lo = _split_bf16(stacked)
    expanded = (jnp.dot(s_hi, e_ref[...], preferred_element_type=F32)
                + jnp.dot(s_lo, e_ref[...], preferred_element_type=F32))
    w_state_e = expanded[0:L]
    e_cs_e = expanded[L:2 * L]
    decay_e = expanded[2 * L:2 * L + 1]

    lane = lax.broadcasted_iota(jnp.int32, (1, V7X_LANES), 1)
    pairs_per_group = gw // V7X_LANES
    for g in range(SSM_GROUPS):
        bg = bmat[:, g * SSM_STATE:(g + 1) * SSM_STATE]
        cg = cmat[:, g * SSM_STATE:(g + 1) * SSM_STATE].astype(BF16)
        cb = lax.dot_general(cg, bg.astype(BF16), (((1,), (1,)), ((), ())),
                             preferred_element_type=F32)
        state_b = st_ref[g].astype(BF16)
        y_off = jnp.dot(cg, state_b, preferred_element_type=F32) * e_cs_e[:, g * gw:(g + 1) * gw]
        for pr in range(pairs_per_group):
            c0 = g * gw + pr * V7X_LANES
            xp = xs_ref[:, c0:c0 + V7X_LANES]
            xpb = xp.astype(BF16)
            acc = y_off[:, pr * V7X_LANES:(pr + 1) * V7X_LANES] + dskip_ref[:, c0:c0 + V7X_LANES] * xp
            for hh in range(2):
                h = c0 // SSM_HEAD_DIM + hh
                d = cs[:, h:h + 1] - cs_t[h:h + 1, :]
                m = cb * jnp.exp(jnp.where(lower, d, NEG)) * dt_t[h:h + 1, :]
                sel = (lane < SSM_HEAD_DIM) if hh == 0 else (lane >= SSM_HEAD_DIM)
                xm = jnp.where(sel, xpb, jnp.zeros_like(xpb))
                acc = acc + jnp.dot(m.astype(BF16), xm, preferred_element_type=F32)
            y_ref[:, c0:c0 + V7X_LANES] = acc
        xw = (xs_ref[:, g * gw:(g + 1) * gw] * w_state_e[:, g * gw:(g + 1) * gw]).astype(BF16)
        new = jnp.dot(bg.T.astype(BF16), xw, preferred_element_type=F32)
        st_ref[g] = st_ref[g] * decay_e[:, g * gw:(g + 1) * gw] + new

    for g in range(SSM_GROUPS):
        sl = slice(g * gw, (g + 1) * gw)
        zz = z_ref[:, sl].astype(F32)
        yz = y_ref[:, sl] * (zz * jax.nn.sigmoid(zz))
        ms = jnp.mean(yz * yz, axis=-1, keepdims=True)
        o_ref[:, sl] = (yz * lax.rsqrt(ms + EPS) * gain_ref[:, sl]).astype(o_ref.dtype)


def _ssd(proj3, dt3, cwx, cbx, cwbc, cbbc, dtb, aneg, dskip_e, gain, e_mat, z_blk, x_blk, bc_blk):
    b, s, _ = proj3.shape
    L = SSD_L
    assert s % L == 0
    inner = SSM_HEADS * SSM_HEAD_DIM
    gs2 = 2 * SSM_GROUPS * SSM_STATE
    const = lambda bi, ci: (0, 0)
    full = lambda a: pl.BlockSpec(a.shape, const)
    return pl.pallas_call(
        functools.partial(_ssd_kernel, L=L),
        grid=(b, s // L),
        in_specs=[
            pl.BlockSpec((None, L, inner), lambda bi, ci: (bi, ci, z_blk)),
            pl.BlockSpec((None, L, inner), lambda bi, ci: (bi, ci, x_blk)),
            pl.BlockSpec((None, L, gs2), lambda bi, ci: (bi, ci, bc_blk)),
            pl.BlockSpec((None, L, V7X_LANES), lambda bi, ci: (bi, ci, 0)),
            full(cwx), full(cbx), full(cwbc), full(cbbc), full(dtb), full(aneg), full(dskip_e),
            full(gain), full(e_mat),
        ],
        out_specs=pl.BlockSpec((None, L, inner), lambda bi, ci: (bi, ci, 0)),
        out_shape=jax.ShapeDtypeStruct((b, s, inner), BF16),
        scratch_shapes=[
            pltpu.VMEM((L + V7X_SUBLANES, inner), F32),
            pltpu.VMEM((L + V7X_SUBLANES, gs2), F32),
            pltpu.VMEM((SSM_GROUPS, SSM_STATE, inner // SSM_GROUPS), F32),
            pltpu.VMEM((L, inner), F32),
            pltpu.VMEM((L, inner), F32),
        ],
        compiler_params=_cparams("parallel", "arbitrary"),
        name="ssd",
    )(proj3, proj3, proj3, dt3, cwx, cbx, cwbc, cbbc, dtb, aneg, dskip_e, gain, e_mat)


def _mem_kv_kernel(mem_ref, g_ref, w_ref, kg_ref, k_ref, v_ref):
    m = mem_ref[...]
    ms = jnp.mean(m * m, axis=-1, keepdims=True)
    mn = (m * lax.rsqrt(ms + EPS) * g_ref[...]).astype(BF16)
    kv = jnp.dot(mn, w_ref[...], preferred_element_type=F32)
    width = X_HEADS * X_HEAD_DIM
    for h in range(X_HEADS):
        sl = slice(h * X_HEAD_DIM, (h + 1) * X_HEAD_DIM)
        kh = kv[:, sl]
        r = lax.rsqrt(jnp.mean(kh * kh, axis=-1, keepdims=True) + EPS)
        k_ref[:, sl] = (kh * r * kg_ref[...]).astype(k_ref.dtype)
    v_ref[...] = kv[:, width:].astype(v_ref.dtype)


def _mem_kv(mem, norm_mem, w_kv, k_gain):
    b, m, d = mem.shape
    width = X_HEADS * X_HEAD_DIM
    const = lambda bi: (0, 0)
    return pl.pallas_call(
        _mem_kv_kernel,
        grid=(b,),
        in_specs=[
            pl.BlockSpec((None, m, d), lambda bi: (bi, 0, 0)),
            pl.BlockSpec((1, d), const),
            pl.BlockSpec((d, 2 * width), const),
            pl.BlockSpec((1, X_HEAD_DIM), const),
        ],
        out_specs=[pl.BlockSpec((None, m, width), lambda bi: (bi, 0, 0))] * 2,
        out_shape=[jax.ShapeDtypeStruct((b, m, width), BF16)] * 2,
        compiler_params=_cparams("parallel"),
        name="mem_kv",
    )(mem, norm_mem, w_kv, k_gain)


def _mem_attn_kernel(q_ref, k_ref, v_ref, o_ref):
    for h in range(X_HEADS):
        sl = slice(h * X_HEAD_DIM, (h + 1) * X_HEAD_DIM)
        s = lax.dot_general(q_ref[:, sl], k_ref[:, sl], (((1,), (1,)), ((), ())),
                            preferred_element_type=F32)
        m = jnp.max(s, axis=-1, keepdims=True)
        p = jnp.exp(s - m)
        l = jnp.sum(p, axis=-1, keepdims=True)
        o = jnp.dot(p.astype(BF16), v_ref[:, sl], preferred_element_type=F32)
        o_ref[:, sl] = (o / l).astype(o_ref.dtype)


def _mem_attn(proj3, k, v, q_blk, tq):
    b, s, _ = proj3.shape
    m = k.shape[1]
    width = X_HEADS * X_HEAD_DIM
    assert s % tq == 0
    return pl.pallas_call(
        _mem_attn_kernel,
        grid=(b, s // tq),
        in_specs=[
            pl.BlockSpec((None, tq, width), lambda bi, qi: (bi, qi, q_blk)),
            pl.BlockSpec((None, m, width), lambda bi, qi: (bi, 0, 0)),
            pl.BlockSpec((None, m, width), lambda bi, qi: (bi, 0, 0)),
        ],
        out_specs=pl.BlockSpec((None, tq, width), lambda bi, qi: (bi, qi, 0)),
        out_shape=jax.ShapeDtypeStruct((b, s, width), BF16),
        compiler_params=_cparams("parallel", "parallel"),
        name="mem_attn",
    )(proj3, k, v)


def _merge_kernel(x_ref, ya_ref, yb_ref, yc_ref, g0_ref, g1_ref, g2_ref, wa_ref, wb_ref, wc_ref,
                  wo_ref, nf_ref, wr_ref, br_ref,
                  x1_ref, h2_ref, idx_ref, wt_ref, cnt_ref, carry_ref, *, tm):
    i = pl.program_id(0)

    @pl.when(i == 0)
    def _():
        carry_ref[...] = jnp.zeros_like(carry_ref)

    merged = (g0_ref[...].astype(F32) * jnp.dot(ya_ref[...], wa_ref[...], preferred_element_type=F32)
              + g1_ref[...].astype(F32) * jnp.dot(yb_ref[...], wb_ref[...], preferred_element_type=F32)
              + g2_ref[...].astype(F32) * jnp.dot(yc_ref[...], wc_ref[...], preferred_element_type=F32))
    x1 = x_ref[...] + jnp.dot(merged.astype(BF16), wo_ref[...], preferred_element_type=F32)
    x1_ref[...] = x1
    ms = jnp.mean(x1 * x1, axis=-1, keepdims=True)
    h2 = x1 * lax.rsqrt(ms + EPS) * nf_ref[...]
    h2_ref[...] = h2

    logits = jnp.dot(h2, wr_ref[...], precision=HIGHEST, preferred_element_type=F32) + br_ref[...]
    lane = lax.broadcasted_iota(jnp.int32, (tm, V7X_LANES), 1)
    lane_f = lane.astype(F32)
    work = jnp.where(lane < N_EXPERTS, logits, NEG)
    sel_idx, sel_val, sel_oh = [], [], []
    for _ in range(TOP_K):
        mval = jnp.max(work, axis=-1, keepdims=True)
        ik = jnp.min(jnp.where(work == mval, lane_f, float(V7X_LANES)), axis=-1, keepdims=True)
        oh = lane_f == ik
        work = jnp.where(oh, NEG, work)
        sel_idx.append(ik)
        sel_val.append(mval)
        sel_oh.append(oh)
    ex = [jnp.exp(v - sel_val[0]) for v in sel_val]
    denom = ex[0] + ex[1] + ex[2] + ex[3]

    oh_all = jnp.zeros((tm, V7X_LANES), F32)
    for oh in sel_oh:
        oh_all = oh_all + jnp.where(oh, 1.0, 0.0)
    row = lax.broadcasted_iota(jnp.int32, (tm, tm), 0)
    colm = lax.broadcasted_iota(jnp.int32, (tm, tm), 1)
    strict = jnp.where(colm < row, 1.0, 0.0).astype(BF16)
    before = jnp.dot(strict, oh_all.astype(BF16), preferred_element_type=F32) + carry_ref[0:1, :]
    idx_out = jnp.zeros((tm, V7X_LANES), F32)
    wt_out = jnp.zeros((tm, V7X_LANES), F32)
    for k in range(TOP_K):
        rank = jnp.sum(jnp.where(sel_oh[k], before, 0.0), axis=-1, keepdims=True)
        idx_out = jnp.where(lane == k, sel_idx[k], idx_out)
        idx_out = jnp.where(lane == TOP_K + k, rank, idx_out)
        wt_out = jnp.where(lane == k, ex[k] / denom, wt_out)
    idx_ref[...] = idx_out.astype(jnp.int32)
    wt_ref[...] = wt_out
    carry_ref[...] = carry_ref[...] + jnp.sum(oh_all, axis=0, keepdims=True)
    cnt_ref[...] = carry_ref[...]


def _merge(x2d, ya, yb, yc, proj, wa, wb, wc, wo, nf, wr, br, gate_blk, tm):
    t, d = x2d.shape
    assert t % tm == 0
    const = lambda i: (0, 0)
    full = lambda a: pl.BlockSpec(a.shape, const)
    rows = lambda w: pl.BlockSpec((tm, w), lambda i: (i, 0))
    return pl.pallas_call(
        functools.partial(_merge_kernel, tm=tm),
        grid=(t // tm,),
        in_specs=[
            rows(d), rows(ya.shape[1]), rows(yb.shape[1]), rows(yc.shape[1]),
            pl.BlockSpec((tm, d), lambda i: (i, gate_blk)),
            pl.BlockSpec((tm, d), lambda i: (i, gate_blk + 1)),
            pl.BlockSpec((tm, d), lambda i: (i, gate_blk + 2)),
            full(wa), full(wb), full(wc), full(wo), full(nf), full(wr), full(br),
        ],
        out_specs=[rows(d), rows(d), rows(V7X_LANES), rows(V7X_LANES),
                   pl.BlockSpec((V7X_SUBLANES, V7X_LANES), const)],
        out_shape=[
            jax.ShapeDtypeStruct((t, d), F32),
            jax.ShapeDtypeStruct((t, d), F32),
            jax.ShapeDtypeStruct((t, V7X_LANES), jnp.int32),
            jax.ShapeDtypeStruct((t, V7X_LANES), F32),
            jax.ShapeDtypeStruct((V7X_SUBLANES, V7X_LANES), F32),
        ],
        scratch_shapes=[pltpu.VMEM((V7X_SUBLANES, V7X_LANES), F32)],
        compiler_params=_cparams("arbitrary"),
        name="merge_route",
    )(x2d, ya, yb, yc, proj, proj, proj, wa, wb, wc, wo, nf, wr, br)


def _row_copy(src, src_row, dst, dst_row, sem):
    return pltpu.make_async_copy(src.at[pl.ds(src_row, 1)], dst.at[pl.ds(dst_row, 1)], sem)


def _dispatch_kernel(dest_ref, pdest_ref, npad_ref, nused_ref, h_ref, xs_ref, zero_ref, sem, *,
                     tm, rows, n_blocks):
    def start(t, carry):
        for k in range(TOP_K):
            _row_copy(h_ref, t, xs_ref, dest_ref[0, t * TOP_K + k], sem).start()
        return carry

    lax.fori_loop(0, tm, start, 0)

    def wait(t, carry):
        for k in range(TOP_K):
            _row_copy(h_ref, 0, xs_ref, 0, sem).wait()
        return carry

    lax.fori_loop(0, tm, wait, 0)

    @pl.when(pl.program_id(0) == pl.num_programs(0) - 1)
    def _():
        zero_ref[...] = jnp.zeros_like(zero_ref)
        n_pad = npad_ref[0]
        n_used = nused_ref[0]

        def block_copy(blk):
            return pltpu.make_async_copy(zero_ref, xs_ref.at[pl.ds(blk * rows, rows)], sem)

        def start_pad(r, carry):
            _row_copy(zero_ref, 0, xs_ref, pdest_ref[r], sem).start()
            return carry

        def wait_pad(r, carry):
            _row_copy(zero_ref, 0, xs_ref, 0, sem).wait()
            return carry

        def start_blk(blk, carry):
            block_copy(blk).start()
            return carry

        def wait_blk(blk, carry):
            block_copy(blk).wait()
            return carry

        lax.fori_loop(0, n_pad, start_pad, 0)
        lax.fori_loop(n_used, n_blocks, start_blk, 0)
        lax.fori_loop(0, n_pad, wait_pad, 0)
        lax.fori_loop(n_used, n_blocks, wait_blk, 0)


def _dispatch(dest3d, pad_dest, n_pad, n_used, h2, n_rows, tm):
    t, d = h2.shape
    rows = MOE_ROWS
    smem = pl.BlockSpec(memory_space=pltpu.SMEM)
    return pl.pallas_call(
        functools.partial(_dispatch_kernel, tm=tm, rows=rows, n_blocks=n_rows // rows),
        grid=(t // tm,),
        in_specs=[
            pl.BlockSpec((None, 1, tm * TOP_K), lambda i: (i, 0, 0), memory_space=pltpu.SMEM),
            smem, smem, smem,
            pl.BlockSpec((tm, d), lambda i: (i, 0)),
        ],
        out_specs=pl.BlockSpec(memory_space=pl.ANY),
        out_shape=jax.ShapeDtypeStruct((n_rows, d), h2.dtype),
        scratch_shapes=[pltpu.VMEM((rows, d), h2.dtype), pltpu.SemaphoreType.DMA],
        compiler_params=_cparams("arbitrary"),
        name="moe_dispatch",
    )(dest3d, pad_dest, n_pad, n_used, h2)


def _experts_kernel(bexp_ref, nused_ref, x_ref, wg_ref, bg_ref, wu_ref, bu_ref, wd_ref, bd_ref, y_ref):
    del bexp_ref

    @pl.when(pl.program_id(0) < nused_ref[0])
    def _():
        xb = x_ref[...].astype(BF16)
        g = jnp.dot(xb, wg_ref[...], preferred_element_type=F32) + bg_ref[...]
        u = jnp.dot(xb, wu_ref[...], preferred_element_type=F32) + bu_ref[...]
        g = jnp.minimum(g, SWIGLU_LIMIT)
        u = jnp.clip(u, -SWIGLU_LIMIT, SWIGLU_LIMIT)
        act = (u + 1.0) * g * jax.nn.sigmoid(SWIGLU_ALPHA * g)
        y_ref[...] = jnp.dot(act.astype(BF16), wd_ref[...], preferred_element_type=F32) + bd_ref[...]

    @pl.when(pl.program_id(0) >= nused_ref[0])
    def _():
        y_ref[...] = jnp.zeros_like(y_ref)


def _experts(block_exp, n_used, xs, wg, bg, wu, bu, wd, bd):
    n_rows, d = xs.shape
    de = wg.shape[2]
    rows = MOE_ROWS
    n_blocks = n_rows // rows
    xmap = lambda b, be, nu: (jnp.minimum(b, nu[0] - 1), 0)
    wmap = lambda b, be, nu: (be[b], 0, 0)
    return pl.pallas_call(
        _experts_kernel,
        grid_spec=pltpu.PrefetchScalarGridSpec(
            num_scalar_prefetch=2,
            grid=(n_blocks,),
            in_specs=[
                pl.BlockSpec((rows, d), xmap),
                pl.BlockSpec((None, d, de), wmap), pl.BlockSpec((None, 1, de), wmap),
                pl.BlockSpec((None, d, de), wmap), pl.BlockSpec((None, 1, de), wmap),
                pl.BlockSpec((None, de, d), wmap), pl.BlockSpec((None, 1, d), wmap),
            ],
            out_specs=pl.BlockSpec((rows, d), lambda b, be, nu: (b, 0)),
        ),
        out_shape=jax.ShapeDtypeStruct((n_rows, d), F32),
        compiler_params=_cparams("arbitrary"),
        name="moe_experts",
    )(block_exp, n_used, xs, wg, bg, wu, bu, wd, bd)


def _combine_kernel(dest_ref, wt_ref, x1_ref, yb_ref, o_ref, buf_ref, sem, *, tm):
    def start(t, carry):
        for k in range(TOP_K):
            _row_copy(yb_ref, dest_ref[0, t * TOP_K + k], buf_ref.at[k], t, sem).start()
        return carry

    lax.fori_loop(0, tm, start, 0)

    def wait(t, carry):
        for k in range(TOP_K):
            _row_copy(yb_ref, 0, buf_ref.at[k], 0, sem).wait()
        return carry

    lax.fori_loop(0, tm, wait, 0)
    acc = x1_ref[...]
    wt = wt_ref[...]
    for k in range(TOP_K):
        acc = acc + wt[:, k:k + 1] * buf_ref[k]
    o_ref[...] = acc


def _combine(dest2d, wts, x1, yb, tm):
    t, d = x1.shape
    return pl.pallas_call(
        functools.partial(_combine_kernel, tm=tm),
        grid=(t // tm,),
        in_specs=[
            pl.BlockSpec((None, 1, tm * TOP_K), lambda i: (i, 0, 0), memory_space=pltpu.SMEM),
            pl.BlockSpec((tm, V7X_LANES), lambda i: (i, 0)),
            pl.BlockSpec((tm, d), lambda i: (i, 0)),
            pl.BlockSpec(memory_space=pl.ANY),
        ],
        out_specs=pl.BlockSpec((tm, d), lambda i: (i, 0)),
        out_shape=jax.ShapeDtypeStruct((t, d), F32),
        scratch_shapes=[pltpu.VMEM((TOP_K, tm, d), F32), pltpu.SemaphoreType.DMA],
        compiler_params=_cparams("arbitrary"),
        name="moe_combine",
    )(dest2d, wts, x1, yb)


def _head_indicator(width, head_dim):
    lane_head = jnp.arange(width) // head_dim
    return (lane_head[:, None] == jnp.arange(V7X_LANES)[None, :]).astype(BF16)


def _layer(x, mem, norm_mix, w_in, a_q_gain, a_k_gain, a_rel_bias, conv_w, conv_b, dt_bias, a_log,
           d_skip, ssm_norm, norm_mem, w_mem_kv, x_q_gain, x_k_gain, w_br_a, w_br_b, w_br_c, w_out,
           norm_ffn, w_router, b_router, w_gate, b_gate, w_up, b_up, w_down, b_down):
    b, s, d = x.shape
    t = b * s
    a_width = A_HEADS * A_HEAD_DIM
    inner = SSM_HEADS * SSM_HEAD_DIM
    gs = SSM_GROUPS * SSM_STATE
    x_width = X_HEADS * X_HEAD_DIM
    assert d == a_width == x_width and inner == 2 * d and 2 * gs == d

    o_q, o_k, o_v = 0, a_width, 2 * a_width
    o_z = 3 * a_width
    o_xbc = o_z + inner
    o_dt = o_xbc + inner + 2 * gs
    o_qx = o_dt + SSM_HEADS
    o_gate = o_qx + x_width
    cols = lambda o, w: w_in[:, o:o + w]
    w_main = jnp.concatenate([
        cols(o_z, inner), cols(o_xbc, inner), cols(o_q, a_width), cols(o_k, a_width),
        cols(o_v, a_width), cols(o_xbc + inner, 2 * gs), cols(o_qx, x_width), cols(o_gate, 3 * d),
    ], axis=1).astype(BF16)
    roles = ("plain", "plain", "plain", "plain", "qa", "ka", "plain", "plain", "qx", "sig", "sig", "sig")
    w_dt = jnp.pad(cols(o_dt, SSM_HEADS), ((0, 0), (0, V7X_LANES - SSM_HEADS))).astype(BF16)
    ind_a = _head_indicator(a_width, A_HEAD_DIM)
    ind_x = _head_indicator(x_width, X_HEAD_DIM)
    gains = jnp.zeros((V7X_SUBLANES, d), F32)
    gains = gains.at[0].set(jnp.tile(a_q_gain, A_HEADS) * A_HEAD_DIM ** -0.5)
    gains = gains.at[1].set(jnp.tile(a_k_gain, A_HEADS))
    gains = gains.at[2].set(jnp.tile(x_q_gain, X_HEADS) * X_HEAD_DIM ** -0.5)

    x2d = x.reshape(t, d)
    proj, dt_raw = _in_proj(x2d, norm_mix.reshape(1, d), w_main, w_dt, ind_a, ind_a.T, ind_x, ind_x.T,
                            gains, roles, tm=min(1024, t), tn=d)
    proj3 = proj.reshape(b, s, proj.shape[1])
    lane_blk = d // V7X_LANES

    y_a = _attention(proj3, _attn_bias(a_rel_bias), q_col=4 * lane_blk, k_col=5 * lane_blk,
                     v_col=6 * lane_blk)

    pad_h = lambda v: jnp.pad(v.astype(F32), (0, V7X_LANES - SSM_HEADS)).reshape(1, V7X_LANES)
    e_mat = _head_indicator(inner, SSM_HEAD_DIM).T
    y_b = _ssd(proj3, dt_raw.reshape(b, s, V7X_LANES),
               conv_w[:, :inner], conv_b[:inner].reshape(1, inner),
               conv_w[:, inner:], conv_b[inner:].reshape(1, 2 * gs),
               pad_h(dt_bias), pad_h(-jnp.exp(a_log.astype(F32))),
               jnp.repeat(d_skip.astype(F32), SSM_HEAD_DIM).reshape(1, inner),
               ssm_norm.reshape(1, inner), e_mat, z_blk=0, x_blk=1, bc_blk=7)

    k_mem, v_mem = _mem_kv(mem, norm_mem.reshape(1, d), w_mem_kv.astype(BF16),
                           x_k_gain.reshape(1, X_HEAD_DIM))
    y_c = _mem_attn(proj3, k_mem, v_mem, q_blk=8, tq=min(512, s))

    w_r = jnp.pad(w_router, ((0, 0), (0, V7X_LANES - N_EXPERTS)))
    b_r = jnp.pad(b_router, (0, V7X_LANES - N_EXPERTS)).reshape(1, V7X_LANES)
    x1, h2, idx_rank, wts, counts = _merge(
        x2d, y_a.reshape(t, a_width), y_b.reshape(t, inner), y_c.reshape(t, x_width), proj,
        w_br_a.astype(BF16), w_br_b.astype(BF16), w_br_c.astype(BF16), w_out.astype(BF16),
        norm_ffn.reshape(1, d), w_r, b_r, gate_blk=9, tm=min(512, t))

    rows = MOE_ROWS
    n_rows = t * TOP_K + N_EXPERTS * rows
    n_blocks = n_rows // rows
    cnt = counts[0, :N_EXPERTS].astype(jnp.int32)
    padded = (cnt + rows - 1) // rows * rows
    pad_ends = jnp.cumsum(padded)
    pad_starts = pad_ends - padded
    e_idx = idx_rank[:, :TOP_K]
    dest = pad_starts[e_idx] + idx_rank[:, TOP_K:2 * TOP_K]
    n_used = pad_ends[-1] // rows
    blk = jnp.minimum(jnp.arange(n_blocks, dtype=jnp.int32), n_used - 1)
    block_exp = jnp.minimum(jnp.searchsorted(pad_ends, blk * rows, side="right"),
                            N_EXPERTS - 1).astype(jnp.int32)
    r = jnp.arange(N_EXPERTS * rows, dtype=jnp.int32)
    pe, pr = r // rows, r % rows
    is_pad = pr < (padded - cnt)[pe]
    pad_row = (pad_starts + cnt)[pe] + pr
    order = jnp.argsort(jnp.logical_not(is_pad), stable=True)
    pad_dest = jnp.where(is_pad, pad_row, 0)[order].astype(jnp.int32)
    n_pad = jnp.sum(is_pad).astype(jnp.int32).reshape(1)

    tm_moe = min(256, t)
    dest2d = dest.reshape(t // tm_moe, 1, tm_moe * TOP_K).astype(jnp.int32)
    n_used = n_used.reshape(1).astype(jnp.int32)
    xs = _dispatch(dest2d, pad_dest, n_pad, n_used, h2, n_rows, tm_moe)
    yb = _experts(block_exp, n_used, xs,
                  w_gate.astype(BF16), b_gate.reshape(N_EXPERTS, 1, -1),
                  w_up.astype(BF16), b_up.reshape(N_EXPERTS, 1, -1),
                  w_down.astype(BF16), b_down.reshape(N_EXPERTS, 1, -1))
    out = _combine(dest2d, wts, x1, yb, tm_moe)
    return out.reshape(b, s, d)


def kernel(x, mem, norm_mix, w_in, a_q_gain, a_k_gain, a_rel_bias, conv_w, conv_b, dt_bias, a_log, d_skip, ssm_norm, norm_mem, w_mem_kv, x_q_gain, x_k_gain, w_br_a, w_br_b, w_br_c, w_out, norm_ffn, w_router, b_router, w_gate, b_gate, w_up, b_up, w_down, b_down):
    for l in range(norm_mix.shape[0]):
        x = _layer(x, mem, norm_mix[l], w_in[l], a_q_gain[l], a_k_gain[l], a_rel_bias[l], conv_w[l],
                   conv_b[l], dt_bias[l], a_log[l], d_skip[l], ssm_norm[l], norm_mem[l], w_mem_kv[l],
                   x_q_gain[l], x_k_gain[l], w_br_a[l], w_br_b[l], w_br_c[l], w_out[l], norm_ffn[l],
                   w_router[l], b_router[l], w_gate[l], b_gate[l], w_up[l], b_up[l], w_down[l],
                   b_down[l])
    return x
```

```python
import functools

import jax
import jax.numpy as jnp
from jax import lax
from jax.experimental import pallas as pl
from jax.experimental.pallas import tpu as pltpu

F32 = jnp.float32
BF16 = jnp.bfloat16
HIGHEST = lax.Precision.HIGHEST

V7X_LANES = 128
V7X_SUBLANES = 8
V7X_VMEM_LIMIT_BYTES = 56 * 1024 * 1024

EPS = 1e-6
NEG = -1e30

CHUNK = 64
A_HEADS = 16
A_HEAD_DIM = 64
LEFT_CHUNKS = 8
REL_CLIP = 128
SSM_HEADS = 32
SSM_HEAD_DIM = 64
SSM_GROUPS = 4
SSM_STATE = 128
CONV_WIDTH = 4
X_HEADS = 4
X_HEAD_DIM = 256
N_EXPERTS = 32
TOP_K = 4
SWIGLU_LIMIT = 7.0
SWIGLU_ALPHA = 1.702

ATTN_TQ = 256
SSD_L = 256
MOE_ROWS = 256
MOE_TILE = 256


def _cparams(*sem):
    return pltpu.CompilerParams(dimension_semantics=sem, vmem_limit_bytes=V7X_VMEM_LIMIT_BYTES)


def _split_bf16(v):
    hi = v.astype(BF16)
    lo = (v - hi.astype(F32)).astype(BF16)
    return hi, lo


def _in_proj_kernel(x_ref, nw_ref, w_ref, wdt_ref, ind_a_ref, ind_at_ref, ind_x_ref, ind_xt_ref,
                    gains_ref, o_ref, dt_ref, h_ref, *, roles):
    j = pl.program_id(1)

    @pl.when(j == 0)
    def _():
        x = x_ref[...]
        ms = jnp.mean(x * x, axis=-1, keepdims=True)
        hb = (x * lax.rsqrt(ms + EPS) * nw_ref[...]).astype(BF16)
        h_ref[...] = hb
        dt_ref[...] = jnp.dot(hb, wdt_ref[...], preferred_element_type=F32)

    acc = jnp.dot(h_ref[...], w_ref[...], preferred_element_type=F32)

    def head_norm(ind_ref, indt_ref, head_dim, gain_row):
        s = jnp.dot((acc * acc).astype(BF16), ind_ref[...], preferred_element_type=F32)
        r = lax.rsqrt(s * (1.0 / head_dim) + EPS)
        r_hi, r_lo = _split_bf16(r)
        rexp = (jnp.dot(r_hi, indt_ref[...], preferred_element_type=F32)
                + jnp.dot(r_lo, indt_ref[...], preferred_element_type=F32))
        return acc * rexp * gain_row

    def cond_for(role):
        c = None
        for jj, r in enumerate(roles):
            if r == role:
                c = (j == jj) if c is None else jnp.logical_or(c, j == jj)
        return c

    for role in sorted(set(roles)):
        @pl.when(cond_for(role))
        def _(role=role):
            if role == "qa":
                out = head_norm(ind_a_ref, ind_at_ref, A_HEAD_DIM, gains_ref[0:1, :])
            elif role == "ka":
                out = head_norm(ind_a_ref, ind_at_ref, A_HEAD_DIM, gains_ref[1:2, :])
            elif role == "qx":
                out = head_norm(ind_x_ref, ind_xt_ref, X_HEAD_DIM, gains_ref[2:3, :])
            elif role == "sig":
                out = jax.nn.sigmoid(acc)
            else:
                out = acc
            o_ref[...] = out.astype(o_ref.dtype)


def _in_proj(x2d, norm_w, w_main, w_dt, ind_a, ind_at, ind_x, ind_xt, gains, roles, tm, tn):
    t, d = x2d.shape
    n = w_main.shape[1]
    assert t % tm == 0 and n % tn == 0 and len(roles) == n // tn
    const = lambda i, j: (0, 0)
    return pl.pallas_call(
        functools.partial(_in_proj_kernel, roles=roles),
        grid=(t // tm, n // tn),
        in_specs=[
            pl.BlockSpec((tm, d), lambda i, j: (i, 0)),
            pl.BlockSpec((1, d), const),
            pl.BlockSpec((d, tn), lambda i, j: (0, j)),
            pl.BlockSpec((d, V7X_LANES), const),
            pl.BlockSpec(ind_a.shape, const),
            pl.BlockSpec(ind_at.shape, const),
            pl.BlockSpec(ind_x.shape, const),
            pl.BlockSpec(ind_xt.shape, const),
            pl.BlockSpec(gains.shape, const),
        ],
        out_specs=[
            pl.BlockSpec((tm, tn), lambda i, j: (i, j)),
            pl.BlockSpec((tm, V7X_LANES), lambda i, j: (i, 0)),
        ],
        out_shape=[
            jax.ShapeDtypeStruct((t, n), BF16),
            jax.ShapeDtypeStruct((t, V7X_LANES), F32),
        ],
        scratch_shapes=[pltpu.VMEM((tm, d), BF16)],
        compiler_params=_cparams("parallel", "arbitrary"),
        name="in_proj",
    )(x2d, norm_w, w_main, w_dt, ind_a, ind_at, ind_x, ind_xt, gains)


def _attn_kernel(q_ref, *refs, tq, nprev):
    k_refs = refs[:nprev + 1]
    v_refs = refs[nprev + 1:2 * nprev + 2]
    bias_ref, o_ref = refs[2 * nprev + 2:]
    qb = pl.program_id(2)
    q2 = q_ref[...]
    kk = jnp.concatenate([r[...] for r in k_refs], axis=0)
    vv = jnp.concatenate([r[...] for r in v_refs], axis=0)
    nk = (nprev + 1) * tq
    lane = lax.broadcasted_iota(jnp.int32, (1, V7X_LANES), 1)
    col = lax.broadcasted_iota(jnp.int32, (1, nk), 1)
    before_start = col < (nprev - qb) * tq
    outs = []
    for hh in range(2):
        sel = (lane < A_HEAD_DIM) if hh == 0 else (lane >= A_HEAD_DIM)
        qm = jnp.where(sel, q2, jnp.zeros_like(q2))
        s = lax.dot_general(qm, kk, (((1,), (1,)), ((), ())), preferred_element_type=F32)
        s = jnp.where(before_start, NEG, s + bias_ref[hh])
        m = jnp.max(s, axis=-1, keepdims=True)
        p = jnp.exp(s - m)
        l = jnp.sum(p, axis=-1, keepdims=True)
        o = jnp.dot(p.astype(BF16), vv, preferred_element_type=F32)
        outs.append(o / l)
    o_ref[...] = jnp.where(lane < A_HEAD_DIM, outs[0], outs[1]).astype(o_ref.dtype)


def _attention(proj3, bias, q_col, k_col, v_col):
    b, s, _ = proj3.shape
    tq = ATTN_TQ
    left = LEFT_CHUNKS * CHUNK
    assert left % tq == 0 and s % tq == 0
    nprev = left // tq
    npairs = A_HEADS // 2

    def kv_spec(col0, back):
        return pl.BlockSpec((None, tq, V7X_LANES),
                            lambda hp, bi, qi: (bi, jnp.maximum(qi - back, 0), col0 + hp))

    in_specs = [pl.BlockSpec((None, tq, V7X_LANES), lambda hp, bi, qi: (bi, qi, q_col + hp))]
    in_specs += [kv_spec(k_col, nprev - i) for i in range(nprev + 1)]
    in_specs += [kv_spec(v_col, nprev - i) for i in range(nprev + 1)]
    in_specs += [pl.BlockSpec((2, tq, left + tq), lambda hp, bi, qi: (hp, 0, 0))]
    return pl.pallas_call(
        functools.partial(_attn_kernel, tq=tq, nprev=nprev),
        grid=(npairs, b, s // tq),
        in_specs=in_specs,
        out_specs=pl.BlockSpec((None, tq, V7X_LANES), lambda hp, bi, qi: (bi, qi, hp)),
        out_shape=jax.ShapeDtypeStruct((b, s, A_HEADS * A_HEAD_DIM), BF16),
        compiler_params=_cparams("parallel", "parallel", "parallel"),
        name="chunk_attn",
    )(proj3, *([proj3] * (2 * nprev + 2)), bias)


def _attn_bias(rel_bias):
    tq = ATTN_TQ
    left = LEFT_CHUNKS * CHUNK
    i = jnp.arange(tq)[:, None]
    j = jnp.arange(left + tq)[None, :]
    m_len = tq + left + tq
    m = jnp.arange(m_len)
    diff = jnp.where(m < left + tq, m, m - m_len)
    v = rel_bias[:, jnp.clip(left - diff, -REL_CLIP, REL_CLIP) + REL_CLIP].astype(F32)
    h = v.shape[0]
    bias = jnp.tile(v, (1, tq))[:, :tq * (m_len - 1)].reshape(h, tq, m_len - 1)[:, :, :left + tq]
    qc, kc = i // CHUNK, j // CHUNK
    in_band = (kc >= qc) & (kc <= qc + LEFT_CHUNKS)
    return jnp.where(in_band[None], bias, NEG)


def _ssd_kernel(z_ref, x_ref, bc_ref, dt_ref, cwx_ref, cbx_ref, cwbc_ref, cbbc_ref, dtb_ref,
                aneg_ref, dskip_ref, gain_ref, e_ref, o_ref,
                xf_ref, bcf_ref, st_ref, xs_ref, y_ref, *, L):
    c = pl.program_id(1)
    inner = SSM_HEADS * SSM_HEAD_DIM
    gw = inner // SSM_GROUPS
    gs = SSM_GROUPS * SSM_STATE
    tail = V7X_SUBLANES

    @pl.when(c == 0)
    def _():
        xf_ref[0:tail, :] = jnp.zeros((tail, inner), F32)
        bcf_ref[0:tail, :] = jnp.zeros((tail, 2 * gs), F32)
        st_ref[...] = jnp.zeros_like(st_ref)

    xf_ref[tail:, :] = x_ref[...].astype(F32)
    bcf_ref[tail:, :] = bc_ref[...].astype(F32)

    def conv_silu(src_ref, w_ref, b_ref, c0, c1):
        acc = b_ref[:, c0:c1] + w_ref[CONV_WIDTH - 1:CONV_WIDTH, c0:c1] * src_ref[tail:tail + L, c0:c1]
        for k in range(1, CONV_WIDTH):
            acc = acc + (w_ref[CONV_WIDTH - 1 - k:CONV_WIDTH - k, c0:c1]
                         * src_ref[tail - k:tail - k + L, c0:c1])
        return acc * jax.nn.sigmoid(acc)

    for g in range(SSM_GROUPS):
        xs_ref[:, g * gw:(g + 1) * gw] = conv_silu(xf_ref, cwx_ref, cbx_ref, g * gw, (g + 1) * gw)
    bmat = conv_silu(bcf_ref, cwbc_ref, cbbc_ref, 0, gs)
    cmat = conv_silu(bcf_ref, cwbc_ref, cbbc_ref, gs, 2 * gs)
    xf_ref[0:tail, :] = xf_ref[L:L + tail, :]
    bcf_ref[0:tail, :] = bcf_ref[L:L + tail, :]

    pre = dt_ref[...] + dtb_ref[...]
    dt = jnp.maximum(pre, 0.0) + jnp.log1p(jnp.exp(-jnp.abs(pre)))
    a = dt * aneg_ref[...]
    row = lax.broadcasted_iota(jnp.int32, (L, L), 0)
    colm = lax.broadcasted_iota(jnp.int32, (L, L), 1)
    lower = colm <= row
    tri = jnp.where(lower, 1.0, 0.0).astype(F32)
    cs = jnp.dot(tri, a, precision=HIGHEST, preferred_element_type=F32)
    cs_t = cs.T
    dt_t = dt.T
    cs_last = cs[L - 1:L, :]
    w_state = dt * jnp.exp(cs_last - cs)
    e_cs = jnp.exp(cs)
    chunk_decay = jnp.broadcast_to(jnp.exp(cs_last), (tail, V7X_LANES))
    stacked = jnp.concatenate([w_state, e_cs, chunk_decay], axis=0)
    s_hi, s_lo = _split_bf16(stacked)
    expanded = (jnp.dot(s_hi, e_ref[...], preferred_element_type=F32)
                + jnp.dot(s_lo, e_ref[...], preferred_element_type=F32))
    w_state_e = expanded[0:L]
    e_cs_e = expanded[L:2 * L]
    decay_e = expanded[2 * L:2 * L + 1]

    lane = lax.broadcasted_iota(jnp.int32, (1, V7X_LANES), 1)
    pairs_per_group = gw // V7X_LANES
    for g in range(SSM_GROUPS):
        bg = bmat[:, g * SSM_STATE:(g + 1) * SSM_STATE]
        cg = cmat[:, g * SSM_STATE:(g + 1) * SSM_STATE].astype(BF16)
        cb = lax.dot_general(cg, bg.astype(BF16), (((1,), (1,)), ((), ())),
                             preferred_element_type=F32)
        state_b = st_ref[g].astype(BF16)
        y_off = jnp.dot(cg, state_b, preferred_element_type=F32) * e_cs_e[:, g * gw:(g + 1) * gw]
        for pr in range(pairs_per_group):
            c0 = g * gw + pr * V7X_LANES
            xp = xs_ref[:, c0:c0 + V7X_LANES]
            xpb = xp.astype(BF16)
            acc = y_off[:, pr * V7X_LANES:(pr + 1) * V7X_LANES] + dskip_ref[:, c0:c0 + V7X_LANES] * xp
            for hh in range(2):
                h = c0 // SSM_HEAD_DIM + hh
                d = cs[:, h:h + 1] - cs_t[h:h + 1, :]
                m = cb * jnp.exp(jnp.where(lower, d, NEG)) * dt_t[h:h + 1, :]
                sel = (lane < SSM_HEAD_DIM) if hh == 0 else (lane >= SSM_HEAD_DIM)
                xm = jnp.where(sel, xpb, jnp.zeros_like(xpb))
                acc = acc + jnp.dot(m.astype(BF16), xm, preferred_element_type=F32)
            y_ref[:, c0:c0 + V7X_LANES] = acc
        xw = (xs_ref[:, g * gw:(g + 1) * gw] * w_state_e[:, g * gw:(g + 1) * gw]).astype(BF16)
        new = jnp.dot(bg.T.astype(BF16), xw, preferred_element_type=F32)
        st_ref[g] = st_ref[g] * decay_e[:, g * gw:(g + 1) * gw] + new

    for g in range(SSM_GROUPS):
        sl = slice(g * gw, (g + 1) * gw)
        zz = z_ref[:, sl].astype(F32)
        yz = y_ref[:, sl] * (zz * jax.nn.sigmoid(zz))
        ms = jnp.mean(yz * yz, axis=-1, keepdims=True)
        o_ref[:, sl] = (yz * lax.rsqrt(ms + EPS) * gain_ref[:, sl]).astype(o_ref.dtype)


def _ssd(proj3, dt3, cwx, cbx, cwbc, cbbc, dtb, aneg, dskip_e, gain, e_mat, z_blk, x_blk, bc_blk):
    b, s, _ = proj3.shape
    L = SSD_L
    assert s % L == 0
    inner = SSM_HEADS * SSM_HEAD_DIM
    gs2 = 2 * SSM_GROUPS * SSM_STATE
    const = lambda bi, ci: (0, 0)
    full = lambda a: pl.BlockSpec(a.shape, const)
    return pl.pallas_call(
        functools.partial(_ssd_kernel, L=L),
        grid=(b, s // L),
        in_specs=[
            pl.BlockSpec((None, L, inner), lambda bi, ci: (bi, ci, z_blk)),
            pl.BlockSpec((None, L, inner), lambda bi, ci: (bi, ci, x_blk)),
            pl.BlockSpec((None, L, gs2), lambda bi, ci: (bi, ci, bc_blk)),
            pl.BlockSpec((None, L, V7X_LANES), lambda bi, ci: (bi, ci, 0)),
            full(cwx), full(cbx), full(cwbc), full(cbbc), full(dtb), full(aneg), full(dskip_e),
            full(gain), full(e_mat),
        ],
        out_specs=pl.BlockSpec((None, L, inner), lambda bi, ci: (bi, ci, 0)),
        out_shape=jax.ShapeDtypeStruct((b, s, inner), BF16),
        scratch_shapes=[
            pltpu.VMEM((L + V7X_SUBLANES, inner), F32),
            pltpu.VMEM((L + V7X_SUBLANES, gs2), F32),
            pltpu.VMEM((SSM_GROUPS, SSM_STATE, inner // SSM_GROUPS), F32),
            pltpu.VMEM((L, inner), F32),
            pltpu.VMEM((L, inner), F32),
        ],
        compiler_params=_cparams("parallel", "arbitrary"),
        name="ssd",
    )(proj3, proj3, proj3, dt3, cwx, cbx, cwbc, cbbc, dtb, aneg, dskip_e, gain, e_mat)


def _mem_kv_kernel(mem_ref, g_ref, w_ref, kg_ref, k_ref, v_ref):
    m = mem_ref[...]
    ms = jnp.mean(m * m, axis=-1, keepdims=True)
    mn = (m * lax.rsqrt(ms + EPS) * g_ref[...]).astype(BF16)
    kv = jnp.dot(mn, w_ref[...], preferred_element_type=F32)
    width = X_HEADS * X_HEAD_DIM
    for h in range(X_HEADS):
        sl = slice(h * X_HEAD_DIM, (h + 1) * X_HEAD_DIM)
        kh = kv[:, sl]
        r = lax.rsqrt(jnp.mean(kh * kh, axis=-1, keepdims=True) + EPS)
        k_ref[:, sl] = (kh * r * kg_ref[...]).astype(k_ref.dtype)
    v_ref[...] = kv[:, width:].astype(v_ref.dtype)


def _mem_kv(mem, norm_mem, w_kv, k_gain):
    b, m, d = mem.shape
    width = X_HEADS * X_HEAD_DIM
    const = lambda bi: (0, 0)
    return pl.pallas_call(
        _mem_kv_kernel,
        grid=(b,),
        in_specs=[
            pl.BlockSpec((None, m, d), lambda bi: (bi, 0, 0)),
            pl.BlockSpec((1, d), const),
            pl.BlockSpec((d, 2 * width), const),
            pl.BlockSpec((1, X_HEAD_DIM), const),
        ],
        out_specs=[pl.BlockSpec((None, m, width), lambda bi: (bi, 0, 0))] * 2,
        out_shape=[jax.ShapeDtypeStruct((b, m, width), BF16)] * 2,
        compiler_params=_cparams("parallel"),
        name="mem_kv",
    )(mem, norm_mem, w_kv, k_gain)


def _mem_attn_kernel(q_ref, k_ref, v_ref, o_ref):
    for h in range(X_HEADS):
        sl = slice(h * X_HEAD_DIM, (h + 1) * X_HEAD_DIM)
        s = lax.dot_general(q_ref[:, sl], k_ref[:, sl], (((1,), (1,)), ((), ())),
                            preferred_element_type=F32)
        m = jnp.max(s, axis=-1, keepdims=True)
        p = jnp.exp(s - m)
        l = jnp.sum(p, axis=-1, keepdims=True)
        o = jnp.dot(p.astype(BF16), v_ref[:, sl], preferred_element_type=F32)
        o_ref[:, sl] = (o / l).astype(o_ref.dtype)


def _mem_attn(proj3, k, v, q_blk, tq):
    b, s, _ = proj3.shape
    m = k.shape[1]
    width = X_HEADS * X_HEAD_DIM
    assert s % tq == 0
    return pl.pallas_call(
        _mem_attn_kernel,
        grid=(b, s // tq),
        in_specs=[
            pl.BlockSpec((None, tq, width), lambda bi, qi: (bi, qi, q_blk)),
            pl.BlockSpec((None, m, width), lambda bi, qi: (bi, 0, 0)),
            pl.BlockSpec((None, m, width), lambda bi, qi: (bi, 0, 0)),
        ],
        out_specs=pl.BlockSpec((None, tq, width), lambda bi, qi: (bi, qi, 0)),
        out_shape=jax.ShapeDtypeStruct((b, s, width), BF16),
        compiler_params=_cparams("parallel", "parallel"),
        name="mem_attn",
    )(proj3, k, v)


def _merge_kernel(x_ref, ya_ref, yb_ref, yc_ref, g0_ref, g1_ref, g2_ref, wa_ref, wb_ref, wc_ref,
                  wo_ref, nf_ref, wr_ref, br_ref,
                  x1_ref, h2_ref, route_ref, cnt_ref, *, tm):
    merged = (g0_ref[...].astype(F32) * jnp.dot(ya_ref[...], wa_ref[...], preferred_element_type=F32)
              + g1_ref[...].astype(F32) * jnp.dot(yb_ref[...], wb_ref[...], preferred_element_type=F32)
              + g2_ref[...].astype(F32) * jnp.dot(yc_ref[...], wc_ref[...], preferred_element_type=F32))
    x1 = x_ref[...] + jnp.dot(merged.astype(BF16), wo_ref[...], preferred_element_type=F32)
    x1_ref[...] = x1
    ms = jnp.mean(x1 * x1, axis=-1, keepdims=True)
    h2 = x1 * lax.rsqrt(ms + EPS) * nf_ref[...]
    h2_ref[...] = h2.astype(h2_ref.dtype)

    logits = jnp.dot(h2, wr_ref[...], precision=HIGHEST, preferred_element_type=F32) + br_ref[...]
    lane = lax.broadcasted_iota(jnp.int32, (tm, V7X_LANES), 1)
    lane_f = lane.astype(F32)
    work = jnp.where(lane < N_EXPERTS, logits, NEG)
    sel_val, sel_oh = [], []
    for _ in range(TOP_K):
        mval = jnp.max(work, axis=-1, keepdims=True)
        ik = jnp.min(jnp.where(work == mval, lane_f, float(V7X_LANES)), axis=-1, keepdims=True)
        oh = lane_f == ik
        work = jnp.where(oh, NEG, work)
        sel_val.append(mval)
        sel_oh.append(oh)
    ex = [jnp.exp(v - sel_val[0]) for v in sel_val]
    denom = ex[0] + ex[1] + ex[2] + ex[3]

    oh_all = jnp.zeros((tm, V7X_LANES), F32)
    for oh in sel_oh:
        oh_all = oh_all + jnp.where(oh, 1.0, 0.0)
    row = lax.broadcasted_iota(jnp.int32, (tm, tm), 0)
    colm = lax.broadcasted_iota(jnp.int32, (tm, tm), 1)
    strict = jnp.where(colm < row, 1.0, 0.0).astype(BF16)
    before = jnp.dot(strict, oh_all.astype(BF16), preferred_element_type=F32)
    cnt = jnp.sum(oh_all, axis=0, keepdims=True)
    cnt8 = jnp.floor((cnt + (V7X_SUBLANES - 1.0)) * (1.0 / V7X_SUBLANES)) * V7X_SUBLANES
    cnt8 = jnp.broadcast_to(cnt8, (V7X_SUBLANES, V7X_LANES))
    er = lax.broadcasted_iota(jnp.int32, (V7X_LANES, V7X_LANES), 0)
    ec = lax.broadcasted_iota(jnp.int32, (V7X_LANES, V7X_LANES), 1)
    earlier = jnp.where(er < ec, 1.0, 0.0).astype(BF16)
    run_start = jnp.dot(cnt8.astype(BF16), earlier, preferred_element_type=F32)[0:1, :]
    slot = before + run_start
    route = jnp.zeros((tm, V7X_LANES), F32)
    for k in range(TOP_K):
        pos = jnp.sum(jnp.where(sel_oh[k], slot, 0.0), axis=-1, keepdims=True)
        route = jnp.where(lane == k, pos, route)
        route = jnp.where(lane == TOP_K + k, ex[k] / denom, route)
    route_ref[...] = route
    cnt_ref[...] = cnt8


def _merge(x2d, ya, yb, yc, proj, wa, wb, wc, wo, nf, wr, br, gate_blk, tm):
    t, d = x2d.shape
    assert t % tm == 0
    const = lambda i: (0, 0)
    full = lambda a: pl.BlockSpec(a.shape, const)
    rows = lambda w: pl.BlockSpec((tm, w), lambda i: (i, 0))
    return pl.pallas_call(
        functools.partial(_merge_kernel, tm=tm),
        grid=(t // tm,),
        in_specs=[
            rows(d), rows(ya.shape[1]), rows(yb.shape[1]), rows(yc.shape[1]),
            pl.BlockSpec((tm, d), lambda i: (i, gate_blk)),
            pl.BlockSpec((tm, d), lambda i: (i, gate_blk + 1)),
            pl.BlockSpec((tm, d), lambda i: (i, gate_blk + 2)),
            full(wa), full(wb), full(wc), full(wo), full(nf), full(wr), full(br),
        ],
        out_specs=[rows(d), rows(d), rows(V7X_LANES),
                   pl.BlockSpec((None, V7X_SUBLANES, V7X_LANES), lambda i: (i, 0, 0))],
        out_shape=[
            jax.ShapeDtypeStruct((t, d), F32),
            jax.ShapeDtypeStruct((t, d), BF16),
            jax.ShapeDtypeStruct((t, V7X_LANES), F32),
            jax.ShapeDtypeStruct((t // tm, V7X_SUBLANES, V7X_LANES), F32),
        ],
        compiler_params=_cparams("parallel"),
        name="merge_route",
    )(x2d, ya, yb, yc, proj, proj, proj, wa, wb, wc, wo, nf, wr, br)


RUN = V7X_SUBLANES
META_SRC = N_EXPERTS
META_DST = 2 * N_EXPERTS


def _run_pieces(meta_ref, body):
    def per_expert(e, carry):
        def per_piece(c, carry2):
            body(e, c)
            return carry2
        return lax.fori_loop(0, meta_ref[0, e], per_piece, carry)

    lax.fori_loop(0, N_EXPERTS, per_expert, 0)


def _piece_rows(meta_ref, col, e, c):
    return pl.ds(pl.multiple_of(meta_ref[0, col + e] + c * RUN, RUN), RUN)


def _dispatch_kernel(meta_ref, zstart_ref, zcnt_ref, nused_ref, h_ref, route_ref, xs_ref,
                     sorted_ref, zero_ref, sem, *, tm, ns, rows, n_blocks):
    pos_t = route_ref[...].T
    q = lax.broadcasted_iota(jnp.int32, (ns, tm), 0).astype(F32)
    perm = jnp.zeros((ns, tm), F32)
    for k in range(TOP_K):
        perm = perm + jnp.where(q == pos_t[k:k + 1, :], 1.0, 0.0)
    sorted_ref[...] = jnp.dot(perm.astype(BF16), h_ref[...], preferred_element_type=F32)

    def piece(e, c):
        return pltpu.make_async_copy(sorted_ref.at[_piece_rows(meta_ref, META_SRC, e, c)],
                                     xs_ref.at[_piece_rows(meta_ref, META_DST, e, c)], sem)

    _run_pieces(meta_ref, lambda e, c: piece(e, c).start())
    _run_pieces(meta_ref, lambda e, c: piece(e, c).wait())

    @pl.when(pl.program_id(0) == pl.num_programs(0) - 1)
    def _():
        zero_ref[...] = jnp.zeros_like(zero_ref)
        n_used = nused_ref[0]

        def pad_piece(e, c):
            dst = pl.ds(pl.multiple_of(zstart_ref[e] + c * RUN, RUN), RUN)
            return pltpu.make_async_copy(zero_ref.at[pl.ds(0, RUN)], xs_ref.at[dst], sem)

        def block_copy(blk):
            return pltpu.make_async_copy(zero_ref, xs_ref.at[pl.ds(blk * rows, rows)], sem)

        def pad_loop(fn):
            def per_expert(e, carry):
                def per_piece(c, carry2):
                    fn(pad_piece(e, c))
                    return carry2
                return lax.fori_loop(0, zcnt_ref[e], per_piece, carry)
            lax.fori_loop(0, N_EXPERTS, per_expert, 0)

        def blk_loop(fn):
            def body(blk, carry):
                fn(block_copy(blk))
                return carry
            lax.fori_loop(n_used, n_blocks, body, 0)

        pad_loop(lambda cp: cp.start())
        blk_loop(lambda cp: cp.start())
        pad_loop(lambda cp: cp.wait())
        blk_loop(lambda cp: cp.wait())


def _dispatch(meta, zstart, zcnt, n_used, h2, route, n_rows, tm, ns):
    t, d = h2.shape
    rows = MOE_ROWS
    smem = pl.BlockSpec(memory_space=pltpu.SMEM)
    return pl.pallas_call(
        functools.partial(_dispatch_kernel, tm=tm, ns=ns, rows=rows, n_blocks=n_rows // rows),
        grid=(t // tm,),
        in_specs=[
            pl.BlockSpec((None, 1, V7X_LANES), lambda i: (i, 0, 0), memory_space=pltpu.SMEM),
            smem, smem, smem,
            pl.BlockSpec((tm, d), lambda i: (i, 0)),
            pl.BlockSpec((tm, V7X_LANES), lambda i: (i, 0)),
        ],
        out_specs=pl.BlockSpec(memory_space=pl.ANY),
        out_shape=jax.ShapeDtypeStruct((n_rows, d), F32),
        scratch_shapes=[pltpu.VMEM((ns, d), F32), pltpu.VMEM((rows, d), F32),
                        pltpu.SemaphoreType.DMA],
        compiler_params=_cparams("arbitrary"),
        name="moe_dispatch",
    )(meta, zstart, zcnt, n_used, h2, route)


def _experts_kernel(bexp_ref, nused_ref, x_ref, wg_ref, bg_ref, wu_ref, bu_ref, wd_ref, bd_ref, y_ref,
                    wgb_ref, wub_ref, wdb_ref):
    b = pl.program_id(0)
    new_expert = jnp.logical_or(b == 0, bexp_ref[b] != bexp_ref[jnp.maximum(b - 1, 0)])

    @pl.when(new_expert)
    def _():
        wgb_ref[...] = wg_ref[...].astype(BF16)
        wub_ref[...] = wu_ref[...].astype(BF16)
        wdb_ref[...] = wd_ref[...].astype(BF16)

    @pl.when(b < nused_ref[0])
    def _():
        xb = x_ref[...].astype(BF16)
        g = jnp.dot(xb, wgb_ref[...], preferred_element_type=F32) + bg_ref[...]
        u = jnp.dot(xb, wub_ref[...], preferred_element_type=F32) + bu_ref[...]
        g = jnp.minimum(g, SWIGLU_LIMIT)
        u = jnp.clip(u, -SWIGLU_LIMIT, SWIGLU_LIMIT)
        act = (u + 1.0) * g * jax.nn.sigmoid(SWIGLU_ALPHA * g)
        y_ref[...] = jnp.dot(act.astype(BF16), wdb_ref[...], preferred_element_type=F32) + bd_ref[...]

    @pl.when(b >= nused_ref[0])
    def _():
        y_ref[...] = jnp.zeros_like(y_ref)


def _experts(block_exp, n_used, xs, wg, bg, wu, bu, wd, bd):
    n_rows, d = xs.shape
    de = wg.shape[2]
    rows = MOE_ROWS
    n_blocks = n_rows // rows
    xmap = lambda b, be, nu: (jnp.minimum(b, nu[0] - 1), 0)
    wmap = lambda b, be, nu: (be[b], 0, 0)
    return pl.pallas_call(
        _experts_kernel,
        grid_spec=pltpu.PrefetchScalarGridSpec(
            num_scalar_prefetch=2,
            grid=(n_blocks,),
            in_specs=[
                pl.BlockSpec((rows, d), xmap),
                pl.BlockSpec((None, d, de), wmap), pl.BlockSpec((None, 1, de), wmap),
                pl.BlockSpec((None, d, de), wmap), pl.BlockSpec((None, 1, de), wmap),
                pl.BlockSpec((None, de, d), wmap), pl.BlockSpec((None, 1, d), wmap),
            ],
            out_specs=pl.BlockSpec((rows, d), lambda b, be, nu: (b, 0)),
            scratch_shapes=[pltpu.VMEM((d, de), BF16), pltpu.VMEM((d, de), BF16),
                            pltpu.VMEM((de, d), BF16)],
        ),
        out_shape=jax.ShapeDtypeStruct((n_rows, d), F32),
        compiler_params=_cparams("arbitrary"),
        name="moe_experts",
    )(block_exp, n_used, xs, wg, bg, wu, bu, wd, bd)


def _combine_kernel(meta_ref, route_ref, x1_ref, yb_ref, o_ref, sorted_ref, sem, *, tm, ns):
    @pl.when(pl.program_id(0) == 0)
    def _():
        sorted_ref[...] = jnp.zeros_like(sorted_ref)

    def piece(e, c):
        return pltpu.make_async_copy(yb_ref.at[_piece_rows(meta_ref, META_DST, e, c)],
                                     sorted_ref.at[_piece_rows(meta_ref, META_SRC, e, c)], sem)

    _run_pieces(meta_ref, lambda e, c: piece(e, c).start())
    _run_pieces(meta_ref, lambda e, c: piece(e, c).wait())

    ys = sorted_ref[...].astype(BF16)
    route = route_ref[...]
    q = lax.broadcasted_iota(jnp.int32, (tm, ns), 1).astype(F32)
    wmat = jnp.zeros((tm, ns), F32)
    for k in range(TOP_K):
        wmat = wmat + jnp.where(q == route[:, k:k + 1], route[:, TOP_K + k:TOP_K + k + 1], 0.0)
    o_ref[...] = x1_ref[...] + jnp.dot(wmat.astype(BF16), ys, preferred_element_type=F32)


def _combine(meta, route, x1, yb, tm, ns):
    t, d = x1.shape
    return pl.pallas_call(
        functools.partial(_combine_kernel, tm=tm, ns=ns),
        grid=(t // tm,),
        in_specs=[
            pl.BlockSpec((None, 1, V7X_LANES), lambda i: (i, 0, 0), memory_space=pltpu.SMEM),
            pl.BlockSpec((tm, V7X_LANES), lambda i: (i, 0)),
            pl.BlockSpec((tm, d), lambda i: (i, 0)),
            pl.BlockSpec(memory_space=pl.ANY),
        ],
        out_specs=pl.BlockSpec((tm, d), lambda i: (i, 0)),
        out_shape=jax.ShapeDtypeStruct((t, d), F32),
        scratch_shapes=[pltpu.VMEM((ns, d), F32), pltpu.SemaphoreType.DMA],
        compiler_params=_cparams("arbitrary"),
        name="moe_combine",
    )(meta, route, x1, yb)


def _head_indicator(width, head_dim):
    lane_head = jnp.arange(width) // head_dim
    return (lane_head[:, None] == jnp.arange(V7X_LANES)[None, :]).astype(BF16)


def _layer(x, mem, norm_mix, w_in, a_q_gain, a_k_gain, a_rel_bias, conv_w, conv_b, dt_bias, a_log,
           d_skip, ssm_norm, norm_mem, w_mem_kv, x_q_gain, x_k_gain, w_br_a, w_br_b, w_br_c, w_out,
           norm_ffn, w_router, b_router, w_gate, b_gate, w_up, b_up, w_down, b_down):
    b, s, d = x.shape
    t = b * s
    a_width = A_HEADS * A_HEAD_DIM
    inner = SSM_HEADS * SSM_HEAD_DIM
    gs = SSM_GROUPS * SSM_STATE
    x_width = X_HEADS * X_HEAD_DIM
    assert d == a_width == x_width and inner == 2 * d and 2 * gs == d

    o_q, o_k, o_v = 0, a_width, 2 * a_width
    o_z = 3 * a_width
    o_xbc = o_z + inner
    o_dt = o_xbc + inner + 2 * gs
    o_qx = o_dt + SSM_HEADS
    o_gate = o_qx + x_width
    cols = lambda o, w: w_in[:, o:o + w]
    w_main = jnp.concatenate([
        cols(o_z, inner), cols(o_xbc, inner), cols(o_q, a_width), cols(o_k, a_width),
        cols(o_v, a_width), cols(o_xbc + inner, 2 * gs), cols(o_qx, x_width), cols(o_gate, 3 * d),
    ], axis=1).astype(BF16)
    roles = ("plain", "plain", "plain", "plain", "qa", "ka", "plain", "plain", "qx", "sig", "sig", "sig")
    w_dt = jnp.pad(cols(o_dt, SSM_HEADS), ((0, 0), (0, V7X_LANES - SSM_HEADS))).astype(BF16)
    ind_a = _head_indicator(a_width, A_HEAD_DIM)
    ind_x = _head_indicator(x_width, X_HEAD_DIM)
    gains = jnp.zeros((V7X_SUBLANES, d), F32)
    gains = gains.at[0].set(jnp.tile(a_q_gain, A_HEADS) * A_HEAD_DIM ** -0.5)
    gains = gains.at[1].set(jnp.tile(a_k_gain, A_HEADS))
    gains = gains.at[2].set(jnp.tile(x_q_gain, X_HEADS) * X_HEAD_DIM ** -0.5)

    x2d = x.reshape(t, d)
    proj, dt_raw = _in_proj(x2d, norm_mix.reshape(1, d), w_main, w_dt, ind_a, ind_a.T, ind_x, ind_x.T,
                            gains, roles, tm=min(1024, t), tn=d)
    proj3 = proj.reshape(b, s, proj.shape[1])
    lane_blk = d // V7X_LANES

    y_a = _attention(proj3, _attn_bias(a_rel_bias), q_col=4 * lane_blk, k_col=5 * lane_blk,
                     v_col=6 * lane_blk)

    pad_h = lambda v: jnp.pad(v.astype(F32), (0, V7X_LANES - SSM_HEADS)).reshape(1, V7X_LANES)
    e_mat = _head_indicator(inner, SSM_HEAD_DIM).T
    y_b = _ssd(proj3, dt_raw.reshape(b, s, V7X_LANES),
               conv_w[:, :inner], conv_b[:inner].reshape(1, inner),
               conv_w[:, inner:], conv_b[inner:].reshape(1, 2 * gs),
               pad_h(dt_bias), pad_h(-jnp.exp(a_log.astype(F32))),
               jnp.repeat(d_skip.astype(F32), SSM_HEAD_DIM).reshape(1, inner),
               ssm_norm.reshape(1, inner), e_mat, z_blk=0, x_blk=1, bc_blk=7)

    k_mem, v_mem = _mem_kv(mem, norm_mem.reshape(1, d), w_mem_kv.astype(BF16),
                           x_k_gain.reshape(1, X_HEAD_DIM))
    y_c = _mem_attn(proj3, k_mem, v_mem, q_blk=8, tq=min(512, s))

    w_r = jnp.pad(w_router, ((0, 0), (0, V7X_LANES - N_EXPERTS)))
    b_r = jnp.pad(b_router, (0, V7X_LANES - N_EXPERTS)).reshape(1, V7X_LANES)
    tm_moe = min(MOE_TILE, t)
    n_tiles = t // tm_moe
    x1, h2, route, tile_cnt = _merge(
        x2d, y_a.reshape(t, a_width), y_b.reshape(t, inner), y_c.reshape(t, x_width), proj,
        w_br_a.astype(BF16), w_br_b.astype(BF16), w_br_c.astype(BF16), w_out.astype(BF16),
        norm_ffn.reshape(1, d), w_r, b_r, gate_blk=9, tm=tm_moe)

    rows = MOE_ROWS
    ns = -(-(tm_moe * TOP_K + N_EXPERTS * (RUN - 1)) // V7X_LANES) * V7X_LANES
    n_blocks = -(-(t * TOP_K + n_tiles * N_EXPERTS * (RUN - 1) + N_EXPERTS * (rows - 1)) // rows)
    n_rows = n_blocks * rows
    cnt8 = tile_cnt[:, 0, :N_EXPERTS].astype(jnp.int32)
    total = jnp.sum(cnt8, axis=0)
    padded = (total + rows - 1) // rows * rows
    pad_ends = jnp.cumsum(padded)
    pad_starts = pad_ends - padded
    n_used = (pad_ends[-1] // rows).reshape(1).astype(jnp.int32)
    blk = jnp.minimum(jnp.arange(n_blocks, dtype=jnp.int32), n_used[0] - 1)
    block_exp = jnp.minimum(jnp.sum(pad_ends[None, :] <= (blk * rows)[:, None], axis=1),
                            N_EXPERTS - 1).astype(jnp.int32)
    src_row = jnp.cumsum(cnt8, axis=1) - cnt8
    dst_row = pad_starts[None, :] + jnp.cumsum(cnt8, axis=0) - cnt8
    meta = jnp.concatenate([cnt8 // RUN, src_row, dst_row, jnp.zeros_like(cnt8)], axis=1)
    meta = meta.reshape(n_tiles, 1, 4 * N_EXPERTS).astype(jnp.int32)
    zstart = (pad_starts + total).astype(jnp.int32)
    zcnt = ((padded - total) // RUN).astype(jnp.int32)

    xs = _dispatch(meta, zstart, zcnt, n_used, h2, route, n_rows, tm_moe, ns)
    yb = _experts(block_exp, n_used, xs,
                  w_gate, b_gate.reshape(N_EXPERTS, 1, -1),
                  w_up, b_up.reshape(N_EXPERTS, 1, -1),
                  w_down, b_down.reshape(N_EXPERTS, 1, -1))
    out = _combine(meta, route, x1, yb, tm_moe, ns)
    return out.reshape(b, s, d)


def kernel(x, mem, norm_mix, w_in, a_q_gain, a_k_gain, a_rel_bias, conv_w, conv_b, dt_bias, a_log, d_skip, ssm_norm, norm_mem, w_mem_kv, x_q_gain, x_k_gain, w_br_a, w_br_b, w_br_c, w_out, norm_ffn, w_router, b_router, w_gate, b_gate, w_up, b_up, w_down, b_down):
    for l in range(norm_mix.shape[0]):
        x = _layer(x, mem, norm_mix[l], w_in[l], a_q_gain[l], a_k_gain[l], a_rel_bias[l], conv_w[l],
                   conv_b[l], dt_bias[l], a_log[l], d_skip[l], ssm_norm[l], norm_mem[l], w_mem_kv[l],
                   x_q_gain[l], x_k_gain[l], w_br_a[l], w_br_b[l], w_br_c[l], w_out[l], norm_ffn[l],
                   w_router[l], b_router[l], w_gate[l], b_gate[l], w_up[l], b_up[l], w_down[l],
                   b_down[l])
    return x
```

```python
import functools

import jax
import jax.numpy as jnp
from jax import lax
from jax.experimental import pallas as pl
from jax.experimental.pallas import tpu as pltpu

F32 = jnp.float32
BF16 = jnp.bfloat16
HIGHEST = lax.Precision.HIGHEST

V7X_LANES = 128
V7X_SUBLANES = 8
V7X_VMEM_LIMIT_BYTES = 56 * 1024 * 1024

EPS = 1e-6
NEG = -1e30

CHUNK = 64
A_HEADS = 16
A_HEAD_DIM = 64
LEFT_CHUNKS = 8
REL_CLIP = 128
SSM_HEADS = 32
SSM_HEAD_DIM = 64
SSM_GROUPS = 4
SSM_STATE = 128
CONV_WIDTH = 4
X_HEADS = 4
X_HEAD_DIM = 256
N_EXPERTS = 32
TOP_K = 4
SWIGLU_LIMIT = 7.0
SWIGLU_ALPHA = 1.702

ATTN_TQ = 256
SSD_L = 256
MOE_ROWS = 512
MOE_TILE = 256
MERGE_TM = 512


def _cparams(*sem):
    return pltpu.CompilerParams(dimension_semantics=sem, vmem_limit_bytes=V7X_VMEM_LIMIT_BYTES)


def _split_bf16(v):
    hi = v.astype(BF16)
    lo = (v - hi.astype(F32)).astype(BF16)
    return hi, lo


def _in_proj_kernel(x_ref, nw_ref, w_ref, wdt_ref, ind_a_ref, ind_at_ref, ind_x_ref, ind_xt_ref,
                    gains_ref, o_ref, dt_ref, h_ref, *, roles):
    j = pl.program_id(1)

    @pl.when(j == 0)
    def _():
        x = x_ref[...]
        ms = jnp.mean(x * x, axis=-1, keepdims=True)
        hb = (x * lax.rsqrt(ms + EPS) * nw_ref[...]).astype(BF16)
        h_ref[...] = hb
        dt_ref[...] = jnp.dot(hb, wdt_ref[...], preferred_element_type=F32)

    def head_norm(acc, ind_ref, indt_ref, head_dim, gain_row):
        s = jnp.dot((acc * acc).astype(BF16), ind_ref[...], preferred_element_type=F32)
        r = lax.rsqrt(s * (1.0 / head_dim) + EPS)
        rexp = jnp.dot(r.astype(BF16), indt_ref[...], preferred_element_type=F32)
        return acc * rexp * gain_row

    def cond_for(role):
        c = None
        for jj, r in enumerate(roles):
            if r == role:
                c = (j == jj) if c is None else jnp.logical_or(c, j == jj)
        return c

    for role in sorted(set(roles)):
        @pl.when(cond_for(role))
        def _(role=role):
            acc = jnp.dot(h_ref[...], w_ref[...], preferred_element_type=F32)
            if role == "qa":
                out = head_norm(acc, ind_a_ref, ind_at_ref, A_HEAD_DIM, gains_ref[0:1, :])
            elif role == "ka":
                out = head_norm(acc, ind_a_ref, ind_at_ref, A_HEAD_DIM, gains_ref[1:2, :])
            elif role == "qx":
                out = head_norm(acc, ind_x_ref, ind_xt_ref, X_HEAD_DIM, gains_ref[2:3, :])
            elif role == "sig":
                out = jax.nn.sigmoid(acc)
            else:
                out = acc
            o_ref[...] = out.astype(o_ref.dtype)


def _in_proj(x2d, norm_w, w_main, w_dt, ind_a, ind_at, ind_x, ind_xt, gains, roles, tm, tn):
    t, d = x2d.shape
    n = w_main.shape[1]
    assert t % tm == 0 and n % tn == 0 and len(roles) == n // tn
    const = lambda i, j: (0, 0)
    return pl.pallas_call(
        functools.partial(_in_proj_kernel, roles=roles),
        grid=(t // tm, n // tn),
        in_specs=[
            pl.BlockSpec((tm, d), lambda i, j: (i, 0)),
            pl.BlockSpec((1, d), const),
            pl.BlockSpec((d, tn), lambda i, j: (0, j)),
            pl.BlockSpec((d, V7X_LANES), const),
            pl.BlockSpec(ind_a.shape, const),
            pl.BlockSpec(ind_at.shape, const),
            pl.BlockSpec(ind_x.shape, const),
            pl.BlockSpec(ind_xt.shape, const),
            pl.BlockSpec(gains.shape, const),
        ],
        out_specs=[
            pl.BlockSpec((tm, tn), lambda i, j: (i, j)),
            pl.BlockSpec((tm, V7X_LANES), lambda i, j: (i, 0)),
        ],
        out_shape=[
            jax.ShapeDtypeStruct((t, n), BF16),
            jax.ShapeDtypeStruct((t, V7X_LANES), F32),
        ],
        scratch_shapes=[pltpu.VMEM((tm, d), BF16)],
        compiler_params=_cparams("parallel", "arbitrary"),
        name="in_proj",
    )(x2d, norm_w, w_main, w_dt, ind_a, ind_at, ind_x, ind_xt, gains)


def _attn_kernel(q_ref, *refs, tq, nprev):
    k_refs = refs[:nprev + 1]
    v_refs = refs[nprev + 1:2 * nprev + 2]
    bias_ref, o_ref = refs[2 * nprev + 2:]
    qb = pl.program_id(1)
    nk = (nprev + 1) * tq
    lane = lax.broadcasted_iota(jnp.int32, (1, V7X_LANES), 1)
    col = lax.broadcasted_iota(jnp.int32, (1, nk), 1)
    before_start = col < (nprev - qb) * tq
    for hp in range(A_HEADS // 2):
        ls = slice(hp * V7X_LANES, (hp + 1) * V7X_LANES)
        q2 = q_ref[:, ls]
        kk = jnp.concatenate([r[:, ls] for r in k_refs], axis=0)
        vv = jnp.concatenate([r[:, ls] for r in v_refs], axis=0)
        outs = []
        for hh in range(2):
            sel = (lane < A_HEAD_DIM) if hh == 0 else (lane >= A_HEAD_DIM)
            qm = jnp.where(sel, q2, jnp.zeros_like(q2))
            s = lax.dot_general(qm, kk, (((1,), (1,)), ((), ())), preferred_element_type=F32)
            s = jnp.where(before_start, NEG, s + bias_ref[2 * hp + hh])
            m = jnp.max(s, axis=-1, keepdims=True)
            p = jnp.exp(s - m)
            l = jnp.sum(p, axis=-1, keepdims=True)
            o = jnp.dot(p.astype(BF16), vv, preferred_element_type=F32)
            outs.append(o / l)
        o_ref[:, ls] = jnp.where(lane < A_HEAD_DIM, outs[0], outs[1]).astype(o_ref.dtype)


def _attention(proj3, bias, q_tile, k_tile, v_tile):
    b, s, _ = proj3.shape
    tq = ATTN_TQ
    width = A_HEADS * A_HEAD_DIM
    left = LEFT_CHUNKS * CHUNK
    assert left % tq == 0 and s % tq == 0
    nprev = left // tq

    def kv_spec(tile, back):
        return pl.BlockSpec((None, tq, width), lambda bi, qi: (bi, jnp.maximum(qi - back, 0), tile))

    in_specs = [pl.BlockSpec((None, tq, width), lambda bi, qi: (bi, qi, q_tile))]
    in_specs += [kv_spec(k_tile, nprev - i) for i in range(nprev + 1)]
    in_specs += [kv_spec(v_tile, nprev - i) for i in range(nprev + 1)]
    in_specs += [pl.BlockSpec(bias.shape, lambda bi, qi: (0, 0, 0), pipeline_mode=pl.Buffered(1))]
    return pl.pallas_call(
        functools.partial(_attn_kernel, tq=tq, nprev=nprev),
        grid=(b, s // tq),
        in_specs=in_specs,
        out_specs=pl.BlockSpec((None, tq, width), lambda bi, qi: (bi, qi, 0)),
        out_shape=jax.ShapeDtypeStruct((b, s, width), BF16),
        compiler_params=_cparams("parallel", "parallel"),
        name="chunk_attn",
    )(proj3, *([proj3] * (2 * nprev + 2)), bias)


def _attn_bias(rel_bias):
    tq = ATTN_TQ
    left = LEFT_CHUNKS * CHUNK
    i = jnp.arange(tq)[:, None]
    j = jnp.arange(left + tq)[None, :]
    m_len = tq + left + tq
    m = jnp.arange(m_len)
    diff = jnp.where(m < left + tq, m, m - m_len)
    v = rel_bias[:, jnp.clip(left - diff, -REL_CLIP, REL_CLIP) + REL_CLIP].astype(F32)
    h = v.shape[0]
    bias = jnp.tile(v, (1, tq))[:, :tq * (m_len - 1)].reshape(h, tq, m_len - 1)[:, :, :left + tq]
    qc, kc = i // CHUNK, j // CHUNK
    in_band = (kc >= qc) & (kc <= qc + LEFT_CHUNKS)
    return jnp.where(in_band[None], bias, NEG)


def _ssd_kernel(z_ref, x_ref, bc_ref, dt_ref, cwx_ref, cbx_ref, cwbc_ref, cbbc_ref, dtb_ref,
                aneg_ref, dskip_ref, gain_ref, e_ref, o_ref,
                xf_ref, bcf_ref, st_ref, xs_ref, y_ref, *, L):
    c = pl.program_id(1)
    inner = SSM_HEADS * SSM_HEAD_DIM
    gw = inner // SSM_GROUPS
    gs = SSM_GROUPS * SSM_STATE
    tail = V7X_SUBLANES

    @pl.when(c == 0)
    def _():
        xf_ref[0:tail, :] = jnp.zeros((tail, inner), F32)
        bcf_ref[0:tail, :] = jnp.zeros((tail, 2 * gs), F32)
        st_ref[...] = jnp.zeros_like(st_ref)

    xf_ref[tail:, :] = x_ref[...].astype(F32)
    bcf_ref[tail:, :] = bc_ref[...].astype(F32)

    def conv_silu(src_ref, w_ref, b_ref, c0, c1):
        acc = b_ref[:, c0:c1] + w_ref[CONV_WIDTH - 1:CONV_WIDTH, c0:c1] * src_ref[tail:tail + L, c0:c1]
        for k in range(1, CONV_WIDTH):
            acc = acc + (w_ref[CONV_WIDTH - 1 - k:CONV_WIDTH - k, c0:c1]
                         * src_ref[tail - k:tail - k + L, c0:c1])
        return acc * jax.nn.sigmoid(acc)

    for g in range(SSM_GROUPS):
        xs_ref[:, g * gw:(g + 1) * gw] = conv_silu(xf_ref, cwx_ref, cbx_ref, g * gw, (g + 1) * gw)
    bmat = conv_silu(bcf_ref, cwbc_ref, cbbc_ref, 0, gs)
    cmat = conv_silu(bcf_ref, cwbc_ref, cbbc_ref, gs, 2 * gs)
    xf_ref[0:tail, :] = xf_ref[L:L + tail, :]
    bcf_ref[0:tail, :] = bcf_ref[L:L + tail, :]

    pre = dt_ref[...] + dtb_ref[...]
    dt = jnp.maximum(pre, 0.0) + jnp.log1p(jnp.exp(-jnp.abs(pre)))
    a = dt * aneg_ref[...]
    row = lax.broadcasted_iota(jnp.int32, (L, L), 0)
    colm = lax.broadcasted_iota(jnp.int32, (L, L), 1)
    lower = colm <= row
    tri = jnp.where(lower, 1.0, 0.0).astype(F32)
    cs = jnp.dot(tri, a, precision=HIGHEST, preferred_element_type=F32)
    cs_t = cs.T
    dt_t = dt.T
    cs_last = cs[L - 1:L, :]
    w_state = dt * jnp.exp(cs_last - cs)
    e_cs = jnp.exp(cs)
    chunk_decay = jnp.broadcast_to(jnp.exp(cs_last), (tail, V7X_LANES))
    stacked = jnp.concatenate([w_state, e_cs, chunk_decay], axis=0)
    s_hi, s_lo = _split_bf16(stacked)
    expanded = (jnp.dot(s_hi, e_ref[...], preferred_element_type=F32)
                + jnp.dot(s_lo, e_ref[...], preferred_element_type=F32))
    w_state_e = expanded[0:L]
    e_cs_e = expanded[L:2 * L]
    decay_e = expanded[2 * L:2 * L + 1]

    lane = lax.broadcasted_iota(jnp.int32, (1, V7X_LANES), 1)
    pairs_per_group = gw // V7X_LANES
    for g in range(SSM_GROUPS):
        bg = bmat[:, g * SSM_STATE:(g + 1) * SSM_STATE]
        cg = cmat[:, g * SSM_STATE:(g + 1) * SSM_STATE].astype(BF16)
        cb = lax.dot_general(cg, bg.astype(BF16), (((1,), (1,)), ((), ())),
                             preferred_element_type=F32)
        state_b = st_ref[g].astype(BF16)
        y_off = jnp.dot(cg, state_b, preferred_element_type=F32) * e_cs_e[:, g * gw:(g + 1) * gw]
        for pr in range(pairs_per_group):
            c0 = g * gw + pr * V7X_LANES
            xp = xs_ref[:, c0:c0 + V7X_LANES]
            xpb = xp.astype(BF16)
            acc = y_off[:, pr * V7X_LANES:(pr + 1) * V7X_LANES] + dskip_ref[:, c0:c0 + V7X_LANES] * xp
            for hh in range(2):
                h = c0 // SSM_HEAD_DIM + hh
                d = cs[:, h:h + 1] - cs_t[h:h + 1, :]
                m = cb * jnp.exp(jnp.where(lower, d, NEG)) * dt_t[h:h + 1, :]
                sel = (lane < SSM_HEAD_DIM) if hh == 0 else (lane >= SSM_HEAD_DIM)
                xm = jnp.where(sel, xpb, jnp.zeros_like(xpb))
                acc = acc + jnp.dot(m.astype(BF16), xm, preferred_element_type=F32)
            y_ref[:, c0:c0 + V7X_LANES] = acc
        xw = (xs_ref[:, g * gw:(g + 1) * gw] * w_state_e[:, g * gw:(g + 1) * gw]).astype(BF16)
        new = jnp.dot(bg.T.astype(BF16), xw, preferred_element_type=F32)
        st_ref[g] = st_ref[g] * decay_e[:, g * gw:(g + 1) * gw] + new

    for g in range(SSM_GROUPS):
        sl = slice(g * gw, (g + 1) * gw)
        zz = z_ref[:, sl].astype(F32)
        yz = y_ref[:, sl] * (zz * jax.nn.sigmoid(zz))
        ms = jnp.mean(yz * yz, axis=-1, keepdims=True)
        o_ref[:, sl] = (yz * lax.rsqrt(ms + EPS) * gain_ref[:, sl]).astype(o_ref.dtype)


def _ssd(proj3, dt3, cwx, cbx, cwbc, cbbc, dtb, aneg, dskip_e, gain, e_mat, z_blk, x_blk, bc_blk):
    b, s, _ = proj3.shape
    L = SSD_L
    assert s % L == 0
    inner = SSM_HEADS * SSM_HEAD_DIM
    gs2 = 2 * SSM_GROUPS * SSM_STATE
    const = lambda bi, ci: (0, 0)
    full = lambda a: pl.BlockSpec(a.shape, const)
    return pl.pallas_call(
        functools.partial(_ssd_kernel, L=L),
        grid=(b, s // L),
        in_specs=[
            pl.BlockSpec((None, L, inner), lambda bi, ci: (bi, ci, z_blk)),
            pl.BlockSpec((None, L, inner), lambda bi, ci: (bi, ci, x_blk)),
            pl.BlockSpec((None, L, gs2), lambda bi, ci: (bi, ci, bc_blk)),
            pl.BlockSpec((None, L, V7X_LANES), lambda bi, ci: (bi, ci, 0)),
            full(cwx), full(cbx), full(cwbc), full(cbbc), full(dtb), full(aneg), full(dskip_e),
            full(gain), full(e_mat),
        ],
        out_specs=pl.BlockSpec((None, L, inner), lambda bi, ci: (bi, ci, 0)),
        out_shape=jax.ShapeDtypeStruct((b, s, inner), BF16),
        scratch_shapes=[
            pltpu.VMEM((L + V7X_SUBLANES, inner), F32),
            pltpu.VMEM((L + V7X_SUBLANES, gs2), F32),
            pltpu.VMEM((SSM_GROUPS, SSM_STATE, inner // SSM_GROUPS), F32),
            pltpu.VMEM((L, inner), F32),
            pltpu.VMEM((L, inner), F32),
        ],
        compiler_params=_cparams("parallel", "arbitrary"),
        name="ssd",
    )(proj3, proj3, proj3, dt3, cwx, cbx, cwbc, cbbc, dtb, aneg, dskip_e, gain, e_mat)


def _mem_kv_kernel(mem_ref, g_ref, w_ref, kg_ref, k_ref, v_ref):
    m = mem_ref[...]
    ms = jnp.mean(m * m, axis=-1, keepdims=True)
    mn = (m * lax.rsqrt(ms + EPS) * g_ref[...]).astype(BF16)
    kv = jnp.dot(mn, w_ref[...], preferred_element_type=F32)
    width = X_HEADS * X_HEAD_DIM
    for h in range(X_HEADS):
        sl = slice(h * X_HEAD_DIM, (h + 1) * X_HEAD_DIM)
        kh = kv[:, sl]
        r = lax.rsqrt(jnp.mean(kh * kh, axis=-1, keepdims=True) + EPS)
        k_ref[:, sl] = (kh * r * kg_ref[...]).astype(k_ref.dtype)
    v_ref[...] = kv[:, width:].astype(v_ref.dtype)


def _mem_kv(mem, norm_mem, w_kv, k_gain):
    b, m, d = mem.shape
    width = X_HEADS * X_HEAD_DIM
    const = lambda bi: (0, 0)
    return pl.pallas_call(
        _mem_kv_kernel,
        grid=(b,),
        in_specs=[
            pl.BlockSpec((None, m, d), lambda bi: (bi, 0, 0)),
            pl.BlockSpec((1, d), const),
            pl.BlockSpec((d, 2 * width), const),
            pl.BlockSpec((1, X_HEAD_DIM), const),
        ],
        out_specs=[pl.BlockSpec((None, m, width), lambda bi: (bi, 0, 0))] * 2,
        out_shape=[jax.ShapeDtypeStruct((b, m, width), BF16)] * 2,
        compiler_params=_cparams("parallel"),
        name="mem_kv",
    )(mem, norm_mem, w_kv, k_gain)


def _mem_attn_kernel(q_ref, k_ref, v_ref, o_ref):
    for h in range(X_HEADS):
        sl = slice(h * X_HEAD_DIM, (h + 1) * X_HEAD_DIM)
        s = lax.dot_general(q_ref[:, sl], k_ref[:, sl], (((1,), (1,)), ((), ())),
                            preferred_element_type=F32)
        m = jnp.max(s, axis=-1, keepdims=True)
        p = jnp.exp(s - m)
        l = jnp.sum(p, axis=-1, keepdims=True)
        o = jnp.dot(p.astype(BF16), v_ref[:, sl], preferred_element_type=F32)
        o_ref[:, sl] = (o / l).astype(o_ref.dtype)


def _mem_attn(proj3, k, v, q_blk, tq):
    b, s, _ = proj3.shape
    m = k.shape[1]
    width = X_HEADS * X_HEAD_DIM
    assert s % tq == 0
    return pl.pallas_call(
        _mem_attn_kernel,
        grid=(b, s // tq),
        in_specs=[
            pl.BlockSpec((None, tq, width), lambda bi, qi: (bi, qi, q_blk)),
            pl.BlockSpec((None, m, width), lambda bi, qi: (bi, 0, 0)),
            pl.BlockSpec((None, m, width), lambda bi, qi: (bi, 0, 0)),
        ],
        out_specs=pl.BlockSpec((None, tq, width), lambda bi, qi: (bi, qi, 0)),
        out_shape=jax.ShapeDtypeStruct((b, s, width), BF16),
        compiler_params=_cparams("parallel", "parallel"),
        name="mem_attn",
    )(proj3, k, v)


def _merge_kernel(x_ref, ya_ref, yb_ref, yc_ref, g0_ref, g1_ref, g2_ref, wa_ref, wb_ref, wc_ref,
                  wo_ref, nf_ref, wrh_ref, wrl_ref, br_ref,
                  x1_ref, h2_ref, route_ref, cnt_ref, *, tm, rt):
    merged = (g0_ref[...].astype(F32) * jnp.dot(ya_ref[...], wa_ref[...], preferred_element_type=F32)
              + g1_ref[...].astype(F32) * jnp.dot(yb_ref[...], wb_ref[...], preferred_element_type=F32)
              + g2_ref[...].astype(F32) * jnp.dot(yc_ref[...], wc_ref[...], preferred_element_type=F32))
    x1 = x_ref[...] + jnp.dot(merged.astype(BF16), wo_ref[...], preferred_element_type=F32)
    x1_ref[...] = x1
    ms = jnp.mean(x1 * x1, axis=-1, keepdims=True)
    h2 = x1 * lax.rsqrt(ms + EPS) * nf_ref[...]
    h2_ref[...] = h2.astype(h2_ref.dtype)

    h_hi, h_lo = _split_bf16(h2)
    logits_all = (jnp.dot(h_hi, wrh_ref[...], preferred_element_type=F32)
                  + jnp.dot(h_lo, wrh_ref[...], preferred_element_type=F32)
                  + jnp.dot(h_hi, wrl_ref[...], preferred_element_type=F32)) + br_ref[...]
    for sub in range(tm // rt):
        _route_tile(logits_all[sub * rt:(sub + 1) * rt], route_ref.at[pl.ds(sub * rt, rt)],
                    cnt_ref.at[sub], rt)


def _route_tile(logits, route_ref, cnt_ref, tm):
    lane = lax.broadcasted_iota(jnp.int32, (tm, V7X_LANES), 1)
    lane_f = lane.astype(F32)
    work = jnp.where(lane < N_EXPERTS, logits, NEG)
    sel_val, sel_oh = [], []
    for _ in range(TOP_K):
        mval = jnp.max(work, axis=-1, keepdims=True)
        ik = jnp.min(jnp.where(work == mval, lane_f, float(V7X_LANES)), axis=-1, keepdims=True)
        oh = lane_f == ik
        work = jnp.where(oh, NEG, work)
        sel_val.append(mval)
        sel_oh.append(oh)
    ex = [jnp.exp(v - sel_val[0]) for v in sel_val]
    denom = ex[0] + ex[1] + ex[2] + ex[3]

    oh_all = jnp.zeros((tm, V7X_LANES), F32)
    for oh in sel_oh:
        oh_all = oh_all + jnp.where(oh, 1.0, 0.0)
    row = lax.broadcasted_iota(jnp.int32, (tm, tm), 0)
    colm = lax.broadcasted_iota(jnp.int32, (tm, tm), 1)
    strict = jnp.where(colm < row, 1.0, 0.0).astype(BF16)
    before = jnp.dot(strict, oh_all.astype(BF16), preferred_element_type=F32)
    cnt = jnp.sum(oh_all, axis=0, keepdims=True)
    cnt8 = jnp.floor((cnt + (V7X_SUBLANES - 1.0)) * (1.0 / V7X_SUBLANES)) * V7X_SUBLANES
    cnt8 = jnp.broadcast_to(cnt8, (V7X_SUBLANES, V7X_LANES))
    er = lax.broadcasted_iota(jnp.int32, (V7X_LANES, V7X_LANES), 0)
    ec = lax.broadcasted_iota(jnp.int32, (V7X_LANES, V7X_LANES), 1)
    earlier = jnp.where(er < ec, 1.0, 0.0).astype(BF16)
    run_start = jnp.dot(cnt8.astype(BF16), earlier, preferred_element_type=F32)[0:1, :]
    slot = before + run_start
    route = jnp.zeros((tm, V7X_LANES), F32)
    for k in range(TOP_K):
        pos = jnp.sum(jnp.where(sel_oh[k], slot, 0.0), axis=-1, keepdims=True)
        route = jnp.where(lane == k, pos, route)
        route = jnp.where(lane == TOP_K + k, ex[k] / denom, route)
    route_ref[...] = route
    cnt_ref[...] = cnt8


def _merge(x2d, ya, yb, yc, proj, wa, wb, wc, wo, nf, wr, br, gate_blk, tm, rt):
    t, d = x2d.shape
    assert t % tm == 0 and tm % rt == 0
    wr_hi, wr_lo = _split_bf16(wr)
    const = lambda i: (0, 0)
    full = lambda a: pl.BlockSpec(a.shape, const)
    rows = lambda w: pl.BlockSpec((tm, w), lambda i: (i, 0))
    return pl.pallas_call(
        functools.partial(_merge_kernel, tm=tm, rt=rt),
        grid=(t // tm,),
        in_specs=[
            rows(d), rows(ya.shape[1]), rows(yb.shape[1]), rows(yc.shape[1]),
            pl.BlockSpec((tm, d), lambda i: (i, gate_blk)),
            pl.BlockSpec((tm, d), lambda i: (i, gate_blk + 1)),
            pl.BlockSpec((tm, d), lambda i: (i, gate_blk + 2)),
            full(wa), full(wb), full(wc), full(wo), full(nf), full(wr_hi), full(wr_lo), full(br),
        ],
        out_specs=[rows(d), rows(d), rows(V7X_LANES),
                   pl.BlockSpec((tm // rt, V7X_SUBLANES, V7X_LANES), lambda i: (i, 0, 0))],
        out_shape=[
            jax.ShapeDtypeStruct((t, d), F32),
            jax.ShapeDtypeStruct((t, d), BF16),
            jax.ShapeDtypeStruct((t, V7X_LANES), F32),
            jax.ShapeDtypeStruct((t // rt, V7X_SUBLANES, V7X_LANES), F32),
        ],
        compiler_params=_cparams("parallel"),
        name="merge_route",
    )(x2d, ya, yb, yc, proj, proj, proj, wa, wb, wc, wo, nf, wr_hi, wr_lo, br)


RUN = V7X_SUBLANES
META_SRC = N_EXPERTS
META_DST = 2 * N_EXPERTS


def _run_pieces(meta_ref, body):
    def per_expert(e, carry):
        def per_piece(c, carry2):
            body(e, c)
            return carry2
        return lax.fori_loop(0, meta_ref[0, e], per_piece, carry)

    lax.fori_loop(0, N_EXPERTS, per_expert, 0)


def _piece_rows(meta_ref, col, e, c):
    return pl.ds(pl.multiple_of(meta_ref[0, col + e] + c * RUN, RUN), RUN)


def _dispatch_kernel(meta_ref, prev_meta_ref, zstart_ref, zcnt_ref, nused_ref, h_ref, route_ref,
                     xs_ref, sorted_ref, zero_ref, sems, *, tm, ns, rows, n_blocks):
    i = pl.program_id(0)
    slot = lax.rem(i, 2)
    pos_t = route_ref[...].T
    q = lax.broadcasted_iota(jnp.int32, (ns, tm), 0).astype(F32)
    perm = jnp.zeros((ns, tm), F32)
    for k in range(TOP_K):
        perm = perm + jnp.where(q == pos_t[k:k + 1, :], 1.0, 0.0)
    sorted_ref[slot] = jnp.dot(perm.astype(BF16), h_ref[...], preferred_element_type=F32)

    def piece(m_ref, sl, e, c):
        return pltpu.make_async_copy(sorted_ref.at[sl, _piece_rows(m_ref, META_SRC, e, c)],
                                     xs_ref.at[_piece_rows(m_ref, META_DST, e, c)], sems.at[sl])

    _run_pieces(meta_ref, lambda e, c: piece(meta_ref, slot, e, c).start())

    @pl.when(i > 0)
    def _():
        _run_pieces(prev_meta_ref, lambda e, c: piece(prev_meta_ref, 1 - slot, e, c).wait())

    @pl.when(i == pl.num_programs(0) - 1)
    def _():
        _run_pieces(meta_ref, lambda e, c: piece(meta_ref, slot, e, c).wait())
        zero_ref[...] = jnp.zeros_like(zero_ref)
        n_used = nused_ref[0]
        sem = sems.at[2]

        def pad_piece(e, c):
            dst = pl.ds(pl.multiple_of(zstart_ref[e] + c * RUN, RUN), RUN)
            return pltpu.make_async_copy(zero_ref.at[pl.ds(0, RUN)], xs_ref.at[dst], sem)

        def block_copy(blk):
            return pltpu.make_async_copy(zero_ref, xs_ref.at[pl.ds(blk * rows, rows)], sem)

        def pad_loop(fn):
            def per_expert(e, carry):
                def per_piece(c, carry2):
                    fn(pad_piece(e, c))
                    return carry2
                return lax.fori_loop(0, zcnt_ref[e], per_piece, carry)
            lax.fori_loop(0, N_EXPERTS, per_expert, 0)

        def blk_loop(fn):
            def body(blk, carry):
                fn(block_copy(blk))
                return carry
            lax.fori_loop(n_used, n_blocks, body, 0)

        pad_loop(lambda cp: cp.start())
        blk_loop(lambda cp: cp.start())
        pad_loop(lambda cp: cp.wait())
        blk_loop(lambda cp: cp.wait())


def _dispatch(meta, zstart, zcnt, n_used, h2, route, n_rows, tm, ns):
    t, d = h2.shape
    rows = MOE_ROWS
    smem = pl.BlockSpec(memory_space=pltpu.SMEM)
    return pl.pallas_call(
        functools.partial(_dispatch_kernel, tm=tm, ns=ns, rows=rows, n_blocks=n_rows // rows),
        grid=(t // tm,),
        in_specs=[
            pl.BlockSpec((None, 1, V7X_LANES), lambda i: (i, 0, 0), memory_space=pltpu.SMEM),
            pl.BlockSpec((None, 1, V7X_LANES), lambda i: (jnp.maximum(i - 1, 0), 0, 0),
                         memory_space=pltpu.SMEM),
            smem, smem, smem,
            pl.BlockSpec((tm, d), lambda i: (i, 0)),
            pl.BlockSpec((tm, V7X_LANES), lambda i: (i, 0)),
        ],
        out_specs=pl.BlockSpec(memory_space=pl.ANY),
        out_shape=jax.ShapeDtypeStruct((n_rows, d), F32),
        scratch_shapes=[pltpu.VMEM((2, ns, d), F32), pltpu.VMEM((rows, d), F32),
                        pltpu.SemaphoreType.DMA((3,))],
        compiler_params=_cparams("arbitrary"),
        name="moe_dispatch",
    )(meta, meta, zstart, zcnt, n_used, h2, route)


def _experts_kernel(bexp_ref, nused_ref, x_ref, wg_ref, bg_ref, wu_ref, bu_ref, wd_ref, bd_ref, y_ref,
                    wgb_ref, wub_ref, wdb_ref):
    b = pl.program_id(0)
    new_expert = jnp.logical_or(b == 0, bexp_ref[b] != bexp_ref[jnp.maximum(b - 1, 0)])

    @pl.when(new_expert)
    def _():
        wgb_ref[...] = wg_ref[...].astype(BF16)
        wub_ref[...] = wu_ref[...].astype(BF16)
        wdb_ref[...] = wd_ref[...].astype(BF16)

    @pl.when(b < nused_ref[0])
    def _():
        xb = x_ref[...].astype(BF16)
        g = jnp.dot(xb, wgb_ref[...], preferred_element_type=F32) + bg_ref[...]
        u = jnp.dot(xb, wub_ref[...], preferred_element_type=F32) + bu_ref[...]
        g = jnp.minimum(g, SWIGLU_LIMIT)
        u = jnp.clip(u, -SWIGLU_LIMIT, SWIGLU_LIMIT)
        act = (u + 1.0) * g * jax.nn.sigmoid(SWIGLU_ALPHA * g)
        y_ref[...] = jnp.dot(act.astype(BF16), wdb_ref[...], preferred_element_type=F32) + bd_ref[...]

    @pl.when(b >= nused_ref[0])
    def _():
        y_ref[...] = jnp.zeros_like(y_ref)


def _experts(block_exp, n_used, xs, wg, bg, wu, bu, wd, bd):
    n_rows, d = xs.shape
    de = wg.shape[2]
    rows = MOE_ROWS
    n_blocks = n_rows // rows
    xmap = lambda b, be, nu: (jnp.minimum(b, nu[0] - 1), 0)
    wmap = lambda b, be, nu: (be[b], 0, 0)
    return pl.pallas_call(
        _experts_kernel,
        grid_spec=pltpu.PrefetchScalarGridSpec(
            num_scalar_prefetch=2,
            grid=(n_blocks,),
            in_specs=[
                pl.BlockSpec((rows, d), xmap),
                pl.BlockSpec((None, d, de), wmap), pl.BlockSpec((None, 1, de), wmap),
                pl.BlockSpec((None, d, de), wmap), pl.BlockSpec((None, 1, de), wmap),
                pl.BlockSpec((None, de, d), wmap), pl.BlockSpec((None, 1, d), wmap),
            ],
            out_specs=pl.BlockSpec((rows, d), lambda b, be, nu: (b, 0)),
            scratch_shapes=[pltpu.VMEM((d, de), BF16), pltpu.VMEM((d, de), BF16),
                            pltpu.VMEM((de, d), BF16)],
        ),
        out_shape=jax.ShapeDtypeStruct((n_rows, d), F32),
        compiler_params=_cparams("arbitrary"),
        name="moe_experts",
    )(block_exp, n_used, xs, wg, bg, wu, bu, wd, bd)


def _combine_kernel(meta_ref, next_meta_ref, route_ref, x1_ref, yb_ref, o_ref, sorted_ref, sems, *,
                    tm, ns):
    i = pl.program_id(0)
    slot = lax.rem(i, 2)

    def piece(m_ref, sl, e, c):
        return pltpu.make_async_copy(yb_ref.at[_piece_rows(m_ref, META_DST, e, c)],
                                     sorted_ref.at[sl, _piece_rows(m_ref, META_SRC, e, c)],
                                     sems.at[sl])

    @pl.when(i == 0)
    def _():
        sorted_ref[...] = jnp.zeros_like(sorted_ref)
        _run_pieces(meta_ref, lambda e, c: piece(meta_ref, slot, e, c).start())

    @pl.when(i + 1 < pl.num_programs(0))
    def _():
        _run_pieces(next_meta_ref, lambda e, c: piece(next_meta_ref, 1 - slot, e, c).start())

    _run_pieces(meta_ref, lambda e, c: piece(meta_ref, slot, e, c).wait())

    ys = sorted_ref[slot].astype(BF16)
    route = route_ref[...]
    q = lax.broadcasted_iota(jnp.int32, (tm, ns), 1).astype(F32)
    wmat = jnp.zeros((tm, ns), F32)
    for k in range(TOP_K):
        wmat = wmat + jnp.where(q == route[:, k:k + 1], route[:, TOP_K + k:TOP_K + k + 1], 0.0)
    o_ref[...] = x1_ref[...] + jnp.dot(wmat.astype(BF16), ys, preferred_element_type=F32)


def _combine(meta, route, x1, yb, tm, ns):
    t, d = x1.shape
    n_tiles = t // tm
    return pl.pallas_call(
        functools.partial(_combine_kernel, tm=tm, ns=ns),
        grid=(t // tm,),
        in_specs=[
            pl.BlockSpec((None, 1, V7X_LANES), lambda i: (i, 0, 0), memory_space=pltpu.SMEM),
            pl.BlockSpec((None, 1, V7X_LANES), lambda i: (jnp.minimum(i + 1, n_tiles - 1), 0, 0),
                         memory_space=pltpu.SMEM),
            pl.BlockSpec((tm, V7X_LANES), lambda i: (i, 0)),
            pl.BlockSpec((tm, d), lambda i: (i, 0)),
            pl.BlockSpec(memory_space=pl.ANY),
        ],
        out_specs=pl.BlockSpec((tm, d), lambda i: (i, 0)),
        out_shape=jax.ShapeDtypeStruct((t, d), F32),
        scratch_shapes=[pltpu.VMEM((2, ns, d), F32), pltpu.SemaphoreType.DMA((2,))],
        compiler_params=_cparams("arbitrary"),
        name="moe_combine",
    )(meta, meta, route, x1, yb)


def _head_indicator(width, head_dim):
    lane_head = jnp.arange(width) // head_dim
    return (lane_head[:, None] == jnp.arange(V7X_LANES)[None, :]).astype(BF16)


def _layer(x, mem, norm_mix, w_in, a_q_gain, a_k_gain, a_rel_bias, conv_w, conv_b, dt_bias, a_log,
           d_skip, ssm_norm, norm_mem, w_mem_kv, x_q_gain, x_k_gain, w_br_a, w_br_b, w_br_c, w_out,
           norm_ffn, w_router, b_router, w_gate, b_gate, w_up, b_up, w_down, b_down):
    b, s, d = x.shape
    t = b * s
    a_width = A_HEADS * A_HEAD_DIM
    inner = SSM_HEADS * SSM_HEAD_DIM
    gs = SSM_GROUPS * SSM_STATE
    x_width = X_HEADS * X_HEAD_DIM
    assert d == a_width == x_width and inner == 2 * d and 2 * gs == d

    o_q, o_k, o_v = 0, a_width, 2 * a_width
    o_z = 3 * a_width
    o_xbc = o_z + inner
    o_dt = o_xbc + inner + 2 * gs
    o_qx = o_dt + SSM_HEADS
    o_gate = o_qx + x_width
    cols = lambda o, w: w_in[:, o:o + w]
    w_main = jnp.concatenate([
        cols(o_z, inner), cols(o_xbc, inner), cols(o_q, a_width), cols(o_k, a_width),
        cols(o_v, a_width), cols(o_xbc + inner, 2 * gs), cols(o_qx, x_width), cols(o_gate, 3 * d),
    ], axis=1).astype(BF16)
    roles = ("plain", "plain", "plain", "plain", "qa", "ka", "plain", "plain", "qx", "sig", "sig", "sig")
    w_dt = jnp.pad(cols(o_dt, SSM_HEADS), ((0, 0), (0, V7X_LANES - SSM_HEADS))).astype(BF16)
    ind_a = _head_indicator(a_width, A_HEAD_DIM)
    ind_x = _head_indicator(x_width, X_HEAD_DIM)
    gains = jnp.zeros((V7X_SUBLANES, d), F32)
    gains = gains.at[0].set(jnp.tile(a_q_gain, A_HEADS) * A_HEAD_DIM ** -0.5)
    gains = gains.at[1].set(jnp.tile(a_k_gain, A_HEADS))
    gains = gains.at[2].set(jnp.tile(x_q_gain, X_HEADS) * X_HEAD_DIM ** -0.5)

    x2d = x.reshape(t, d)
    proj, dt_raw = _in_proj(x2d, norm_mix.reshape(1, d), w_main, w_dt, ind_a, ind_a.T, ind_x, ind_x.T,
                            gains, roles, tm=min(1024, t), tn=d)
    proj3 = proj.reshape(b, s, proj.shape[1])

    y_a = _attention(proj3, _attn_bias(a_rel_bias), q_tile=4, k_tile=5, v_tile=6)

    pad_h = lambda v: jnp.pad(v.astype(F32), (0, V7X_LANES - SSM_HEADS)).reshape(1, V7X_LANES)
    e_mat = _head_indicator(inner, SSM_HEAD_DIM).T
    y_b = _ssd(proj3, dt_raw.reshape(b, s, V7X_LANES),
               conv_w[:, :inner], conv_b[:inner].reshape(1, inner),
               conv_w[:, inner:], conv_b[inner:].reshape(1, 2 * gs),
               pad_h(dt_bias), pad_h(-jnp.exp(a_log.astype(F32))),
               jnp.repeat(d_skip.astype(F32), SSM_HEAD_DIM).reshape(1, inner),
               ssm_norm.reshape(1, inner), e_mat, z_blk=0, x_blk=1, bc_blk=7)

    k_mem, v_mem = _mem_kv(mem, norm_mem.reshape(1, d), w_mem_kv.astype(BF16),
                           x_k_gain.reshape(1, X_HEAD_DIM))
    y_c = _mem_attn(proj3, k_mem, v_mem, q_blk=8, tq=min(512, s))

    w_r = jnp.pad(w_router, ((0, 0), (0, V7X_LANES - N_EXPERTS)))
    b_r = jnp.pad(b_router, (0, V7X_LANES - N_EXPERTS)).reshape(1, V7X_LANES)
    tm_moe = min(MOE_TILE, t)
    n_tiles = t // tm_moe
    x1, h2, route, tile_cnt = _merge(
        x2d, y_a.reshape(t, a_width), y_b.reshape(t, inner), y_c.reshape(t, x_width), proj,
        w_br_a.astype(BF16), w_br_b.astype(BF16), w_br_c.astype(BF16), w_out.astype(BF16),
        norm_ffn.reshape(1, d), w_r, b_r, gate_blk=9, tm=min(MERGE_TM, t), rt=tm_moe)

    rows = MOE_ROWS
    ns = -(-(tm_moe * TOP_K + N_EXPERTS * (RUN - 1)) // V7X_LANES) * V7X_LANES
    n_blocks = -(-(t * TOP_K + n_tiles * N_EXPERTS * (RUN - 1) + N_EXPERTS * (rows - 1)) // rows)
    n_rows = n_blocks * rows
    cnt8 = tile_cnt[:, 0, :N_EXPERTS].astype(jnp.int32)
    total = jnp.sum(cnt8, axis=0)
    padded = (total + rows - 1) // rows * rows
    pad_ends = jnp.cumsum(padded)
    pad_starts = pad_ends - padded
    n_used = (pad_ends[-1] // rows).reshape(1).astype(jnp.int32)
    blk = jnp.minimum(jnp.arange(n_blocks, dtype=jnp.int32), n_used[0] - 1)
    block_exp = jnp.minimum(jnp.sum(pad_ends[None, :] <= (blk * rows)[:, None], axis=1),
                            N_EXPERTS - 1).astype(jnp.int32)
    src_row = jnp.cumsum(cnt8, axis=1) - cnt8
    dst_row = pad_starts[None, :] + jnp.cumsum(cnt8, axis=0) - cnt8
    meta = jnp.concatenate([cnt8 // RUN, src_row, dst_row, jnp.zeros_like(cnt8)], axis=1)
    meta = meta.reshape(n_tiles, 1, 4 * N_EXPERTS).astype(jnp.int32)
    zstart = (pad_starts + total).astype(jnp.int32)
    zcnt = ((padded - total) // RUN).astype(jnp.int32)

    xs = _dispatch(meta, zstart, zcnt, n_used, h2, route, n_rows, tm_moe, ns)
    yb = _experts(block_exp, n_used, xs,
                  w_gate, b_gate.reshape(N_EXPERTS, 1, -1),
                  w_up, b_up.reshape(N_EXPERTS, 1, -1),
                  w_down, b_down.reshape(N_EXPERTS, 1, -1))
    out = _combine(meta, route, x1, yb, tm_moe, ns)
    return out.reshape(b, s, d)


def kernel(x, mem, norm_mix, w_in, a_q_gain, a_k_gain, a_rel_bias, conv_w, conv_b, dt_bias, a_log, d_skip, ssm_norm, norm_mem, w_mem_kv, x_q_gain, x_k_gain, w_br_a, w_br_b, w_br_c, w_out, norm_ffn, w_router, b_router, w_gate, b_gate, w_up, b_up, w_down, b_down):
    for l in range(norm_mix.shape[0]):
        x = _layer(x, mem, norm_mix[l], w_in[l], a_q_gain[l], a_k_gain[l], a_rel_bias[l], conv_w[l],
                   conv_b[l], dt_bias[l], a_log[l], d_skip[l], ssm_norm[l], norm_mem[l], w_mem_kv[l],
                   x_q_gain[l], x_k_gain[l], w_br_a[l], w_br_b[l], w_br_c[l], w_out[l], norm_ffn[l],
                   w_router[l], b_router[l], w_gate[l], b_gate[l], w_up[l], b_up[l], w_down[l],
                   b_down[l])
    return x
```

```python
import functools

import jax
import jax.numpy as jnp
from jax import lax
from jax.experimental import pallas as pl
from jax.experimental.pallas import tpu as pltpu

F32 = jnp.float32
BF16 = jnp.bfloat16
HIGHEST = lax.Precision.HIGHEST

V7X_LANES = 128
V7X_SUBLANES = 8
V7X_VMEM_LIMIT_BYTES = 56 * 1024 * 1024

EPS = 1e-6
LOG2_E = 1.4426950408889634
NEG = -1e30

CHUNK = 64
A_HEADS = 16
A_HEAD_DIM = 64
LEFT_CHUNKS = 8
REL_CLIP = 128
SSM_HEADS = 32
SSM_HEAD_DIM = 64
SSM_GROUPS = 4
SSM_STATE = 128
CONV_WIDTH = 4
X_HEADS = 4
X_HEAD_DIM = 256
N_EXPERTS = 32
TOP_K = 4
SWIGLU_LIMIT = 7.0
SWIGLU_ALPHA = 1.702

ATTN_TQ = 256
SSD_L = 256
MOE_ROWS = 512
MOE_TILE = 256
MERGE_TM = 512


def _cparams(*sem):
    return pltpu.CompilerParams(dimension_semantics=sem, vmem_limit_bytes=V7X_VMEM_LIMIT_BYTES)


def _split_bf16(v):
    hi = v.astype(BF16)
    lo = (v - hi.astype(F32)).astype(BF16)
    return hi, lo


def _in_proj_kernel(x_ref, nw_ref, w_ref, wdt_ref, ind_a_ref, ind_at_ref, ind_x_ref, ind_xt_ref,
                    gains_ref, o_ref, dt_ref, h_ref, *, roles):
    j = pl.program_id(1)

    @pl.when(j == 0)
    def _():
        x = x_ref[...]
        ms = jnp.mean(x * x, axis=-1, keepdims=True)
        hb = (x * lax.rsqrt(ms + EPS) * nw_ref[...]).astype(BF16)
        h_ref[...] = hb
        dt_ref[...] = jnp.dot(hb, wdt_ref[...], preferred_element_type=F32)

    def head_norm(acc, ind_ref, indt_ref, head_dim, gain_row):
        s = jnp.dot((acc * acc).astype(BF16), ind_ref[...], preferred_element_type=F32)
        r = lax.rsqrt(s * (1.0 / head_dim) + EPS)
        rexp = jnp.dot(r.astype(BF16), indt_ref[...], preferred_element_type=F32)
        return acc * rexp * gain_row

    def cond_for(role):
        c = None
        for jj, r in enumerate(roles):
            if r == role:
                c = (j == jj) if c is None else jnp.logical_or(c, j == jj)
        return c

    for role in sorted(set(roles)):
        @pl.when(cond_for(role))
        def _(role=role):
            acc = jnp.dot(h_ref[...], w_ref[...], preferred_element_type=F32)
            if role == "qa":
                out = head_norm(acc, ind_a_ref, ind_at_ref, A_HEAD_DIM, gains_ref[0:1, :])
            elif role == "ka":
                out = head_norm(acc, ind_a_ref, ind_at_ref, A_HEAD_DIM, gains_ref[1:2, :])
            elif role == "qx":
                out = head_norm(acc, ind_x_ref, ind_xt_ref, X_HEAD_DIM, gains_ref[2:3, :])
            elif role == "sig":
                out = jax.nn.sigmoid(acc)
            else:
                out = acc
            o_ref[...] = out.astype(o_ref.dtype)


def _in_proj(x2d, norm_w, w_main, w_dt, ind_a, ind_at, ind_x, ind_xt, gains, roles, tm, tn):
    t, d = x2d.shape
    n = w_main.shape[1]
    assert t % tm == 0 and n % tn == 0 and len(roles) == n // tn
    const = lambda i, j: (0, 0)
    return pl.pallas_call(
        functools.partial(_in_proj_kernel, roles=roles),
        grid=(t // tm, n // tn),
        in_specs=[
            pl.BlockSpec((tm, d), lambda i, j: (i, 0)),
            pl.BlockSpec((1, d), const),
            pl.BlockSpec((d, tn), lambda i, j: (0, j)),
            pl.BlockSpec((d, V7X_LANES), const),
            pl.BlockSpec(ind_a.shape, const),
            pl.BlockSpec(ind_at.shape, const),
            pl.BlockSpec(ind_x.shape, const),
            pl.BlockSpec(ind_xt.shape, const),
            pl.BlockSpec(gains.shape, const),
        ],
        out_specs=[
            pl.BlockSpec((tm, tn), lambda i, j: (i, j)),
            pl.BlockSpec((tm, V7X_LANES), lambda i, j: (i, 0)),
        ],
        out_shape=[
            jax.ShapeDtypeStruct((t, n), BF16),
            jax.ShapeDtypeStruct((t, V7X_LANES), F32),
        ],
        scratch_shapes=[pltpu.VMEM((tm, d), BF16)],
        compiler_params=_cparams("parallel", "arbitrary"),
        name="in_proj",
    )(x2d, norm_w, w_main, w_dt, ind_a, ind_at, ind_x, ind_xt, gains)


def _attn_kernel(q_ref, *refs, tq, nprev):
    k_refs = refs[:nprev + 1]
    v_refs = refs[nprev + 1:2 * nprev + 2]
    bias_ref, o_ref = refs[2 * nprev + 2:]
    qb = pl.program_id(1)
    nk = (nprev + 1) * tq
    lane = lax.broadcasted_iota(jnp.int32, (1, V7X_LANES), 1)
    col = lax.broadcasted_iota(jnp.int32, (1, nk), 1)
    before_start = col < (nprev - qb) * tq
    for hp in range(A_HEADS // 2):
        ls = slice(hp * V7X_LANES, (hp + 1) * V7X_LANES)
        q2 = q_ref[:, ls]
        kk = jnp.concatenate([r[:, ls] for r in k_refs], axis=0)
        vv = jnp.concatenate([r[:, ls] for r in v_refs], axis=0)
        outs = []
        for hh in range(2):
            sel = (lane < A_HEAD_DIM) if hh == 0 else (lane >= A_HEAD_DIM)
            qm = jnp.where(sel, q2, jnp.zeros_like(q2))
            s = lax.dot_general(qm, kk, (((1,), (1,)), ((), ())), preferred_element_type=F32)
            s = jnp.where(before_start, NEG, s + bias_ref[2 * hp + hh])
            m = jnp.max(s, axis=-1, keepdims=True)
            p = jnp.exp2(s - m)
            l = jnp.sum(p, axis=-1, keepdims=True)
            o = jnp.dot(p.astype(BF16), vv, preferred_element_type=F32)
            outs.append(o / l)
        o_ref[:, ls] = jnp.where(lane < A_HEAD_DIM, outs[0], outs[1]).astype(o_ref.dtype)


def _attention(proj3, bias, q_tile, k_tile, v_tile):
    b, s, _ = proj3.shape
    tq = ATTN_TQ
    width = A_HEADS * A_HEAD_DIM
    left = LEFT_CHUNKS * CHUNK
    assert left % tq == 0 and s % tq == 0
    nprev = left // tq

    def kv_spec(tile, back):
        return pl.BlockSpec((None, tq, width), lambda bi, qi: (bi, jnp.maximum(qi - back, 0), tile))

    in_specs = [pl.BlockSpec((None, tq, width), lambda bi, qi: (bi, qi, q_tile))]
    in_specs += [kv_spec(k_tile, nprev - i) for i in range(nprev + 1)]
    in_specs += [kv_spec(v_tile, nprev - i) for i in range(nprev + 1)]
    in_specs += [pl.BlockSpec(bias.shape, lambda bi, qi: (0, 0, 0), pipeline_mode=pl.Buffered(1))]
    return pl.pallas_call(
        functools.partial(_attn_kernel, tq=tq, nprev=nprev),
        grid=(b, s // tq),
        in_specs=in_specs,
        out_specs=pl.BlockSpec((None, tq, width), lambda bi, qi: (bi, qi, 0)),
        out_shape=jax.ShapeDtypeStruct((b, s, width), BF16),
        compiler_params=_cparams("parallel", "parallel"),
        name="chunk_attn",
    )(proj3, *([proj3] * (2 * nprev + 2)), bias)


def _attn_bias(rel_bias):
    tq = ATTN_TQ
    left = LEFT_CHUNKS * CHUNK
    i = jnp.arange(tq)[:, None]
    j = jnp.arange(left + tq)[None, :]
    m_len = tq + left + tq
    m = jnp.arange(m_len)
    diff = jnp.where(m < left + tq, m, m - m_len)
    v = rel_bias[:, jnp.clip(left - diff, -REL_CLIP, REL_CLIP) + REL_CLIP].astype(F32)
    h = v.shape[0]
    bias = jnp.tile(v, (1, tq))[:, :tq * (m_len - 1)].reshape(h, tq, m_len - 1)[:, :, :left + tq]
    qc, kc = i // CHUNK, j // CHUNK
    in_band = (kc >= qc) & (kc <= qc + LEFT_CHUNKS)
    return jnp.where(in_band[None], bias * LOG2_E, NEG)


def _ssd_kernel(z_ref, x_ref, bc_ref, dt_ref, cwx_ref, cbx_ref, cwbc_ref, cbbc_ref, dtb_ref,
                aneg_ref, dskip_ref, gain_ref, e_ref, o_ref,
                xf_ref, bcf_ref, st_ref, xs_ref, y_ref, *, L):
    c = pl.program_id(1)
    inner = SSM_HEADS * SSM_HEAD_DIM
    gw = inner // SSM_GROUPS
    gs = SSM_GROUPS * SSM_STATE
    tail = V7X_SUBLANES

    @pl.when(c == 0)
    def _():
        xf_ref[0:tail, :] = jnp.zeros((tail, inner), F32)
        bcf_ref[0:tail, :] = jnp.zeros((tail, 2 * gs), F32)
        st_ref[...] = jnp.zeros_like(st_ref)

    xf_ref[tail:, :] = x_ref[...].astype(F32)
    bcf_ref[tail:, :] = bc_ref[...].astype(F32)

    def conv_silu(src_ref, w_ref, b_ref, c0, c1):
        acc = b_ref[:, c0:c1] + w_ref[CONV_WIDTH - 1:CONV_WIDTH, c0:c1] * src_ref[tail:tail + L, c0:c1]
        for k in range(1, CONV_WIDTH):
            acc = acc + (w_ref[CONV_WIDTH - 1 - k:CONV_WIDTH - k, c0:c1]
                         * src_ref[tail - k:tail - k + L, c0:c1])
        return acc * jax.nn.sigmoid(acc)

    for g in range(SSM_GROUPS):
        xs_ref[:, g * gw:(g + 1) * gw] = conv_silu(xf_ref, cwx_ref, cbx_ref, g * gw, (g + 1) * gw)
    bmat = conv_silu(bcf_ref, cwbc_ref, cbbc_ref, 0, gs)
    cmat = conv_silu(bcf_ref, cwbc_ref, cbbc_ref, gs, 2 * gs)
    xf_ref[0:tail, :] = xf_ref[L:L + tail, :]
    bcf_ref[0:tail, :] = bcf_ref[L:L + tail, :]

    pre = dt_ref[...] + dtb_ref[...]
    dt = jnp.maximum(pre, 0.0) + jnp.log1p(jnp.exp(-jnp.abs(pre)))
    a = dt * aneg_ref[...]
    row = lax.broadcasted_iota(jnp.int32, (L, L), 0)
    colm = lax.broadcasted_iota(jnp.int32, (L, L), 1)
    lower = colm <= row
    tri = jnp.where(lower, 1.0, 0.0).astype(F32)
    cs = jnp.dot(tri, a, precision=HIGHEST, preferred_element_type=F32)
    cs_t = cs.T
    dt_t = dt.T
    cs_last = cs[L - 1:L, :]
    w_state = dt * jnp.exp(cs_last - cs)
    e_cs = jnp.exp(cs)
    chunk_decay = jnp.broadcast_to(jnp.exp(cs_last), (tail, V7X_LANES))
    stacked = jnp.concatenate([w_state, e_cs, chunk_decay], axis=0)
    s_hi, s_lo = _split_bf16(stacked)
    expanded = (jnp.dot(s_hi, e_ref[...], preferred_element_type=F32)
                + jnp.dot(s_lo, e_ref[...], preferred_element_type=F32))
    w_state_e = expanded[0:L]
    e_cs_e = expanded[L:2 * L]
    decay_e = expanded[2 * L:2 * L + 1]

    lane = lax.broadcasted_iota(jnp.int32, (1, V7X_LANES), 1)
    pairs_per_group = gw // V7X_LANES
    for g in range(SSM_GROUPS):
        bg = bmat[:, g * SSM_STATE:(g + 1) * SSM_STATE]
        cg = cmat[:, g * SSM_STATE:(g + 1) * SSM_STATE].astype(BF16)
        cb = lax.dot_general(cg, bg.astype(BF16), (((1,), (1,)), ((), ())),
                             preferred_element_type=F32)
        state_b = st_ref[g].astype(BF16)
        y_off = jnp.dot(cg, state_b, preferred_element_type=F32) * e_cs_e[:, g * gw:(g + 1) * gw]
        for pr in range(pairs_per_group):
            c0 = g * gw + pr * V7X_LANES
            xp = xs_ref[:, c0:c0 + V7X_LANES]
            xpb = xp.astype(BF16)
            acc = y_off[:, pr * V7X_LANES:(pr + 1) * V7X_LANES] + dskip_ref[:, c0:c0 + V7X_LANES] * xp
            for hh in range(2):
                h = c0 // SSM_HEAD_DIM + hh
                d = cs[:, h:h + 1] - cs_t[h:h + 1, :]
                m = cb * jnp.exp(jnp.where(lower, d, NEG)) * dt_t[h:h + 1, :]
                sel = (lane < SSM_HEAD_DIM) if hh == 0 else (lane >= SSM_HEAD_DIM)
                xm = jnp.where(sel, xpb, jnp.zeros_like(xpb))
                acc = acc + jnp.dot(m.astype(BF16), xm, preferred_element_type=F32)
            y_ref[:, c0:c0 + V7X_LANES] = acc
        xw = (xs_ref[:, g * gw:(g + 1) * gw] * w_state_e[:, g * gw:(g + 1) * gw]).astype(BF16)
        new = jnp.dot(bg.T.astype(BF16), xw, preferred_element_type=F32)
        st_ref[g] = st_ref[g] * decay_e[:, g * gw:(g + 1) * gw] + new

    for g in range(SSM_GROUPS):
        sl = slice(g * gw, (g + 1) * gw)
        zz = z_ref[:, sl].astype(F32)
        yz = y_ref[:, sl] * (zz * jax.nn.sigmoid(zz))
        ms = jnp.mean(yz * yz, axis=-1, keepdims=True)
        o_ref[:, sl] = (yz * lax.rsqrt(ms + EPS) * gain_ref[:, sl]).astype(o_ref.dtype)


def _ssd(proj3, dt3, cwx, cbx, cwbc, cbbc, dtb, aneg, dskip_e, gain, e_mat, z_blk, x_blk, bc_blk):
    b, s, _ = proj3.shape
    L = SSD_L
    assert s % L == 0
    inner = SSM_HEADS * SSM_HEAD_DIM
    gs2 = 2 * SSM_GROUPS * SSM_STATE
    const = lambda bi, ci: (0, 0)
    full = lambda a: pl.BlockSpec(a.shape, const)
    return pl.pallas_call(
        functools.partial(_ssd_kernel, L=L),
        grid=(b, s // L),
        in_specs=[
            pl.BlockSpec((None, L, inner), lambda bi, ci: (bi, ci, z_blk)),
            pl.BlockSpec((None, L, inner), lambda bi, ci: (bi, ci, x_blk)),
            pl.BlockSpec((None, L, gs2), lambda bi, ci: (bi, ci, bc_blk)),
            pl.BlockSpec((None, L, V7X_LANES), lambda bi, ci: (bi, ci, 0)),
            full(cwx), full(cbx), full(cwbc), full(cbbc), full(dtb), full(aneg), full(dskip_e),
            full(gain), full(e_mat),
        ],
        out_specs=pl.BlockSpec((None, L, inner), lambda bi, ci: (bi, ci, 0)),
        out_shape=jax.ShapeDtypeStruct((b, s, inner), BF16),
        scratch_shapes=[
            pltpu.VMEM((L + V7X_SUBLANES, inner), F32),
            pltpu.VMEM((L + V7X_SUBLANES, gs2), F32),
            pltpu.VMEM((SSM_GROUPS, SSM_STATE, inner // SSM_GROUPS), F32),
            pltpu.VMEM((L, inner), F32),
            pltpu.VMEM((L, inner), F32),
        ],
        compiler_params=_cparams("parallel", "arbitrary"),
        name="ssd",
    )(proj3, proj3, proj3, dt3, cwx, cbx, cwbc, cbbc, dtb, aneg, dskip_e, gain, e_mat)


def _mem_kv_kernel(mem_ref, g_ref, w_ref, kg_ref, k_ref, v_ref):
    m = mem_ref[...]
    ms = jnp.mean(m * m, axis=-1, keepdims=True)
    mn = (m * lax.rsqrt(ms + EPS) * g_ref[...]).astype(BF16)
    kv = jnp.dot(mn, w_ref[...], preferred_element_type=F32)
    width = X_HEADS * X_HEAD_DIM
    for h in range(X_HEADS):
        sl = slice(h * X_HEAD_DIM, (h + 1) * X_HEAD_DIM)
        kh = kv[:, sl]
        r = lax.rsqrt(jnp.mean(kh * kh, axis=-1, keepdims=True) + EPS)
        k_ref[:, sl] = (kh * r * kg_ref[...]).astype(k_ref.dtype)
    v_ref[...] = kv[:, width:].astype(v_ref.dtype)


def _mem_kv(mem, norm_mem, w_kv, k_gain):
    b, m, d = mem.shape
    width = X_HEADS * X_HEAD_DIM
    const = lambda bi: (0, 0)
    return pl.pallas_call(
        _mem_kv_kernel,
        grid=(b,),
        in_specs=[
            pl.BlockSpec((None, m, d), lambda bi: (bi, 0, 0)),
            pl.BlockSpec((1, d), const),
            pl.BlockSpec((d, 2 * width), const),
            pl.BlockSpec((1, X_HEAD_DIM), const),
        ],
        out_specs=[pl.BlockSpec((None, m, width), lambda bi: (bi, 0, 0))] * 2,
        out_shape=[jax.ShapeDtypeStruct((b, m, width), BF16)] * 2,
        compiler_params=_cparams("parallel"),
        name="mem_kv",
    )(mem, norm_mem, w_kv, k_gain)


def _mem_attn_kernel(q_ref, k_ref, v_ref, o_ref):
    for h in range(X_HEADS):
        sl = slice(h * X_HEAD_DIM, (h + 1) * X_HEAD_DIM)
        s = lax.dot_general(q_ref[:, sl], k_ref[:, sl], (((1,), (1,)), ((), ())),
                            preferred_element_type=F32)
        m = jnp.max(s, axis=-1, keepdims=True)
        p = jnp.exp(s - m)
        l = jnp.sum(p, axis=-1, keepdims=True)
        o = jnp.dot(p.astype(BF16), v_ref[:, sl], preferred_element_type=F32)
        o_ref[:, sl] = (o / l).astype(o_ref.dtype)


def _mem_attn(proj3, k, v, q_blk, tq):
    b, s, _ = proj3.shape
    m = k.shape[1]
    width = X_HEADS * X_HEAD_DIM
    assert s % tq == 0
    return pl.pallas_call(
        _mem_attn_kernel,
        grid=(b, s // tq),
        in_specs=[
            pl.BlockSpec((None, tq, width), lambda bi, qi: (bi, qi, q_blk)),
            pl.BlockSpec((None, m, width), lambda bi, qi: (bi, 0, 0)),
            pl.BlockSpec((None, m, width), lambda bi, qi: (bi, 0, 0)),
        ],
        out_specs=pl.BlockSpec((None, tq, width), lambda bi, qi: (bi, qi, 0)),
        out_shape=jax.ShapeDtypeStruct((b, s, width), BF16),
        compiler_params=_cparams("parallel", "parallel"),
        name="mem_attn",
    )(proj3, k, v)


def _merge_kernel(x_ref, ya_ref, yb_ref, yc_ref, g0_ref, g1_ref, g2_ref, wa_ref, wb_ref, wc_ref,
                  wo_ref, nf_ref, wrh_ref, wrl_ref, br_ref,
                  x1_ref, h2_ref, route_ref, cnt_ref, *, tm, rt):
    merged = (g0_ref[...].astype(F32) * jnp.dot(ya_ref[...], wa_ref[...], preferred_element_type=F32)
              + g1_ref[...].astype(F32) * jnp.dot(yb_ref[...], wb_ref[...], preferred_element_type=F32)
              + g2_ref[...].astype(F32) * jnp.dot(yc_ref[...], wc_ref[...], preferred_element_type=F32))
    x1 = x_ref[...] + jnp.dot(merged.astype(BF16), wo_ref[...], preferred_element_type=F32)
    x1_ref[...] = x1
    ms = jnp.mean(x1 * x1, axis=-1, keepdims=True)
    h2 = x1 * lax.rsqrt(ms + EPS) * nf_ref[...]
    h2_ref[...] = h2.astype(h2_ref.dtype)

    h_hi, h_lo = _split_bf16(h2)
    logits_all = (jnp.dot(h_hi, wrh_ref[...], preferred_element_type=F32)
                  + jnp.dot(h_lo, wrh_ref[...], preferred_element_type=F32)
                  + jnp.dot(h_hi, wrl_ref[...], preferred_element_type=F32)) + br_ref[...]
    for sub in range(tm // rt):
        _route_tile(logits_all[sub * rt:(sub + 1) * rt], route_ref.at[pl.ds(sub * rt, rt)],
                    cnt_ref.at[sub], rt)


def _route_tile(logits, route_ref, cnt_ref, tm):
    lane = lax.broadcasted_iota(jnp.int32, (tm, V7X_LANES), 1)
    lane_f = lane.astype(F32)
    work = jnp.where(lane < N_EXPERTS, logits, NEG)
    sel_val, sel_oh = [], []
    for _ in range(TOP_K):
        mval = jnp.max(work, axis=-1, keepdims=True)
        ik = jnp.min(jnp.where(work == mval, lane_f, float(V7X_LANES)), axis=-1, keepdims=True)
        oh = lane_f == ik
        work = jnp.where(oh, NEG, work)
        sel_val.append(mval)
        sel_oh.append(oh)
    ex = [jnp.exp(v - sel_val[0]) for v in sel_val]
    denom = ex[0] + ex[1] + ex[2] + ex[3]

    oh_all = jnp.zeros((tm, V7X_LANES), F32)
    for oh in sel_oh:
        oh_all = oh_all + jnp.where(oh, 1.0, 0.0)
    row = lax.broadcasted_iota(jnp.int32, (tm, tm), 0)
    colm = lax.broadcasted_iota(jnp.int32, (tm, tm), 1)
    strict = jnp.where(colm < row, 1.0, 0.0).astype(BF16)
    before = jnp.dot(strict, oh_all.astype(BF16), preferred_element_type=F32)
    cnt = jnp.sum(oh_all, axis=0, keepdims=True)
    cnt8 = jnp.floor((cnt + (V7X_SUBLANES - 1.0)) * (1.0 / V7X_SUBLANES)) * V7X_SUBLANES
    cnt8 = jnp.broadcast_to(cnt8, (V7X_SUBLANES, V7X_LANES))
    er = lax.broadcasted_iota(jnp.int32, (V7X_LANES, V7X_LANES), 0)
    ec = lax.broadcasted_iota(jnp.int32, (V7X_LANES, V7X_LANES), 1)
    earlier = jnp.where(er < ec, 1.0, 0.0).astype(BF16)
    run_start = jnp.dot(cnt8.astype(BF16), earlier, preferred_element_type=F32)[0:1, :]
    slot = before + run_start
    route = jnp.zeros((tm, V7X_LANES), F32)
    for k in range(TOP_K):
        pos = jnp.sum(jnp.where(sel_oh[k], slot, 0.0), axis=-1, keepdims=True)
        route = jnp.where(lane == k, pos, route)
        route = jnp.where(lane == TOP_K + k, ex[k] / denom, route)
    route_ref[...] = route
    cnt_ref[...] = cnt8


def _merge(x2d, ya, yb, yc, proj, wa, wb, wc, wo, nf, wr, br, gate_blk, tm, rt):
    t, d = x2d.shape
    assert t % tm == 0 and tm % rt == 0
    wr_hi, wr_lo = _split_bf16(wr)
    const = lambda i: (0, 0)
    full = lambda a: pl.BlockSpec(a.shape, const)
    rows = lambda w: pl.BlockSpec((tm, w), lambda i: (i, 0))
    return pl.pallas_call(
        functools.partial(_merge_kernel, tm=tm, rt=rt),
        grid=(t // tm,),
        in_specs=[
            rows(d), rows(ya.shape[1]), rows(yb.shape[1]), rows(yc.shape[1]),
            pl.BlockSpec((tm, d), lambda i: (i, gate_blk)),
            pl.BlockSpec((tm, d), lambda i: (i, gate_blk + 1)),
            pl.BlockSpec((tm, d), lambda i: (i, gate_blk + 2)),
            full(wa), full(wb), full(wc), full(wo), full(nf), full(wr_hi), full(wr_lo), full(br),
        ],
        out_specs=[rows(d), rows(d), rows(V7X_LANES),
                   pl.BlockSpec((tm // rt, V7X_SUBLANES, V7X_LANES), lambda i: (i, 0, 0))],
        out_shape=[
            jax.ShapeDtypeStruct((t, d), F32),
            jax.ShapeDtypeStruct((t, d), BF16),
            jax.ShapeDtypeStruct((t, V7X_LANES), F32),
            jax.ShapeDtypeStruct((t // rt, V7X_SUBLANES, V7X_LANES), F32),
        ],
        compiler_params=_cparams("parallel"),
        name="merge_route",
    )(x2d, ya, yb, yc, proj, proj, proj, wa, wb, wc, wo, nf, wr_hi, wr_lo, br)


RUN = V7X_SUBLANES
META_SRC = N_EXPERTS
META_DST = 2 * N_EXPERTS


PIECE_SIZES = (32, 16, 8, 4, 2, 1)


def _run_pieces(meta_ref, body):
    assert MOE_TILE // RUN <= 2 * PIECE_SIZES[0] - 1

    def per_expert(e, carry):
        n = meta_ref[0, e]
        for idx, size in enumerate(PIECE_SIZES):
            @pl.when((n & size) != 0)
            def _(idx=idx, size=size):
                body(e, n & ~(2 * size - 1), size, idx % 2)
        return carry

    lax.fori_loop(0, N_EXPERTS, per_expert, 0)


def _piece_rows(meta_ref, col, e, first, size):
    return pl.ds(pl.multiple_of(meta_ref[0, col + e] + first * RUN, RUN), size * RUN)


def _dispatch_kernel(meta_ref, prev_meta_ref, zstart_ref, zcnt_ref, nused_ref, h_ref, route_ref,
                     xs_ref, sorted_ref, zero_ref, sems, *, tm, ns, rows, n_blocks):
    i = pl.program_id(0)
    slot = lax.rem(i, 2)
    pos_t = route_ref[...].T
    q = lax.broadcasted_iota(jnp.int32, (ns, tm), 0).astype(F32)
    perm = jnp.zeros((ns, tm), F32)
    for k in range(TOP_K):
        perm = perm + jnp.where(q == pos_t[k:k + 1, :], 1.0, 0.0)
    sorted_ref[slot] = jnp.dot(perm.astype(BF16), h_ref[...], preferred_element_type=F32)

    def piece(m_ref, sl, e, first, size):
        return pltpu.make_async_copy(
            sorted_ref.at[sl, _piece_rows(m_ref, META_SRC, e, first, size)],
            xs_ref.at[_piece_rows(m_ref, META_DST, e, first, size)], sems.at[sl])

    def start_all(m_ref, sl):
        _run_pieces(m_ref, lambda e, f, s, prio: piece(m_ref, sl, e, f, s).start(priority=prio))

    def wait_all(m_ref, sl):
        _run_pieces(m_ref, lambda e, f, s, prio: piece(m_ref, sl, e, f, s).wait())

    start_all(meta_ref, slot)

    @pl.when(i > 0)
    def _():
        wait_all(prev_meta_ref, 1 - slot)

    @pl.when(i == pl.num_programs(0) - 1)
    def _():
        wait_all(meta_ref, slot)
        zero_ref[...] = jnp.zeros_like(zero_ref)
        n_used = nused_ref[0]
        sem = sems.at[2]

        def pad_piece(e, first, size):
            dst = pl.ds(pl.multiple_of(zstart_ref[e] + first * RUN, RUN), size * RUN)
            return pltpu.make_async_copy(zero_ref.at[pl.ds(0, size * RUN)], xs_ref.at[dst], sem)

        def block_copy(blk):
            return pltpu.make_async_copy(zero_ref, xs_ref.at[pl.ds(blk * rows, rows)], sem)

        def pad_loop(fn):
            assert rows // RUN <= 2 * PIECE_SIZES[0]

            def per_expert(e, carry):
                n = zcnt_ref[e]
                for size in PIECE_SIZES:
                    @pl.when((n & size) != 0)
                    def _(size=size):
                        fn(pad_piece(e, n & ~(2 * size - 1), size))
                return carry
            lax.fori_loop(0, N_EXPERTS, per_expert, 0)

        def blk_loop(fn):
            def body(blk, carry):
                fn(block_copy(blk))
                return carry
            lax.fori_loop(n_used, n_blocks, body, 0)

        pad_loop(lambda cp: cp.start())
        blk_loop(lambda cp: cp.start())
        pad_loop(lambda cp: cp.wait())
        blk_loop(lambda cp: cp.wait())


def _dispatch(meta, zstart, zcnt, n_used, h2, route, n_rows, tm, ns):
    t, d = h2.shape
    rows = MOE_ROWS
    smem = pl.BlockSpec(memory_space=pltpu.SMEM)
    return pl.pallas_call(
        functools.partial(_dispatch_kernel, tm=tm, ns=ns, rows=rows, n_blocks=n_rows // rows),
        grid=(t // tm,),
        in_specs=[
            pl.BlockSpec((None, 1, V7X_LANES), lambda i: (i, 0, 0), memory_space=pltpu.SMEM),
            pl.BlockSpec((None, 1, V7X_LANES), lambda i: (jnp.maximum(i - 1, 0), 0, 0),
                         memory_space=pltpu.SMEM),
            smem, smem, smem,
            pl.BlockSpec((tm, d), lambda i: (i, 0)),
            pl.BlockSpec((tm, V7X_LANES), lambda i: (i, 0)),
        ],
        out_specs=pl.BlockSpec(memory_space=pl.ANY),
        out_shape=jax.ShapeDtypeStruct((n_rows, d), F32),
        scratch_shapes=[pltpu.VMEM((2, ns, d), F32), pltpu.VMEM((rows, d), F32),
                        pltpu.SemaphoreType.DMA((3,))],
        compiler_params=_cparams("arbitrary"),
        name="moe_dispatch",
    )(meta, meta, zstart, zcnt, n_used, h2, route)


def _experts_kernel(bexp_ref, nused_ref, x_ref, wg_ref, bg_ref, wu_ref, bu_ref, wd_ref, bd_ref, y_ref,
                    wgb_ref, wub_ref, wdb_ref):
    b = pl.program_id(0)
    new_expert = jnp.logical_or(b == 0, bexp_ref[b] != bexp_ref[jnp.maximum(b - 1, 0)])

    @pl.when(new_expert)
    def _():
        wgb_ref[...] = wg_ref[...].astype(BF16)
        wub_ref[...] = wu_ref[...].astype(BF16)
        wdb_ref[...] = wd_ref[...].astype(BF16)

    @pl.when(b < nused_ref[0])
    def _():
        xb = x_ref[...].astype(BF16)
        g = jnp.dot(xb, wgb_ref[...], preferred_element_type=F32) + bg_ref[...]
        u = jnp.dot(xb, wub_ref[...], preferred_element_type=F32) + bu_ref[...]
        g = jnp.minimum(g, SWIGLU_LIMIT)
        u = jnp.clip(u, -SWIGLU_LIMIT, SWIGLU_LIMIT)
        act = (u + 1.0) * g * jax.nn.sigmoid(SWIGLU_ALPHA * g)
        y_ref[...] = jnp.dot(act.astype(BF16), wdb_ref[...], preferred_element_type=F32) + bd_ref[...]

    @pl.when(b >= nused_ref[0])
    def _():
        y_ref[...] = jnp.zeros_like(y_ref)


def _experts(block_exp, n_used, xs, wg, bg, wu, bu, wd, bd):
    n_rows, d = xs.shape
    de = wg.shape[2]
    rows = MOE_ROWS
    n_blocks = n_rows // rows
    xmap = lambda b, be, nu: (jnp.minimum(b, nu[0] - 1), 0)
    wmap = lambda b, be, nu: (be[b], 0, 0)
    return pl.pallas_call(
        _experts_kernel,
        grid_spec=pltpu.PrefetchScalarGridSpec(
            num_scalar_prefetch=2,
            grid=(n_blocks,),
            in_specs=[
                pl.BlockSpec((rows, d), xmap),
                pl.BlockSpec((None, d, de), wmap), pl.BlockSpec((None, 1, de), wmap),
                pl.BlockSpec((None, d, de), wmap), pl.BlockSpec((None, 1, de), wmap),
                pl.BlockSpec((None, de, d), wmap), pl.BlockSpec((None, 1, d), wmap),
            ],
            out_specs=pl.BlockSpec((rows, d), lambda b, be, nu: (b, 0)),
            scratch_shapes=[pltpu.VMEM((d, de), BF16), pltpu.VMEM((d, de), BF16),
                            pltpu.VMEM((de, d), BF16)],
        ),
        out_shape=jax.ShapeDtypeStruct((n_rows, d), F32),
        compiler_params=_cparams("arbitrary"),
        name="moe_experts",
    )(block_exp, n_used, xs, wg, bg, wu, bu, wd, bd)


def _combine_kernel(meta_ref, next_meta_ref, route_ref, x1_ref, yb_ref, o_ref, sorted_ref, sems, *,
                    tm, ns):
    i = pl.program_id(0)
    slot = lax.rem(i, 2)

    def piece(m_ref, sl, e, first, size):
        return pltpu.make_async_copy(
            yb_ref.at[_piece_rows(m_ref, META_DST, e, first, size)],
            sorted_ref.at[sl, _piece_rows(m_ref, META_SRC, e, first, size)], sems.at[sl])

    def start_all(m_ref, sl):
        _run_pieces(m_ref, lambda e, f, s, prio: piece(m_ref, sl, e, f, s).start(priority=prio))

    @pl.when(i == 0)
    def _():
        sorted_ref[...] = jnp.zeros_like(sorted_ref)
        start_all(meta_ref, slot)

    @pl.when(i + 1 < pl.num_programs(0))
    def _():
        start_all(next_meta_ref, 1 - slot)

    _run_pieces(meta_ref, lambda e, f, s, prio: piece(meta_ref, slot, e, f, s).wait())

    ys = sorted_ref[slot].astype(BF16)
    route = route_ref[...]
    q = lax.broadcasted_iota(jnp.int32, (tm, ns), 1).astype(F32)
    wmat = jnp.zeros((tm, ns), F32)
    for k in range(TOP_K):
        wmat = wmat + jnp.where(q == route[:, k:k + 1], route[:, TOP_K + k:TOP_K + k + 1], 0.0)
    o_ref[...] = x1_ref[...] + jnp.dot(wmat.astype(BF16), ys, preferred_element_type=F32)


def _combine(meta, route, x1, yb, tm, ns):
    t, d = x1.shape
    n_tiles = t // tm
    return pl.pallas_call(
        functools.partial(_combine_kernel, tm=tm, ns=ns),
        grid=(t // tm,),
        in_specs=[
            pl.BlockSpec((None, 1, V7X_LANES), lambda i: (i, 0, 0), memory_space=pltpu.SMEM),
            pl.BlockSpec((None, 1, V7X_LANES), lambda i: (jnp.minimum(i + 1, n_tiles - 1), 0, 0),
                         memory_space=pltpu.SMEM),
            pl.BlockSpec((tm, V7X_LANES), lambda i: (i, 0)),
            pl.BlockSpec((tm, d), lambda i: (i, 0)),
            pl.BlockSpec(memory_space=pl.ANY),
        ],
        out_specs=pl.BlockSpec((tm, d), lambda i: (i, 0)),
        out_shape=jax.ShapeDtypeStruct((t, d), F32),
        scratch_shapes=[pltpu.VMEM((2, ns, d), F32), pltpu.SemaphoreType.DMA((2,))],
        compiler_params=_cparams("arbitrary"),
        name="moe_combine",
    )(meta, meta, route, x1, yb)


def _head_indicator(width, head_dim):
    lane_head = jnp.arange(width) // head_dim
    return (lane_head[:, None] == jnp.arange(V7X_LANES)[None, :]).astype(BF16)


def _layer(x, mem, norm_mix, w_in, a_q_gain, a_k_gain, a_rel_bias, conv_w, conv_b, dt_bias, a_log,
           d_skip, ssm_norm, norm_mem, w_mem_kv, x_q_gain, x_k_gain, w_br_a, w_br_b, w_br_c, w_out,
           norm_ffn, w_router, b_router, w_gate, b_gate, w_up, b_up, w_down, b_down):
    b, s, d = x.shape
    t = b * s
    a_width = A_HEADS * A_HEAD_DIM
    inner = SSM_HEADS * SSM_HEAD_DIM
    gs = SSM_GROUPS * SSM_STATE
    x_width = X_HEADS * X_HEAD_DIM
    assert d == a_width == x_width and inner == 2 * d and 2 * gs == d

    o_q, o_k, o_v = 0, a_width, 2 * a_width
    o_z = 3 * a_width
    o_xbc = o_z + inner
    o_dt = o_xbc + inner + 2 * gs
    o_qx = o_dt + SSM_HEADS
    o_gate = o_qx + x_width
    cols = lambda o, w: w_in[:, o:o + w]
    w_main = jnp.concatenate([
        cols(o_z, inner), cols(o_xbc, inner), cols(o_q, a_width), cols(o_k, a_width),
        cols(o_v, a_width), cols(o_xbc + inner, 2 * gs), cols(o_qx, x_width), cols(o_gate, 3 * d),
    ], axis=1).astype(BF16)
    roles = ("plain", "plain", "plain", "plain", "qa", "ka", "plain", "plain", "qx", "sig", "sig", "sig")
    w_dt = jnp.pad(cols(o_dt, SSM_HEADS), ((0, 0), (0, V7X_LANES - SSM_HEADS))).astype(BF16)
    ind_a = _head_indicator(a_width, A_HEAD_DIM)
    ind_x = _head_indicator(x_width, X_HEAD_DIM)
    gains = jnp.zeros((V7X_SUBLANES, d), F32)
    gains = gains.at[0].set(jnp.tile(a_q_gain, A_HEADS) * (A_HEAD_DIM ** -0.5 * LOG2_E))
    gains = gains.at[1].set(jnp.tile(a_k_gain, A_HEADS))
    gains = gains.at[2].set(jnp.tile(x_q_gain, X_HEADS) * X_HEAD_DIM ** -0.5)

    x2d = x.reshape(t, d)
    proj, dt_raw = _in_proj(x2d, norm_mix.reshape(1, d), w_main, w_dt, ind_a, ind_a.T, ind_x, ind_x.T,
                            gains, roles, tm=min(1024, t), tn=d)
    proj3 = proj.reshape(b, s, proj.shape[1])

    y_a = _attention(proj3, _attn_bias(a_rel_bias), q_tile=4, k_tile=5, v_tile=6)

    pad_h = lambda v: jnp.pad(v.astype(F32), (0, V7X_LANES - SSM_HEADS)).reshape(1, V7X_LANES)
    e_mat = _head_indicator(inner, SSM_HEAD_DIM).T
    y_b = _ssd(proj3, dt_raw.reshape(b, s, V7X_LANES),
               conv_w[:, :inner], conv_b[:inner].reshape(1, inner),
               conv_w[:, inner:], conv_b[inner:].reshape(1, 2 * gs),
               pad_h(dt_bias), pad_h(-jnp.exp(a_log.astype(F32))),
               jnp.repeat(d_skip.astype(F32), SSM_HEAD_DIM).reshape(1, inner),
               ssm_norm.reshape(1, inner), e_mat, z_blk=0, x_blk=1, bc_blk=7)

    k_mem, v_mem = _mem_kv(mem, norm_mem.reshape(1, d), w_mem_kv.astype(BF16),
                           x_k_gain.reshape(1, X_HEAD_DIM))
    y_c = _mem_attn(proj3, k_mem, v_mem, q_blk=8, tq=min(512, s))

    w_r = jnp.pad(w_router, ((0, 0), (0, V7X_LANES - N_EXPERTS)))
    b_r = jnp.pad(b_router, (0, V7X_LANES - N_EXPERTS)).reshape(1, V7X_LANES)
    tm_moe = min(MOE_TILE, t)
    n_tiles = t // tm_moe
    x1, h2, route, tile_cnt = _merge(
        x2d, y_a.reshape(t, a_width), y_b.reshape(t, inner), y_c.reshape(t, x_width), proj,
        w_br_a.astype(BF16), w_br_b.astype(BF16), w_br_c.astype(BF16), w_out.astype(BF16),
        norm_ffn.reshape(1, d), w_r, b_r, gate_blk=9, tm=min(MERGE_TM, t), rt=tm_moe)

    rows = MOE_ROWS
    ns = -(-(tm_moe * TOP_K + N_EXPERTS * (RUN - 1)) // V7X_LANES) * V7X_LANES
    n_blocks = -(-(t * TOP_K + n_tiles * N_EXPERTS * (RUN - 1) + N_EXPERTS * (rows - 1)) // rows)
    n_rows = n_blocks * rows
    cnt8 = tile_cnt[:, 0, :N_EXPERTS].astype(jnp.int32)
    total = jnp.sum(cnt8, axis=0)
    padded = (total + rows - 1) // rows * rows
    pad_ends = jnp.cumsum(padded)
    pad_starts = pad_ends - padded
    n_used = (pad_ends[-1] // rows).reshape(1).astype(jnp.int32)
    blk = jnp.minimum(jnp.arange(n_blocks, dtype=jnp.int32), n_used[0] - 1)
    block_exp = jnp.minimum(jnp.sum(pad_ends[None, :] <= (blk * rows)[:, None], axis=1),
                            N_EXPERTS - 1).astype(jnp.int32)
    src_row = jnp.cumsum(cnt8, axis=1) - cnt8
    dst_row = pad_starts[None, :] + jnp.cumsum(cnt8, axis=0) - cnt8
    meta = jnp.concatenate([cnt8 // RUN, src_row, dst_row, jnp.zeros_like(cnt8)], axis=1)
    meta = meta.reshape(n_tiles, 1, 4 * N_EXPERTS).astype(jnp.int32)
    zstart = (pad_starts + total).astype(jnp.int32)
    zcnt = ((padded - total) // RUN).astype(jnp.int32)

    xs = _dispatch(meta, zstart, zcnt, n_used, h2, route, n_rows, tm_moe, ns)
    yb = _experts(block_exp, n_used, xs,
                  w_gate, b_gate.reshape(N_EXPERTS, 1, -1),
                  w_up, b_up.reshape(N_EXPERTS, 1, -1),
                  w_down, b_down.reshape(N_EXPERTS, 1, -1))
    out = _combine(meta, route, x1, yb, tm_moe, ns)
    return out.reshape(b, s, d)


def kernel(x, mem, norm_mix, w_in, a_q_gain, a_k_gain, a_rel_bias, conv_w, conv_b, dt_bias, a_log, d_skip, ssm_norm, norm_mem, w_mem_kv, x_q_gain, x_k_gain, w_br_a, w_br_b, w_br_c, w_out, norm_ffn, w_router, b_router, w_gate, b_gate, w_up, b_up, w_down, b_down):
    for l in range(norm_mix.shape[0]):
        x = _layer(x, mem, norm_mix[l], w_in[l], a_q_gain[l], a_k_gain[l], a_rel_bias[l], conv_w[l],
                   conv_b[l], dt_bias[l], a_log[l], d_skip[l], ssm_norm[l], norm_mem[l], w_mem_kv[l],
                   x_q_gain[l], x_k_gain[l], w_br_a[l], w_br_b[l], w_br_c[l], w_out[l], norm_ffn[l],
                   w_router[l], b_router[l], w_gate[l], b_gate[l], w_up[l], b_up[l], w_down[l],
                   b_down[l])
    return x
```

```python
import functools

import jax
import jax.numpy as jnp
from jax import lax
from jax.experimental import pallas as pl
from jax.experimental.pallas import tpu as pltpu

F32 = jnp.float32
BF16 = jnp.bfloat16
HIGHEST = lax.Precision.HIGHEST

V7X_LANES = 128
V7X_SUBLANES = 8
V7X_VMEM_LIMIT_BYTES = 56 * 1024 * 1024

EPS = 1e-6
LOG2_E = 1.4426950408889634
NEG = -1e30

CHUNK = 64
A_HEADS = 16
A_HEAD_DIM = 64
LEFT_CHUNKS = 8
REL_CLIP = 128
SSM_HEADS = 32
SSM_HEAD_DIM = 64
SSM_GROUPS = 4
SSM_STATE = 128
CONV_WIDTH = 4
X_HEADS = 4
X_HEAD_DIM = 256
N_EXPERTS = 32
TOP_K = 4
SWIGLU_LIMIT = 7.0
SWIGLU_ALPHA = 1.702

ATTN_TQ = 256
SSD_L = 256
MOE_ROWS = 512
MOE_TILE = 256
MERGE_TM = 512


def _cparams(*sem):
    return pltpu.CompilerParams(dimension_semantics=sem, vmem_limit_bytes=V7X_VMEM_LIMIT_BYTES)


def _split_bf16(v):
    hi = v.astype(BF16)
    lo = (v - hi.astype(F32)).astype(BF16)
    return hi, lo


def _in_proj_kernel(x_ref, nw_ref, w_ref, wdt_ref, ind_a_ref, ind_at_ref, ind_x_ref, ind_xt_ref,
                    gains_ref, o_ref, dt_ref, h_ref, *, roles):
    j = pl.program_id(1)

    @pl.when(j == 0)
    def _():
        x = x_ref[...]
        ms = jnp.mean(x * x, axis=-1, keepdims=True)
        hb = (x * lax.rsqrt(ms + EPS) * nw_ref[...]).astype(BF16)
        h_ref[...] = hb
        dt_ref[...] = jnp.dot(hb, wdt_ref[...], preferred_element_type=F32)

    def head_norm(acc, ind_ref, indt_ref, head_dim, gain_row):
        s = jnp.dot((acc * acc).astype(BF16), ind_ref[...], preferred_element_type=F32)
        r = lax.rsqrt(s * (1.0 / head_dim) + EPS)
        rexp = jnp.dot(r.astype(BF16), indt_ref[...], preferred_element_type=F32)
        return acc * rexp * gain_row

    def cond_for(role):
        c = None
        for jj, r in enumerate(roles):
            if r == role:
                c = (j == jj) if c is None else jnp.logical_or(c, j == jj)
        return c

    for role in sorted(set(roles)):
        @pl.when(cond_for(role))
        def _(role=role):
            acc = jnp.dot(h_ref[...], w_ref[...], preferred_element_type=F32)
            if role == "qa":
                out = head_norm(acc, ind_a_ref, ind_at_ref, A_HEAD_DIM, gains_ref[0:1, :])
            elif role == "ka":
                out = head_norm(acc, ind_a_ref, ind_at_ref, A_HEAD_DIM, gains_ref[1:2, :])
            elif role == "qx":
                out = head_norm(acc, ind_x_ref, ind_xt_ref, X_HEAD_DIM, gains_ref[2:3, :])
            elif role == "sig":
                out = jax.nn.sigmoid(acc)
            else:
                out = acc
            o_ref[...] = out.astype(o_ref.dtype)


def _in_proj(x2d, norm_w, w_main, w_dt, ind_a, ind_at, ind_x, ind_xt, gains, roles, tm, tn):
    t, d = x2d.shape
    n = w_main.shape[1]
    assert t % tm == 0 and n % tn == 0 and len(roles) == n // tn
    const = lambda i, j: (0, 0)
    return pl.pallas_call(
        functools.partial(_in_proj_kernel, roles=roles),
        grid=(t // tm, n // tn),
        in_specs=[
            pl.BlockSpec((tm, d), lambda i, j: (i, 0)),
            pl.BlockSpec((1, d), const),
            pl.BlockSpec((d, tn), lambda i, j: (0, j)),
            pl.BlockSpec((d, V7X_LANES), const),
            pl.BlockSpec(ind_a.shape, const),
            pl.BlockSpec(ind_at.shape, const),
            pl.BlockSpec(ind_x.shape, const),
            pl.BlockSpec(ind_xt.shape, const),
            pl.BlockSpec(gains.shape, const),
        ],
        out_specs=[
            pl.BlockSpec((tm, tn), lambda i, j: (i, j)),
            pl.BlockSpec((tm, V7X_LANES), lambda i, j: (i, 0)),
        ],
        out_shape=[
            jax.ShapeDtypeStruct((t, n), BF16),
            jax.ShapeDtypeStruct((t, V7X_LANES), F32),
        ],
        scratch_shapes=[pltpu.VMEM((tm, d), BF16)],
        compiler_params=_cparams("parallel", "arbitrary"),
        name="in_proj",
    )(x2d, norm_w, w_main, w_dt, ind_a, ind_at, ind_x, ind_xt, gains)


def _attn_kernel(q_ref, *refs, tq, nprev):
    k_refs = refs[:nprev + 1]
    v_refs = refs[nprev + 1:2 * nprev + 2]
    bias_ref, o_ref = refs[2 * nprev + 2:]
    qb = pl.program_id(1)
    nk = (nprev + 1) * tq
    lane = lax.broadcasted_iota(jnp.int32, (1, V7X_LANES), 1)
    col = lax.broadcasted_iota(jnp.int32, (1, nk), 1)
    before_start = col < (nprev - qb) * tq
    for hp in range(A_HEADS // 2):
        ls = slice(hp * V7X_LANES, (hp + 1) * V7X_LANES)
        q2 = q_ref[:, ls]
        kk = jnp.concatenate([r[:, ls] for r in k_refs], axis=0)
        vv = jnp.concatenate([r[:, ls] for r in v_refs], axis=0)
        outs = []
        for hh in range(2):
            sel = (lane < A_HEAD_DIM) if hh == 0 else (lane >= A_HEAD_DIM)
            qm = jnp.where(sel, q2, jnp.zeros_like(q2))
            s = lax.dot_general(qm, kk, (((1,), (1,)), ((), ())), preferred_element_type=F32)
            s = jnp.where(before_start, NEG, s + bias_ref[2 * hp + hh])
            m = jnp.max(s, axis=-1, keepdims=True)
            p = jnp.exp2(s - m)
            l = jnp.sum(p, axis=-1, keepdims=True)
            o = jnp.dot(p.astype(BF16), vv, preferred_element_type=F32)
            outs.append(o / l)
        o_ref[:, ls] = jnp.where(lane < A_HEAD_DIM, outs[0], outs[1]).astype(o_ref.dtype)


def _attention(proj3, bias, q_tile, k_tile, v_tile):
    b, s, _ = proj3.shape
    tq = ATTN_TQ
    width = A_HEADS * A_HEAD_DIM
    left = LEFT_CHUNKS * CHUNK
    assert left % tq == 0 and s % tq == 0
    nprev = left // tq

    def kv_spec(tile, back):
        return pl.BlockSpec((None, tq, width), lambda bi, qi: (bi, jnp.maximum(qi - back, 0), tile))

    in_specs = [pl.BlockSpec((None, tq, width), lambda bi, qi: (bi, qi, q_tile))]
    in_specs += [kv_spec(k_tile, nprev - i) for i in range(nprev + 1)]
    in_specs += [kv_spec(v_tile, nprev - i) for i in range(nprev + 1)]
    in_specs += [pl.BlockSpec(bias.shape, lambda bi, qi: (0, 0, 0), pipeline_mode=pl.Buffered(1))]
    return pl.pallas_call(
        functools.partial(_attn_kernel, tq=tq, nprev=nprev),
        grid=(b, s // tq),
        in_specs=in_specs,
        out_specs=pl.BlockSpec((None, tq, width), lambda bi, qi: (bi, qi, 0)),
        out_shape=jax.ShapeDtypeStruct((b, s, width), BF16),
        compiler_params=_cparams("parallel", "parallel"),
        name="chunk_attn",
    )(proj3, *([proj3] * (2 * nprev + 2)), bias)


def _attn_bias(rel_bias):
    tq = ATTN_TQ
    left = LEFT_CHUNKS * CHUNK
    i = jnp.arange(tq)[:, None]
    j = jnp.arange(left + tq)[None, :]
    m_len = tq + left + tq
    m = jnp.arange(m_len)
    diff = jnp.where(m < left + tq, m, m - m_len)
    v = rel_bias[:, jnp.clip(left - diff, -REL_CLIP, REL_CLIP) + REL_CLIP].astype(F32)
    h = v.shape[0]
    bias = jnp.tile(v, (1, tq))[:, :tq * (m_len - 1)].reshape(h, tq, m_len - 1)[:, :, :left + tq]
    qc, kc = i // CHUNK, j // CHUNK
    in_band = (kc >= qc) & (kc <= qc + LEFT_CHUNKS)
    return jnp.where(in_band[None], bias * LOG2_E, NEG)


def _ssd_kernel(z_ref, x_ref, bc_ref, dt_ref, cwx_ref, cbx_ref, cwbc_ref, cbbc_ref, dtb_ref,
                aneg_ref, dskip_ref, gain_ref, e_ref, o_ref,
                xf_ref, bcf_ref, st_ref, xs_ref, y_ref, *, L):
    c = pl.program_id(1)
    inner = SSM_HEADS * SSM_HEAD_DIM
    gw = inner // SSM_GROUPS
    gs = SSM_GROUPS * SSM_STATE
    tail = V7X_SUBLANES

    @pl.when(c == 0)
    def _():
        xf_ref[0:tail, :] = jnp.zeros((tail, inner), F32)
        bcf_ref[0:tail, :] = jnp.zeros((tail, 2 * gs), F32)
        st_ref[...] = jnp.zeros_like(st_ref)

    xf_ref[tail:, :] = x_ref[...].astype(F32)
    bcf_ref[tail:, :] = bc_ref[...].astype(F32)

    def conv_silu(src_ref, w_ref, b_ref, c0, c1):
        acc = b_ref[:, c0:c1] + w_ref[CONV_WIDTH - 1:CONV_WIDTH, c0:c1] * src_ref[tail:tail + L, c0:c1]
        for k in range(1, CONV_WIDTH):
            acc = acc + (w_ref[CONV_WIDTH - 1 - k:CONV_WIDTH - k, c0:c1]
                         * src_ref[tail - k:tail - k + L, c0:c1])
        return acc * jax.nn.sigmoid(acc)

    for g in range(SSM_GROUPS):
        xs_ref[:, g * gw:(g + 1) * gw] = conv_silu(xf_ref, cwx_ref, cbx_ref, g * gw, (g + 1) * gw)
    bmat = conv_silu(bcf_ref, cwbc_ref, cbbc_ref, 0, gs)
    cmat = conv_silu(bcf_ref, cwbc_ref, cbbc_ref, gs, 2 * gs)
    xf_ref[0:tail, :] = xf_ref[L:L + tail, :]
    bcf_ref[0:tail, :] = bcf_ref[L:L + tail, :]

    pre = dt_ref[...] + dtb_ref[...]
    dt = jnp.maximum(pre, 0.0) + jnp.log1p(jnp.exp(-jnp.abs(pre)))
    a = dt * aneg_ref[...]
    row = lax.broadcasted_iota(jnp.int32, (L, L), 0)
    colm = lax.broadcasted_iota(jnp.int32, (L, L), 1)
    lower = colm <= row
    tri = jnp.where(lower, 1.0, 0.0).astype(F32)
    cs = jnp.dot(tri, a, precision=HIGHEST, preferred_element_type=F32)
    cs_t = cs.T
    dt_t = dt.T
    cs_last = cs[L - 1:L, :]
    w_state = dt * jnp.exp(cs_last - cs)
    e_cs = jnp.exp(cs)
    chunk_decay = jnp.broadcast_to(jnp.exp(cs_last), (tail, V7X_LANES))
    stacked = jnp.concatenate([w_state, e_cs, chunk_decay], axis=0)
    s_hi, s_lo = _split_bf16(stacked)
    expanded = (jnp.dot(s_hi, e_ref[...], preferred_element_type=F32)
                + jnp.dot(s_lo, e_ref[...], preferred_element_type=F32))
    w_state_e = expanded[0:L]
    e_cs_e = expanded[L:2 * L]
    decay_e = expanded[2 * L:2 * L + 1]

    lane = lax.broadcasted_iota(jnp.int32, (1, V7X_LANES), 1)
    pairs_per_group = gw // V7X_LANES
    for g in range(SSM_GROUPS):
        bg = bmat[:, g * SSM_STATE:(g + 1) * SSM_STATE]
        cg = cmat[:, g * SSM_STATE:(g + 1) * SSM_STATE].astype(BF16)
        cb = lax.dot_general(cg, bg.astype(BF16), (((1,), (1,)), ((), ())),
                             preferred_element_type=F32)
        state_b = st_ref[g].astype(BF16)
        y_off = jnp.dot(cg, state_b, preferred_element_type=F32) * e_cs_e[:, g * gw:(g + 1) * gw]
        for pr in range(pairs_per_group):
            c0 = g * gw + pr * V7X_LANES
            xp = xs_ref[:, c0:c0 + V7X_LANES]
            xpb = xp.astype(BF16)
            acc = y_off[:, pr * V7X_LANES:(pr + 1) * V7X_LANES] + dskip_ref[:, c0:c0 + V7X_LANES] * xp
            for hh in range(2):
                h = c0 // SSM_HEAD_DIM + hh
                d = cs[:, h:h + 1] - cs_t[h:h + 1, :]
                m = cb * jnp.exp(jnp.where(lower, d, NEG)) * dt_t[h:h + 1, :]
                sel = (lane < SSM_HEAD_DIM) if hh == 0 else (lane >= SSM_HEAD_DIM)
                xm = jnp.where(sel, xpb, jnp.zeros_like(xpb))
                acc = acc + jnp.dot(m.astype(BF16), xm, preferred_element_type=F32)
            y_ref[:, c0:c0 + V7X_LANES] = acc
        xw = (xs_ref[:, g * gw:(g + 1) * gw] * w_state_e[:, g * gw:(g + 1) * gw]).astype(BF16)
        new = jnp.dot(bg.T.astype(BF16), xw, preferred_element_type=F32)
        st_ref[g] = st_ref[g] * decay_e[:, g * gw:(g + 1) * gw] + new

    for g in range(SSM_GROUPS):
        sl = slice(g * gw, (g + 1) * gw)
        zz = z_ref[:, sl].astype(F32)
        yz = y_ref[:, sl] * (zz * jax.nn.sigmoid(zz))
        ms = jnp.mean(yz * yz, axis=-1, keepdims=True)
        o_ref[:, sl] = (yz * lax.rsqrt(ms + EPS) * gain_ref[:, sl]).astype(o_ref.dtype)


def _ssd(proj3, dt3, cwx, cbx, cwbc, cbbc, dtb, aneg, dskip_e, gain, e_mat, z_blk, x_blk, bc_blk):
    b, s, _ = proj3.shape
    L = SSD_L
    assert s % L == 0
    inner = SSM_HEADS * SSM_HEAD_DIM
    gs2 = 2 * SSM_GROUPS * SSM_STATE
    const = lambda bi, ci: (0, 0)
    full = lambda a: pl.BlockSpec(a.shape, const)
    return pl.pallas_call(
        functools.partial(_ssd_kernel, L=L),
        grid=(b, s // L),
        in_specs=[
            pl.BlockSpec((None, L, inner), lambda bi, ci: (bi, ci, z_blk)),
            pl.BlockSpec((None, L, inner), lambda bi, ci: (bi, ci, x_blk)),
            pl.BlockSpec((None, L, gs2), lambda bi, ci: (bi, ci, bc_blk)),
            pl.BlockSpec((None, L, V7X_LANES), lambda bi, ci: (bi, ci, 0)),
            full(cwx), full(cbx), full(cwbc), full(cbbc), full(dtb), full(aneg), full(dskip_e),
            full(gain), full(e_mat),
        ],
        out_specs=pl.BlockSpec((None, L, inner), lambda bi, ci: (bi, ci, 0)),
        out_shape=jax.ShapeDtypeStruct((b, s, inner), BF16),
        scratch_shapes=[
            pltpu.VMEM((L + V7X_SUBLANES, inner), F32),
            pltpu.VMEM((L + V7X_SUBLANES, gs2), F32),
            pltpu.VMEM((SSM_GROUPS, SSM_STATE, inner // SSM_GROUPS), F32),
            pltpu.VMEM((L, inner), F32),
            pltpu.VMEM((L, inner), F32),
        ],
        compiler_params=_cparams("parallel", "arbitrary"),
        name="ssd",
    )(proj3, proj3, proj3, dt3, cwx, cbx, cwbc, cbbc, dtb, aneg, dskip_e, gain, e_mat)


def _mem_kv_kernel(mem_ref, g_ref, w_ref, kg_ref, k_ref, v_ref):
    m = mem_ref[...]
    ms = jnp.mean(m * m, axis=-1, keepdims=True)
    mn = (m * lax.rsqrt(ms + EPS) * g_ref[...]).astype(BF16)
    kv = jnp.dot(mn, w_ref[...], preferred_element_type=F32)
    width = X_HEADS * X_HEAD_DIM
    for h in range(X_HEADS):
        sl = slice(h * X_HEAD_DIM, (h + 1) * X_HEAD_DIM)
        kh = kv[:, sl]
        r = lax.rsqrt(jnp.mean(kh * kh, axis=-1, keepdims=True) + EPS)
        k_ref[:, sl] = (kh * r * kg_ref[...]).astype(k_ref.dtype)
    v_ref[...] = kv[:, width:].astype(v_ref.dtype)


def _mem_kv(mem, norm_mem, w_kv, k_gain):
    b, m, d = mem.shape
    width = X_HEADS * X_HEAD_DIM
    const = lambda bi: (0, 0)
    return pl.pallas_call(
        _mem_kv_kernel,
        grid=(b,),
        in_specs=[
            pl.BlockSpec((None, m, d), lambda bi: (bi, 0, 0)),
            pl.BlockSpec((1, d), const),
            pl.BlockSpec((d, 2 * width), const),
            pl.BlockSpec((1, X_HEAD_DIM), const),
        ],
        out_specs=[pl.BlockSpec((None, m, width), lambda bi: (bi, 0, 0))] * 2,
        out_shape=[jax.ShapeDtypeStruct((b, m, width), BF16)] * 2,
        compiler_params=_cparams("parallel"),
        name="mem_kv",
    )(mem, norm_mem, w_kv, k_gain)


def _mem_attn_kernel(q_ref, k_ref, v_ref, o_ref):
    for h in range(X_HEADS):
        sl = slice(h * X_HEAD_DIM, (h + 1) * X_HEAD_DIM)
        s = lax.dot_general(q_ref[:, sl], k_ref[:, sl], (((1,), (1,)), ((), ())),
                            preferred_element_type=F32)
        m = jnp.max(s, axis=-1, keepdims=True)
        p = jnp.exp(s - m)
        l = jnp.sum(p, axis=-1, keepdims=True)
        o = jnp.dot(p.astype(BF16), v_ref[:, sl], preferred_element_type=F32)
        o_ref[:, sl] = (o / l).astype(o_ref.dtype)


def _mem_attn(proj3, k, v, q_blk, tq):
    b, s, _ = proj3.shape
    m = k.shape[1]
    width = X_HEADS * X_HEAD_DIM
    assert s % tq == 0
    return pl.pallas_call(
        _mem_attn_kernel,
        grid=(b, s // tq),
        in_specs=[
            pl.BlockSpec((None, tq, width), lambda bi, qi: (bi, qi, q_blk)),
            pl.BlockSpec((None, m, width), lambda bi, qi: (bi, 0, 0)),
            pl.BlockSpec((None, m, width), lambda bi, qi: (bi, 0, 0)),
        ],
        out_specs=pl.BlockSpec((None, tq, width), lambda bi, qi: (bi, qi, 0)),
        out_shape=jax.ShapeDtypeStruct((b, s, width), BF16),
        compiler_params=_cparams("parallel", "parallel"),
        name="mem_attn",
    )(proj3, k, v)


def _merge_kernel(x_ref, ya_ref, yb_ref, yc_ref, g0_ref, g1_ref, g2_ref, wa_ref, wb_ref, wc_ref,
                  wo_ref, nf_ref, wrh_ref, wrl_ref, br_ref,
                  x1_ref, h2_ref, route_ref, cnt_ref, *, tm, rt):
    merged = (g0_ref[...].astype(F32) * jnp.dot(ya_ref[...], wa_ref[...], preferred_element_type=F32)
              + g1_ref[...].astype(F32) * jnp.dot(yb_ref[...], wb_ref[...], preferred_element_type=F32)
              + g2_ref[...].astype(F32) * jnp.dot(yc_ref[...], wc_ref[...], preferred_element_type=F32))
    x1 = x_ref[...] + jnp.dot(merged.astype(BF16), wo_ref[...], preferred_element_type=F32)
    x1_ref[...] = x1
    ms = jnp.mean(x1 * x1, axis=-1, keepdims=True)
    h2 = x1 * lax.rsqrt(ms + EPS) * nf_ref[...]
    h2_ref[...] = h2.astype(h2_ref.dtype)

    h_hi, h_lo = _split_bf16(h2)
    logits_all = (jnp.dot(h_hi, wrh_ref[...], preferred_element_type=F32)
                  + jnp.dot(h_lo, wrh_ref[...], preferred_element_type=F32)
                  + jnp.dot(h_hi, wrl_ref[...], preferred_element_type=F32)) + br_ref[...]
    for sub in range(tm // rt):
        _route_tile(logits_all[sub * rt:(sub + 1) * rt], route_ref.at[pl.ds(sub * rt, rt)],
                    cnt_ref.at[sub], rt)


def _route_tile(logits, route_ref, cnt_ref, tm):
    lane = lax.broadcasted_iota(jnp.int32, (tm, V7X_LANES), 1)
    lane_f = lane.astype(F32)
    work = jnp.where(lane < N_EXPERTS, logits, NEG)
    sel_val, sel_oh = [], []
    for _ in range(TOP_K):
        mval = jnp.max(work, axis=-1, keepdims=True)
        ik = jnp.min(jnp.where(work == mval, lane_f, float(V7X_LANES)), axis=-1, keepdims=True)
        oh = lane_f == ik
        work = jnp.where(oh, NEG, work)
        sel_val.append(mval)
        sel_oh.append(oh)
    ex = [jnp.exp(v - sel_val[0]) for v in sel_val]
    denom = ex[0] + ex[1] + ex[2] + ex[3]

    oh_all = jnp.zeros((tm, V7X_LANES), F32)
    for oh in sel_oh:
        oh_all = oh_all + jnp.where(oh, 1.0, 0.0)
    row = lax.broadcasted_iota(jnp.int32, (tm, tm), 0)
    colm = lax.broadcasted_iota(jnp.int32, (tm, tm), 1)
    strict = jnp.where(colm < row, 1.0, 0.0).astype(BF16)
    before = jnp.dot(strict, oh_all.astype(BF16), preferred_element_type=F32)
    cnt = jnp.sum(oh_all, axis=0, keepdims=True)
    cnt8 = jnp.floor((cnt + (V7X_SUBLANES - 1.0)) * (1.0 / V7X_SUBLANES)) * V7X_SUBLANES
    cnt8 = jnp.broadcast_to(cnt8, (V7X_SUBLANES, V7X_LANES))
    er = lax.broadcasted_iota(jnp.int32, (V7X_LANES, V7X_LANES), 0)
    ec = lax.broadcasted_iota(jnp.int32, (V7X_LANES, V7X_LANES), 1)
    earlier = jnp.where(er < ec, 1.0, 0.0).astype(BF16)
    run_start = jnp.dot(cnt8.astype(BF16), earlier, preferred_element_type=F32)[0:1, :]
    slot = before + run_start
    route = jnp.zeros((tm, V7X_LANES), F32)
    for k in range(TOP_K):
        pos = jnp.sum(jnp.where(sel_oh[k], slot, 0.0), axis=-1, keepdims=True)
        route = jnp.where(lane == k, pos, route)
        route = jnp.where(lane == TOP_K + k, ex[k] / denom, route)
    route_ref[...] = route
    cnt_ref[...] = cnt8


def _merge(x2d, ya, yb, yc, proj, wa, wb, wc, wo, nf, wr, br, gate_blk, tm, rt):
    t, d = x2d.shape
    assert t % tm == 0 and tm % rt == 0
    wr_hi, wr_lo = _split_bf16(wr)
    const = lambda i: (0, 0)
    full = lambda a: pl.BlockSpec(a.shape, const)
    rows = lambda w: pl.BlockSpec((tm, w), lambda i: (i, 0))
    return pl.pallas_call(
        functools.partial(_merge_kernel, tm=tm, rt=rt),
        grid=(t // tm,),
        in_specs=[
            rows(d), rows(ya.shape[1]), rows(yb.shape[1]), rows(yc.shape[1]),
            pl.BlockSpec((tm, d), lambda i: (i, gate_blk)),
            pl.BlockSpec((tm, d), lambda i: (i, gate_blk + 1)),
            pl.BlockSpec((tm, d), lambda i: (i, gate_blk + 2)),
            full(wa), full(wb), full(wc), full(wo), full(nf), full(wr_hi), full(wr_lo), full(br),
        ],
        out_specs=[rows(d), rows(d), rows(V7X_LANES),
                   pl.BlockSpec((tm // rt, V7X_SUBLANES, V7X_LANES), lambda i: (i, 0, 0))],
        out_shape=[
            jax.ShapeDtypeStruct((t, d), F32),
            jax.ShapeDtypeStruct((t, d), BF16),
            jax.ShapeDtypeStruct((t, V7X_LANES), F32),
            jax.ShapeDtypeStruct((t // rt, V7X_SUBLANES, V7X_LANES), F32),
        ],
        compiler_params=_cparams("parallel"),
        name="merge_route",
    )(x2d, ya, yb, yc, proj, proj, proj, wa, wb, wc, wo, nf, wr_hi, wr_lo, br)


RUN = V7X_SUBLANES
BLOCK_PIECES = MOE_ROWS // RUN
TAIL_SIZES = (32, 16, 8, 4, 2, 1)
assert MOE_TILE * TOP_K % V7X_LANES == 0 and N_EXPERTS * (RUN - 1) < 2 * TAIL_SIZES[0] * RUN


def _binary_pieces(n, body):
    for size in TAIL_SIZES:
        @pl.when((n & size) != 0)
        def _(size=size):
            body(n & ~(2 * size - 1), size)


def _dispatch_kernel(h_ref, route_ref, xs_ref, *, tm, ns):
    pos_t = route_ref[...].T
    q = lax.broadcasted_iota(jnp.int32, (ns, tm), 0).astype(F32)
    perm = jnp.zeros((ns, tm), F32)
    for k in range(TOP_K):
        perm = perm + jnp.where(q == pos_t[k:k + 1, :], 1.0, 0.0)
    xs_ref[...] = jnp.dot(perm.astype(BF16), h_ref[...], preferred_element_type=F32)


def _dispatch(h2, route, tm, ns):
    t, d = h2.shape
    return pl.pallas_call(
        functools.partial(_dispatch_kernel, tm=tm, ns=ns),
        grid=(t // tm,),
        in_specs=[
            pl.BlockSpec((tm, d), lambda i: (i, 0)),
            pl.BlockSpec((tm, V7X_LANES), lambda i: (i, 0)),
        ],
        out_specs=pl.BlockSpec((ns, d), lambda i: (i, 0)),
        out_shape=jax.ShapeDtypeStruct((t // tm * ns, d), F32),
        compiler_params=_cparams("parallel"),
        name="moe_dispatch",
    )(h2, route)


def _experts_kernel(bexp_ref, nused_ref, tail_ref, tab_ref, next_tab_ref, xs_ref, wg_ref, bg_ref,
                    wu_ref, bu_ref, wd_ref, bd_ref, yb_ref,
                    wgb_ref, wub_ref, wdb_ref, xbuf_ref, ybuf_ref, zero_ref, gsem, ssem, zsem, *,
                    ns, n_tiles):
    b = pl.program_id(0)
    n_used = nused_ref[0]
    slot = lax.rem(b, 2)

    def gather(t_ref, sl, j):
        src = pl.ds(pl.multiple_of(t_ref[0, j] * RUN, RUN), RUN)
        return pltpu.make_async_copy(xs_ref.at[src], xbuf_ref.at[sl, pl.ds(j * RUN, RUN)], gsem.at[sl])

    def scatter(t_ref, sl, j):
        dst = pl.ds(pl.multiple_of(t_ref[0, BLOCK_PIECES + j] * RUN, RUN), RUN)
        return pltpu.make_async_copy(ybuf_ref.at[sl, pl.ds(j * RUN, RUN)], yb_ref.at[dst], ssem.at[sl])

    def zero_copy(rows):
        return pltpu.make_async_copy(zero_ref.at[pl.ds(0, rows.size)], yb_ref.at[rows], zsem)

    def zero_unwritten(fn):
        for half in range(2):
            fn(zero_copy(pl.ds(n_tiles * ns + half * MOE_ROWS, MOE_ROWS)))

        def per_tile(i, carry):
            n = tail_ref[i]
            first = (i + 1) * ns - n * RUN
            _binary_pieces(n, lambda off, size: fn(zero_copy(
                pl.ds(pl.multiple_of(first + off * RUN, RUN), size * RUN))))
            return carry
        lax.fori_loop(0, n_tiles, per_tile, 0)

    @pl.when(b == 0)
    def _():
        zero_ref[...] = jnp.zeros_like(zero_ref)
        zero_unwritten(lambda cp: cp.start())
        zero_unwritten(lambda cp: cp.wait())
        for j in range(BLOCK_PIECES):
            gather(tab_ref, 0, j).start(priority=j % 2)

    @pl.when(b + 1 < n_used)
    def _():
        for j in range(BLOCK_PIECES):
            gather(next_tab_ref, 1 - slot, j).start(priority=j % 2)

    new_expert = jnp.logical_or(b == 0, bexp_ref[b] != bexp_ref[jnp.maximum(b - 1, 0)])

    @pl.when(new_expert)
    def _():
        wgb_ref[...] = wg_ref[...].astype(BF16)
        wub_ref[...] = wu_ref[...].astype(BF16)
        wdb_ref[...] = wd_ref[...].astype(BF16)

    @pl.when(b < n_used)
    def _():
        for j in range(BLOCK_PIECES):
            gather(tab_ref, slot, j).wait()

        @pl.when(b >= 2)
        def _():
            for j in range(BLOCK_PIECES):
                scatter(tab_ref, slot, j).wait()

        xb = xbuf_ref[slot].astype(BF16)
        g = jnp.dot(xb, wgb_ref[...], preferred_element_type=F32) + bg_ref[...]
        u = jnp.dot(xb, wub_ref[...], preferred_element_type=F32) + bu_ref[...]
        g = jnp.minimum(g, SWIGLU_LIMIT)
        u = jnp.clip(u, -SWIGLU_LIMIT, SWIGLU_LIMIT)
        act = (u + 1.0) * g * jax.nn.sigmoid(SWIGLU_ALPHA * g)
        ybuf_ref[slot] = (jnp.dot(act.astype(BF16), wdb_ref[...], preferred_element_type=F32)
                          + bd_ref[...])
        for j in range(BLOCK_PIECES):
            scatter(tab_ref, slot, j).start(priority=j % 2)

    @pl.when(b == n_used - 1)
    def _():
        for j in range(BLOCK_PIECES):
            scatter(tab_ref, slot, j).wait()

        @pl.when(b >= 1)
        def _():
            for j in range(BLOCK_PIECES):
                scatter(tab_ref, 1 - slot, j).wait()


def _experts(block_exp, n_used, tail, tab, xs, wg, bg, wu, bu, wd, bd, ns):
    d, de = wg.shape[1], wg.shape[2]
    n_blocks = tab.shape[0]
    n_tiles = xs.shape[0] // ns
    assert tab.shape[2] == 2 * BLOCK_PIECES == V7X_LANES
    wmap = lambda b, be, nu, tl: (be[b], 0, 0)
    smem_tab = lambda f: pl.BlockSpec((None, 1, V7X_LANES), f, memory_space=pltpu.SMEM)
    any_spec = pl.BlockSpec(memory_space=pl.ANY)
    return pl.pallas_call(
        functools.partial(_experts_kernel, ns=ns, n_tiles=n_tiles),
        grid_spec=pltpu.PrefetchScalarGridSpec(
            num_scalar_prefetch=3,
            grid=(n_blocks,),
            in_specs=[
                smem_tab(lambda b, be, nu, tl: (b, 0, 0)),
                smem_tab(lambda b, be, nu, tl: (jnp.minimum(b + 1, n_blocks - 1), 0, 0)),
                any_spec,
                pl.BlockSpec((None, d, de), wmap), pl.BlockSpec((None, 1, de), wmap),
                pl.BlockSpec((None, d, de), wmap), pl.BlockSpec((None, 1, de), wmap),
                pl.BlockSpec((None, de, d), wmap), pl.BlockSpec((None, 1, d), wmap),
            ],
            out_specs=any_spec,
            scratch_shapes=[
                pltpu.VMEM((d, de), BF16), pltpu.VMEM((d, de), BF16), pltpu.VMEM((de, d), BF16),
                pltpu.VMEM((2, MOE_ROWS, d), F32), pltpu.VMEM((2, MOE_ROWS, d), F32),
                pltpu.VMEM((MOE_ROWS, d), F32),
                pltpu.SemaphoreType.DMA((2,)), pltpu.SemaphoreType.DMA((2,)), pltpu.SemaphoreType.DMA,
            ],
        ),
        out_shape=jax.ShapeDtypeStruct((n_tiles * ns + 2 * MOE_ROWS, d), F32),
        compiler_params=_cparams("arbitrary"),
        name="moe_experts",
    )(block_exp, n_used, tail, tab, tab, xs, wg, bg, wu, bu, wd, bd)


def _combine_kernel(extra_ref, route_ref, x1_ref, yb_ref, o_ref, sorted_ref, sems, *, tm, ns):
    i = pl.program_id(0)
    slot = lax.rem(i, 2)
    base_rows = tm * TOP_K

    def copies(tile, sl, fn):
        def rows_copy(first, n_rows):
            src = pl.ds(pl.multiple_of(tile * ns + first, RUN), n_rows)
            dst = pl.ds(pl.multiple_of(first, RUN), n_rows)
            return pltpu.make_async_copy(yb_ref.at[src], sorted_ref.at[sl, dst], sems.at[sl])

        fn(rows_copy(0, base_rows))
        _binary_pieces(extra_ref[tile],
                       lambda off, size: fn(rows_copy(base_rows + off * RUN, size * RUN)))

    @pl.when(i == 0)
    def _():
        sorted_ref[...] = jnp.zeros_like(sorted_ref)
        copies(i, slot, lambda cp: cp.start())

    @pl.when(i + 1 < pl.num_programs(0))
    def _():
        copies(i + 1, 1 - slot, lambda cp: cp.start())

    copies(i, slot, lambda cp: cp.wait())

    ys = sorted_ref[slot].astype(BF16)
    route = route_ref[...]
    q = lax.broadcasted_iota(jnp.int32, (tm, ns), 1).astype(F32)
    wmat = jnp.zeros((tm, ns), F32)
    for k in range(TOP_K):
        wmat = wmat + jnp.where(q == route[:, k:k + 1], route[:, TOP_K + k:TOP_K + k + 1], 0.0)
    o_ref[...] = x1_ref[...] + jnp.dot(wmat.astype(BF16), ys, preferred_element_type=F32)


def _combine(extra, route, x1, yb, tm, ns):
    t, d = x1.shape
    return pl.pallas_call(
        functools.partial(_combine_kernel, tm=tm, ns=ns),
        grid=(t // tm,),
        in_specs=[
            pl.BlockSpec(memory_space=pltpu.SMEM),
            pl.BlockSpec((tm, V7X_LANES), lambda i: (i, 0)),
            pl.BlockSpec((tm, d), lambda i: (i, 0)),
            pl.BlockSpec(memory_space=pl.ANY),
        ],
        out_specs=pl.BlockSpec((tm, d), lambda i: (i, 0)),
        out_shape=jax.ShapeDtypeStruct((t, d), F32),
        scratch_shapes=[pltpu.VMEM((2, ns, d), F32), pltpu.SemaphoreType.DMA((2,))],
        compiler_params=_cparams("arbitrary"),
        name="moe_combine",
    )(extra, route, x1, yb)


def _head_indicator(width, head_dim):
    lane_head = jnp.arange(width) // head_dim
    return (lane_head[:, None] == jnp.arange(V7X_LANES)[None, :]).astype(BF16)


def _layer(x, mem, norm_mix, w_in, a_q_gain, a_k_gain, a_rel_bias, conv_w, conv_b, dt_bias, a_log,
           d_skip, ssm_norm, norm_mem, w_mem_kv, x_q_gain, x_k_gain, w_br_a, w_br_b, w_br_c, w_out,
           norm_ffn, w_router, b_router, w_gate, b_gate, w_up, b_up, w_down, b_down):
    b, s, d = x.shape
    t = b * s
    a_width = A_HEADS * A_HEAD_DIM
    inner = SSM_HEADS * SSM_HEAD_DIM
    gs = SSM_GROUPS * SSM_STATE
    x_width = X_HEADS * X_HEAD_DIM
    assert d == a_width == x_width and inner == 2 * d and 2 * gs == d

    o_q, o_k, o_v = 0, a_width, 2 * a_width
    o_z = 3 * a_width
    o_xbc = o_z + inner
    o_dt = o_xbc + inner + 2 * gs
    o_qx = o_dt + SSM_HEADS
    o_gate = o_qx + x_width
    cols = lambda o, w: w_in[:, o:o + w]
    w_main = jnp.concatenate([
        cols(o_z, inner), cols(o_xbc, inner), cols(o_q, a_width), cols(o_k, a_width),
        cols(o_v, a_width), cols(o_xbc + inner, 2 * gs), cols(o_qx, x_width), cols(o_gate, 3 * d),
    ], axis=1).astype(BF16)
    roles = ("plain", "plain", "plain", "plain", "qa", "ka", "plain", "plain", "qx", "sig", "sig", "sig")
    w_dt = jnp.pad(cols(o_dt, SSM_HEADS), ((0, 0), (0, V7X_LANES - SSM_HEADS))).astype(BF16)
    ind_a = _head_indicator(a_width, A_HEAD_DIM)
    ind_x = _head_indicator(x_width, X_HEAD_DIM)
    gains = jnp.zeros((V7X_SUBLANES, d), F32)
    gains = gains.at[0].set(jnp.tile(a_q_gain, A_HEADS) * (A_HEAD_DIM ** -0.5 * LOG2_E))
    gains = gains.at[1].set(jnp.tile(a_k_gain, A_HEADS))
    gains = gains.at[2].set(jnp.tile(x_q_gain, X_HEADS) * X_HEAD_DIM ** -0.5)

    x2d = x.reshape(t, d)
    proj, dt_raw = _in_proj(x2d, norm_mix.reshape(1, d), w_main, w_dt, ind_a, ind_a.T, ind_x, ind_x.T,
                            gains, roles, tm=min(1024, t), tn=d)
    proj3 = proj.reshape(b, s, proj.shape[1])

    y_a = _attention(proj3, _attn_bias(a_rel_bias), q_tile=4, k_tile=5, v_tile=6)

    pad_h = lambda v: jnp.pad(v.astype(F32), (0, V7X_LANES - SSM_HEADS)).reshape(1, V7X_LANES)
    e_mat = _head_indicator(inner, SSM_HEAD_DIM).T
    y_b = _ssd(proj3, dt_raw.reshape(b, s, V7X_LANES),
               conv_w[:, :inner], conv_b[:inner].reshape(1, inner),
               conv_w[:, inner:], conv_b[inner:].reshape(1, 2 * gs),
               pad_h(dt_bias), pad_h(-jnp.exp(a_log.astype(F32))),
               jnp.repeat(d_skip.astype(F32), SSM_HEAD_DIM).reshape(1, inner),
               ssm_norm.reshape(1, inner), e_mat, z_blk=0, x_blk=1, bc_blk=7)

    k_mem, v_mem = _mem_kv(mem, norm_mem.reshape(1, d), w_mem_kv.astype(BF16),
                           x_k_gain.reshape(1, X_HEAD_DIM))
    y_c = _mem_attn(proj3, k_mem, v_mem, q_blk=8, tq=min(512, s))

    w_r = jnp.pad(w_router, ((0, 0), (0, V7X_LANES - N_EXPERTS)))
    b_r = jnp.pad(b_router, (0, V7X_LANES - N_EXPERTS)).reshape(1, V7X_LANES)
    tm_moe = min(MOE_TILE, t)
    n_tiles = t // tm_moe
    x1, h2, route, tile_cnt = _merge(
        x2d, y_a.reshape(t, a_width), y_b.reshape(t, inner), y_c.reshape(t, x_width), proj,
        w_br_a.astype(BF16), w_br_b.astype(BF16), w_br_c.astype(BF16), w_out.astype(BF16),
        norm_ffn.reshape(1, d), w_r, b_r, gate_blk=9, tm=min(MERGE_TM, t), rt=tm_moe)

    ns = -(-(tm_moe * TOP_K + N_EXPERTS * (RUN - 1)) // V7X_LANES) * V7X_LANES
    units = ns // RUN
    n_blocks = -(-(t * TOP_K + n_tiles * N_EXPERTS * (RUN - 1) + N_EXPERTS * (MOE_ROWS - 1))
                 // MOE_ROWS)
    n8 = tile_cnt[:, 0, :N_EXPERTS].astype(jnp.int32) // RUN
    total = jnp.sum(n8, axis=0)
    padded = (total + BLOCK_PIECES - 1) // BLOCK_PIECES * BLOCK_PIECES
    pad_ends = jnp.cumsum(padded)
    pad_starts = pad_ends - padded
    n_used = (pad_ends[-1] // BLOCK_PIECES).reshape(1).astype(jnp.int32)
    blk = jnp.minimum(jnp.arange(n_blocks, dtype=jnp.int32), n_used[0] - 1)
    block_exp = jnp.minimum(jnp.sum(pad_ends[None, :] <= (blk * BLOCK_PIECES)[:, None], axis=1),
                            N_EXPERTS - 1).astype(jnp.int32)
    slot_j = jnp.arange(BLOCK_PIECES, dtype=jnp.int32)[None, :]
    onehot_e = (block_exp[:, None] == jnp.arange(N_EXPERTS)[None, :]).astype(jnp.int32)
    q = blk[:, None] * BLOCK_PIECES + slot_j - (onehot_e @ pad_starts)[:, None]
    real = q < (onehot_e @ total)[:, None]
    ends_b = onehot_e @ jnp.cumsum(n8, axis=0).T
    tile_of = jnp.minimum(jnp.sum(ends_b[:, None, :] <= q[:, :, None], axis=-1), n_tiles - 1)
    tile_1h = (tile_of[:, :, None] == jnp.arange(n_tiles)[None, None, :]).astype(jnp.int32)
    starts_b = ends_b - onehot_e @ n8.T
    in_tile_b = onehot_e @ (jnp.cumsum(n8, axis=1) - n8).T
    piece = (tile_of * units + jnp.sum(tile_1h * (in_tile_b - starts_b)[:, None, :], axis=-1) + q)
    zero_piece = (tm_moe * TOP_K + N_EXPERTS * (RUN - 1)) // RUN
    spare = n_tiles * units + (blk % 2)[:, None] * BLOCK_PIECES + slot_j
    tab = jnp.concatenate([jnp.where(real, piece, zero_piece), jnp.where(real, piece, spare)], axis=1)
    tab = tab.reshape(n_blocks, 1, 2 * BLOCK_PIECES).astype(jnp.int32)
    used = jnp.sum(n8, axis=1)
    tail = (units - used).astype(jnp.int32)
    extra = (used - tm_moe * TOP_K // RUN).astype(jnp.int32)

    xs = _dispatch(h2, route, tm_moe, ns)
    yb = _experts(block_exp, n_used, tail, tab, xs,
                  w_gate, b_gate.reshape(N_EXPERTS, 1, -1),
                  w_up, b_up.reshape(N_EXPERTS, 1, -1),
                  w_down, b_down.reshape(N_EXPERTS, 1, -1), ns)
    out = _combine(extra, route, x1, yb, tm_moe, ns)
    return out.reshape(b, s, d)


def kernel(x, mem, norm_mix, w_in, a_q_gain, a_k_gain, a_rel_bias, conv_w, conv_b, dt_bias, a_log, d_skip, ssm_norm, norm_mem, w_mem_kv, x_q_gain, x_k_gain, w_br_a, w_br_b, w_br_c, w_out, norm_ffn, w_router, b_router, w_gate, b_gate, w_up, b_up, w_down, b_down):
    for l in range(norm_mix.shape[0]):
        x = _layer(x, mem, norm_mix[l], w_in[l], a_q_gain[l], a_k_gain[l], a_rel_bias[l], conv_w[l],
                   conv_b[l], dt_bias[l], a_log[l], d_skip[l], ssm_norm[l], norm_mem[l], w_mem_kv[l],
                   x_q_gain[l], x_k_gain[l], w_br_a[l], w_br_b[l], w_br_c[l], w_out[l], norm_ffn[l],
                   w_router[l], b_router[l], w_gate[l], b_gate[l], w_up[l], b_up[l], w_down[l],
                   b_down[l])
    return x
```

```python
import functools

import jax
import jax.numpy as jnp
from jax import lax
from jax.experimental import pallas as pl
from jax.experimental.pallas import tpu as pltpu

F32 = jnp.float32
BF16 = jnp.bfloat16
HIGHEST = lax.Precision.HIGHEST

V7X_LANES = 128
V7X_SUBLANES = 8
V7X_VMEM_LIMIT_BYTES = 56 * 1024 * 1024

EPS = 1e-6
LOG2_E = 1.4426950408889634
NEG = -1e30

CHUNK = 64
A_HEADS = 16
A_HEAD_DIM = 64
LEFT_CHUNKS = 8
REL_CLIP = 128
SSM_HEADS = 32
SSM_HEAD_DIM = 64
SSM_GROUPS = 4
SSM_STATE = 128
CONV_WIDTH = 4
X_HEADS = 4
X_HEAD_DIM = 256
N_EXPERTS = 32
TOP_K = 4
SWIGLU_LIMIT = 7.0
SWIGLU_ALPHA = 1.702

ATTN_TQ = 256
SSD_L = 256
MOE_ROWS = 512
MOE_TILE = 256
MERGE_TM = 512


def _cparams(*sem):
    return pltpu.CompilerParams(dimension_semantics=sem, vmem_limit_bytes=V7X_VMEM_LIMIT_BYTES)


def _split_bf16(v):
    hi = v.astype(BF16)
    lo = (v - hi.astype(F32)).astype(BF16)
    return hi, lo


def _in_proj_kernel(x_ref, nw_ref, w_ref, wdt_ref, ind_a_ref, ind_at_ref, ind_x_ref, ind_xt_ref,
                    gains_ref, o_ref, dt_ref, h_ref, *, roles):
    j = pl.program_id(1)

    @pl.when(j == 0)
    def _():
        x = x_ref[...]
        ms = jnp.mean(x * x, axis=-1, keepdims=True)
        hb = (x * lax.rsqrt(ms + EPS) * nw_ref[...]).astype(BF16)
        h_ref[...] = hb
        dt_ref[...] = jnp.dot(hb, wdt_ref[...], preferred_element_type=F32)

    def head_norm(acc, ind_ref, indt_ref, head_dim, gain_row):
        s = jnp.dot((acc * acc).astype(BF16), ind_ref[...], preferred_element_type=F32)
        r = lax.rsqrt(s * (1.0 / head_dim) + EPS)
        rexp = jnp.dot(r.astype(BF16), indt_ref[...], preferred_element_type=F32)
        return acc * rexp * gain_row

    def cond_for(role):
        c = None
        for jj, r in enumerate(roles):
            if r == role:
                c = (j == jj) if c is None else jnp.logical_or(c, j == jj)
        return c

    for role in sorted(set(roles)):
        @pl.when(cond_for(role))
        def _(role=role):
            acc = jnp.dot(h_ref[...], w_ref[...], preferred_element_type=F32)
            if role == "qa":
                out = head_norm(acc, ind_a_ref, ind_at_ref, A_HEAD_DIM, gains_ref[0:1, :])
            elif role == "ka":
                out = head_norm(acc, ind_a_ref, ind_at_ref, A_HEAD_DIM, gains_ref[1:2, :])
            elif role == "qx":
                out = head_norm(acc, ind_x_ref, ind_xt_ref, X_HEAD_DIM, gains_ref[2:3, :])
            elif role == "sig":
                out = 0.5 * jnp.tanh(0.5 * acc) + 0.5
            else:
                out = acc
            o_ref[...] = out.astype(o_ref.dtype)


def _in_proj(x2d, norm_w, w_main, w_dt, ind_a, ind_at, ind_x, ind_xt, gains, roles, tm, tn):
    t, d = x2d.shape
    n = w_main.shape[1]
    assert t % tm == 0 and n % tn == 0 and len(roles) == n // tn
    const = lambda i, j: (0, 0)
    return pl.pallas_call(
        functools.partial(_in_proj_kernel, roles=roles),
        grid=(t // tm, n // tn),
        in_specs=[
            pl.BlockSpec((tm, d), lambda i, j: (i, 0)),
            pl.BlockSpec((1, d), const),
            pl.BlockSpec((d, tn), lambda i, j: (0, j)),
            pl.BlockSpec((d, V7X_LANES), const),
            pl.BlockSpec(ind_a.shape, const),
            pl.BlockSpec(ind_at.shape, const),
            pl.BlockSpec(ind_x.shape, const),
            pl.BlockSpec(ind_xt.shape, const),
            pl.BlockSpec(gains.shape, const),
        ],
        out_specs=[
            pl.BlockSpec((tm, tn), lambda i, j: (i, j)),
            pl.BlockSpec((tm, V7X_LANES), lambda i, j: (i, 0)),
        ],
        out_shape=[
            jax.ShapeDtypeStruct((t, n), BF16),
            jax.ShapeDtypeStruct((t, V7X_LANES), F32),
        ],
        scratch_shapes=[pltpu.VMEM((tm, d), BF16)],
        compiler_params=_cparams("parallel", "arbitrary"),
        name="in_proj",
    )(x2d, norm_w, w_main, w_dt, ind_a, ind_at, ind_x, ind_xt, gains)


def _attn_kernel(q_ref, *refs, tq, nprev):
    k_refs = refs[:nprev + 1]
    v_refs = refs[nprev + 1:2 * nprev + 2]
    bias_ref, o_ref = refs[2 * nprev + 2:]
    qb = pl.program_id(1)
    nk = (nprev + 1) * tq
    lane = lax.broadcasted_iota(jnp.int32, (1, V7X_LANES), 1)
    col = lax.broadcasted_iota(jnp.int32, (1, nk), 1)
    before_start = col < (nprev - qb) * tq
    for hp in range(A_HEADS // 2):
        ls = slice(hp * V7X_LANES, (hp + 1) * V7X_LANES)
        q2 = q_ref[:, ls]
        kk = jnp.concatenate([r[:, ls] for r in k_refs], axis=0)
        vv = jnp.concatenate([r[:, ls] for r in v_refs], axis=0)
        outs = []
        for hh in range(2):
            sel = (lane < A_HEAD_DIM) if hh == 0 else (lane >= A_HEAD_DIM)
            qm = jnp.where(sel, q2, jnp.zeros_like(q2))
            s = lax.dot_general(qm, kk, (((1,), (1,)), ((), ())), preferred_element_type=F32)
            s = jnp.where(before_start, NEG, s + bias_ref[2 * hp + hh])
            m = jnp.max(s, axis=-1, keepdims=True)
            p = jnp.exp2(s - m)
            l = jnp.sum(p, axis=-1, keepdims=True)
            o = jnp.dot(p.astype(BF16), vv, preferred_element_type=F32)
            outs.append(o / l)
        o_ref[:, ls] = jnp.where(lane < A_HEAD_DIM, outs[0], outs[1]).astype(o_ref.dtype)


def _attention(proj3, bias, q_tile, k_tile, v_tile):
    b, s, _ = proj3.shape
    tq = ATTN_TQ
    width = A_HEADS * A_HEAD_DIM
    left = LEFT_CHUNKS * CHUNK
    assert left % tq == 0 and s % tq == 0
    nprev = left // tq

    def kv_spec(tile, back):
        return pl.BlockSpec((None, tq, width), lambda bi, qi: (bi, jnp.maximum(qi - back, 0), tile))

    in_specs = [pl.BlockSpec((None, tq, width), lambda bi, qi: (bi, qi, q_tile))]
    in_specs += [kv_spec(k_tile, nprev - i) for i in range(nprev + 1)]
    in_specs += [kv_spec(v_tile, nprev - i) for i in range(nprev + 1)]
    in_specs += [pl.BlockSpec(bias.shape, lambda bi, qi: (0, 0, 0), pipeline_mode=pl.Buffered(1))]
    return pl.pallas_call(
        functools.partial(_attn_kernel, tq=tq, nprev=nprev),
        grid=(b, s // tq),
        in_specs=in_specs,
        out_specs=pl.BlockSpec((None, tq, width), lambda bi, qi: (bi, qi, 0)),
        out_shape=jax.ShapeDtypeStruct((b, s, width), BF16),
        compiler_params=_cparams("parallel", "parallel"),
        name="chunk_attn",
    )(proj3, *([proj3] * (2 * nprev + 2)), bias)


def _attn_bias_kernel(v_ref, o_ref, *, tq, nk):
    x = jnp.broadcast_to(v_ref[...], (tq, v_ref.shape[-1]))
    toeplitz = pltpu.roll(x, 0, 1, stride=1, stride_axis=0)[:, :nk]
    qc = lax.broadcasted_iota(jnp.int32, (tq, nk), 0) // CHUNK
    kc = lax.broadcasted_iota(jnp.int32, (tq, nk), 1) // CHUNK
    in_band = jnp.where(kc >= qc, kc - qc, LEFT_CHUNKS + 1) <= LEFT_CHUNKS
    o_ref[...] = jnp.where(in_band, toeplitz * LOG2_E, NEG)


def _attn_bias(rel_bias):
    tq = ATTN_TQ
    left = LEFT_CHUNKS * CHUNK
    nk = left + tq
    m_len = 1 << (tq + nk - 1).bit_length()
    m = jnp.arange(m_len)
    diff = jnp.where(m < nk, m, m - m_len)
    v = rel_bias[:, jnp.clip(left - diff, -REL_CLIP, REL_CLIP) + REL_CLIP].astype(F32)
    h = v.shape[0]
    return pl.pallas_call(
        functools.partial(_attn_bias_kernel, tq=tq, nk=nk),
        grid=(h,),
        in_specs=[pl.BlockSpec((None, 1, m_len), lambda i: (i, 0, 0))],
        out_specs=pl.BlockSpec((None, tq, nk), lambda i: (i, 0, 0)),
        out_shape=jax.ShapeDtypeStruct((h, tq, nk), F32),
        compiler_params=_cparams("parallel"),
        name="attn_bias",
    )(v.reshape(h, 1, m_len))


def _ssd_kernel(z0_ref, z1_ref, x0_ref, x1_ref, bc_ref, dt_ref, cwx_ref, cbx_ref, cwbc_ref, cbbc_ref,
                dtb_ref, aneg_ref, dskip_ref, gain_ref, e_ref, o_ref,
                xf_ref, bcf_ref, st_ref, xs_ref, y_ref, *, L):
    c = pl.program_id(1)
    inner = SSM_HEADS * SSM_HEAD_DIM
    gw = inner // SSM_GROUPS
    gs = SSM_GROUPS * SSM_STATE
    tail = V7X_SUBLANES
    half = inner // 2

    @pl.when(c == 0)
    def _():
        xf_ref[0:tail, :] = jnp.zeros((tail, inner), F32)
        bcf_ref[0:tail, :] = jnp.zeros((tail, 2 * gs), F32)
        st_ref[...] = jnp.zeros_like(st_ref)

    xf_ref[tail:, 0:half] = x0_ref[...].astype(F32)
    xf_ref[tail:, half:inner] = x1_ref[...].astype(F32)
    bcf_ref[tail:, :] = bc_ref[...].astype(F32)

    def conv_silu(src_ref, w_ref, b_ref, c0, c1):
        acc = b_ref[:, c0:c1] + w_ref[CONV_WIDTH - 1:CONV_WIDTH, c0:c1] * src_ref[tail:tail + L, c0:c1]
        for k in range(1, CONV_WIDTH):
            acc = acc + (w_ref[CONV_WIDTH - 1 - k:CONV_WIDTH - k, c0:c1]
                         * src_ref[tail - k:tail - k + L, c0:c1])
        return acc * jax.nn.sigmoid(acc)

    for g in range(SSM_GROUPS):
        xs_ref[:, g * gw:(g + 1) * gw] = conv_silu(xf_ref, cwx_ref, cbx_ref, g * gw, (g + 1) * gw)
    bmat = conv_silu(bcf_ref, cwbc_ref, cbbc_ref, 0, gs)
    cmat = conv_silu(bcf_ref, cwbc_ref, cbbc_ref, gs, 2 * gs)
    xf_ref[0:tail, :] = xf_ref[L:L + tail, :]
    bcf_ref[0:tail, :] = bcf_ref[L:L + tail, :]

    pre = dt_ref[...] + dtb_ref[...]
    dt = jnp.maximum(pre, 0.0) + jnp.log1p(jnp.exp(-jnp.abs(pre)))
    a = dt * aneg_ref[...]
    row = lax.broadcasted_iota(jnp.int32, (L, L), 0)
    colm = lax.broadcasted_iota(jnp.int32, (L, L), 1)
    lower = colm <= row
    tri = jnp.where(lower, 1.0, 0.0).astype(F32)
    cs = jnp.dot(tri, a, precision=HIGHEST, preferred_element_type=F32)
    cs_t = cs.T
    dt_t = dt.T
    cs_last = cs[L - 1:L, :]
    w_state = dt * jnp.exp(cs_last - cs)
    e_cs = jnp.exp(cs)
    chunk_decay = jnp.broadcast_to(jnp.exp(cs_last), (tail, V7X_LANES))
    stacked = jnp.concatenate([w_state, e_cs, chunk_decay], axis=0)
    s_hi, s_lo = _split_bf16(stacked)
    expanded = (jnp.dot(s_hi, e_ref[...], preferred_element_type=F32)
                + jnp.dot(s_lo, e_ref[...], preferred_element_type=F32))
    w_state_e = expanded[0:L]
    e_cs_e = expanded[L:2 * L]
    decay_e = expanded[2 * L:2 * L + 1]

    lane = lax.broadcasted_iota(jnp.int32, (1, V7X_LANES), 1)
    pairs_per_group = gw // V7X_LANES
    for g in range(SSM_GROUPS):
        bg = bmat[:, g * SSM_STATE:(g + 1) * SSM_STATE]
        cg = cmat[:, g * SSM_STATE:(g + 1) * SSM_STATE].astype(BF16)
        cb = lax.dot_general(cg, bg.astype(BF16), (((1,), (1,)), ((), ())),
                             preferred_element_type=F32)
        state_b = st_ref[g].astype(BF16)
        y_off = jnp.dot(cg, state_b, preferred_element_type=F32) * e_cs_e[:, g * gw:(g + 1) * gw]
        for pr in range(pairs_per_group):
            c0 = g * gw + pr * V7X_LANES
            xp = xs_ref[:, c0:c0 + V7X_LANES]
            xpb = xp.astype(BF16)
            acc = y_off[:, pr * V7X_LANES:(pr + 1) * V7X_LANES] + dskip_ref[:, c0:c0 + V7X_LANES] * xp
            for hh in range(2):
                h = c0 // SSM_HEAD_DIM + hh
                d = cs[:, h:h + 1] - cs_t[h:h + 1, :]
                m = cb * jnp.exp(jnp.where(lower, d, NEG)) * dt_t[h:h + 1, :]
                sel = (lane < SSM_HEAD_DIM) if hh == 0 else (lane >= SSM_HEAD_DIM)
                xm = jnp.where(sel, xpb, jnp.zeros_like(xpb))
                acc = acc + jnp.dot(m.astype(BF16), xm, preferred_element_type=F32)
            y_ref[:, c0:c0 + V7X_LANES] = acc
        xw = (xs_ref[:, g * gw:(g + 1) * gw] * w_state_e[:, g * gw:(g + 1) * gw]).astype(BF16)
        new = jnp.dot(bg.T.astype(BF16), xw, preferred_element_type=F32)
        st_ref[g] = st_ref[g] * decay_e[:, g * gw:(g + 1) * gw] + new

    for g in range(SSM_GROUPS):
        sl = slice(g * gw, (g + 1) * gw)
        z_ref, z0 = (z0_ref, g * gw) if g * gw < half else (z1_ref, g * gw - half)
        zz = z_ref[:, z0:z0 + gw].astype(F32)
        yz = y_ref[:, sl] * (zz * jax.nn.sigmoid(zz))
        ms = jnp.mean(yz * yz, axis=-1, keepdims=True)
        o_ref[:, sl] = (yz * lax.rsqrt(ms + EPS) * gain_ref[:, sl]).astype(o_ref.dtype)


def _ssd(proj3, dt3, cwx, cbx, cwbc, cbbc, dtb, aneg, dskip_e, gain, e_mat, z_tile, x_tile, bc_tile):
    b, s, _ = proj3.shape
    L = SSD_L
    assert s % L == 0
    inner = SSM_HEADS * SSM_HEAD_DIM
    gs2 = 2 * SSM_GROUPS * SSM_STATE
    assert gs2 == inner // 2
    const = lambda bi, ci: (0, 0)
    full = lambda a: pl.BlockSpec(a.shape, const)
    tile = lambda k: pl.BlockSpec((None, L, gs2), lambda bi, ci: (bi, ci, k))
    return pl.pallas_call(
        functools.partial(_ssd_kernel, L=L),
        grid=(b, s // L),
        in_specs=[
            tile(z_tile), tile(z_tile + 1), tile(x_tile), tile(x_tile + 1), tile(bc_tile),
            pl.BlockSpec((None, L, V7X_LANES), lambda bi, ci: (bi, ci, 0)),
            full(cwx), full(cbx), full(cwbc), full(cbbc), full(dtb), full(aneg), full(dskip_e),
            full(gain), full(e_mat),
        ],
        out_specs=pl.BlockSpec((None, L, inner), lambda bi, ci: (bi, ci, 0)),
        out_shape=jax.ShapeDtypeStruct((b, s, inner), BF16),
        scratch_shapes=[
            pltpu.VMEM((L + V7X_SUBLANES, inner), F32),
            pltpu.VMEM((L + V7X_SUBLANES, gs2), F32),
            pltpu.VMEM((SSM_GROUPS, SSM_STATE, inner // SSM_GROUPS), F32),
            pltpu.VMEM((L, inner), F32),
            pltpu.VMEM((L, inner), F32),
        ],
        compiler_params=_cparams("parallel", "arbitrary"),
        name="ssd",
    )(proj3, proj3, proj3, proj3, proj3, dt3, cwx, cbx, cwbc, cbbc, dtb, aneg, dskip_e, gain, e_mat)


def _mem_kv_kernel(mem_ref, g_ref, w_ref, kg_ref, k_ref, v_ref):
    m = mem_ref[...]
    ms = jnp.mean(m * m, axis=-1, keepdims=True)
    mn = (m * lax.rsqrt(ms + EPS) * g_ref[...]).astype(BF16)
    kv = jnp.dot(mn, w_ref[...], preferred_element_type=F32)
    width = X_HEADS * X_HEAD_DIM
    for h in range(X_HEADS):
        sl = slice(h * X_HEAD_DIM, (h + 1) * X_HEAD_DIM)
        kh = kv[:, sl]
        r = lax.rsqrt(jnp.mean(kh * kh, axis=-1, keepdims=True) + EPS)
        k_ref[:, sl] = (kh * r * kg_ref[...]).astype(k_ref.dtype)
    v_ref[...] = kv[:, width:].astype(v_ref.dtype)


def _mem_kv(mem, norm_mem, w_kv, k_gain):
    b, m, d = mem.shape
    width = X_HEADS * X_HEAD_DIM
    const = lambda bi: (0, 0)
    return pl.pallas_call(
        _mem_kv_kernel,
        grid=(b,),
        in_specs=[
            pl.BlockSpec((None, m, d), lambda bi: (bi, 0, 0)),
            pl.BlockSpec((1, d), const),
            pl.BlockSpec((d, 2 * width), const),
            pl.BlockSpec((1, X_HEAD_DIM), const),
        ],
        out_specs=[pl.BlockSpec((None, m, width), lambda bi: (bi, 0, 0))] * 2,
        out_shape=[jax.ShapeDtypeStruct((b, m, width), BF16)] * 2,
        compiler_params=_cparams("parallel"),
        name="mem_kv",
    )(mem, norm_mem, w_kv, k_gain)


def _mem_attn_kernel(q_ref, k_ref, v_ref, o_ref):
    for h in range(X_HEADS):
        sl = slice(h * X_HEAD_DIM, (h + 1) * X_HEAD_DIM)
        s = lax.dot_general(q_ref[:, sl], k_ref[:, sl], (((1,), (1,)), ((), ())),
                            preferred_element_type=F32)
        m = jnp.max(s, axis=-1, keepdims=True)
        p = jnp.exp(s - m)
        l = jnp.sum(p, axis=-1, keepdims=True)
        o = jnp.dot(p.astype(BF16), v_ref[:, sl], preferred_element_type=F32)
        o_ref[:, sl] = (o / l).astype(o_ref.dtype)


def _mem_attn(proj3, k, v, q_blk, tq):
    b, s, _ = proj3.shape
    m = k.shape[1]
    width = X_HEADS * X_HEAD_DIM
    assert s % tq == 0
    return pl.pallas_call(
        _mem_attn_kernel,
        grid=(b, s // tq),
        in_specs=[
            pl.BlockSpec((None, tq, width), lambda bi, qi: (bi, qi, q_blk)),
            pl.BlockSpec((None, m, width), lambda bi, qi: (bi, 0, 0)),
            pl.BlockSpec((None, m, width), lambda bi, qi: (bi, 0, 0)),
        ],
        out_specs=pl.BlockSpec((None, tq, width), lambda bi, qi: (bi, qi, 0)),
        out_shape=jax.ShapeDtypeStruct((b, s, width), BF16),
        compiler_params=_cparams("parallel", "parallel"),
        name="mem_attn",
    )(proj3, k, v)


def _merge_kernel(x_ref, ya_ref, yb_ref, yc_ref, g0_ref, g1_ref, g2_ref, wa_ref, wb_ref, wc_ref,
                  wo_ref, nf_ref, wrh_ref, wrl_ref, br_ref,
                  x1_ref, h2_ref, route_ref, cnt_ref, *, tm, rt):
    merged = (g0_ref[...].astype(F32) * jnp.dot(ya_ref[...], wa_ref[...], preferred_element_type=F32)
              + g1_ref[...].astype(F32) * jnp.dot(yb_ref[...], wb_ref[...], preferred_element_type=F32)
              + g2_ref[...].astype(F32) * jnp.dot(yc_ref[...], wc_ref[...], preferred_element_type=F32))
    x1 = x_ref[...] + jnp.dot(merged.astype(BF16), wo_ref[...], preferred_element_type=F32)
    x1_ref[...] = x1
    ms = jnp.mean(x1 * x1, axis=-1, keepdims=True)
    h2 = x1 * lax.rsqrt(ms + EPS) * nf_ref[...]
    h2_ref[...] = h2.astype(h2_ref.dtype)

    h_hi, h_lo = _split_bf16(h2)
    logits_all = (jnp.dot(h_hi, wrh_ref[...], preferred_element_type=F32)
                  + jnp.dot(h_lo, wrh_ref[...], preferred_element_type=F32)
                  + jnp.dot(h_hi, wrl_ref[...], preferred_element_type=F32)) + br_ref[...]
    for sub in range(tm // rt):
        _route_tile(logits_all[sub * rt:(sub + 1) * rt], route_ref.at[pl.ds(sub * rt, rt)],
                    cnt_ref.at[sub], rt)


def _route_tile(logits, route_ref, cnt_ref, tm):
    lane = lax.broadcasted_iota(jnp.int32, (tm, V7X_LANES), 1)
    lane_f = lane.astype(F32)
    work = jnp.where(lane < N_EXPERTS, logits, NEG)
    sel_val, sel_oh = [], []
    for _ in range(TOP_K):
        mval = jnp.max(work, axis=-1, keepdims=True)
        ik = jnp.min(jnp.where(work == mval, lane_f, float(V7X_LANES)), axis=-1, keepdims=True)
        oh = lane_f == ik
        work = jnp.where(oh, NEG, work)
        sel_val.append(mval)
        sel_oh.append(oh)
    ex = [jnp.exp(v - sel_val[0]) for v in sel_val]
    denom = ex[0] + ex[1] + ex[2] + ex[3]

    oh_all = jnp.zeros((tm, V7X_LANES), F32)
    for oh in sel_oh:
        oh_all = oh_all + jnp.where(oh, 1.0, 0.0)
    row = lax.broadcasted_iota(jnp.int32, (tm, tm), 0)
    colm = lax.broadcasted_iota(jnp.int32, (tm, tm), 1)
    strict = jnp.where(colm < row, 1.0, 0.0).astype(BF16)
    before = jnp.dot(strict, oh_all.astype(BF16), preferred_element_type=F32)
    cnt = jnp.sum(oh_all, axis=0, keepdims=True)
    cnt8 = jnp.floor((cnt + (V7X_SUBLANES - 1.0)) * (1.0 / V7X_SUBLANES)) * V7X_SUBLANES
    cnt8 = jnp.broadcast_to(cnt8, (V7X_SUBLANES, V7X_LANES))
    er = lax.broadcasted_iota(jnp.int32, (V7X_LANES, V7X_LANES), 0)
    ec = lax.broadcasted_iota(jnp.int32, (V7X_LANES, V7X_LANES), 1)
    earlier = jnp.where(er < ec, 1.0, 0.0).astype(BF16)
    run_start = jnp.dot(cnt8.astype(BF16), earlier, preferred_element_type=F32)[0:1, :]
    slot = before + run_start
    route = jnp.zeros((tm, V7X_LANES), F32)
    for k in range(TOP_K):
        pos = jnp.sum(jnp.where(sel_oh[k], slot, 0.0), axis=-1, keepdims=True)
        route = jnp.where(lane == k, pos, route)
        route = jnp.where(lane == TOP_K + k, ex[k] / denom, route)
    route_ref[...] = route
    cnt_ref[...] = cnt8


def _merge(x2d, ya, yb, yc, proj, wa, wb, wc, wo, nf, wr, br, gate_blk, tm, rt):
    t, d = x2d.shape
    assert t % tm == 0 and tm % rt == 0
    wr_hi, wr_lo = _split_bf16(wr)
    const = lambda i: (0, 0)
    full = lambda a: pl.BlockSpec(a.shape, const)
    rows = lambda w: pl.BlockSpec((tm, w), lambda i: (i, 0))
    return pl.pallas_call(
        functools.partial(_merge_kernel, tm=tm, rt=rt),
        grid=(t // tm,),
        in_specs=[
            rows(d), rows(ya.shape[1]), rows(yb.shape[1]), rows(yc.shape[1]),
            pl.BlockSpec((tm, d), lambda i: (i, gate_blk)),
            pl.BlockSpec((tm, d), lambda i: (i, gate_blk + 1)),
            pl.BlockSpec((tm, d), lambda i: (i, gate_blk + 2)),
            full(wa), full(wb), full(wc), full(wo), full(nf), full(wr_hi), full(wr_lo), full(br),
        ],
        out_specs=[rows(d), rows(d), rows(V7X_LANES),
                   pl.BlockSpec((tm // rt, V7X_SUBLANES, V7X_LANES), lambda i: (i, 0, 0))],
        out_shape=[
            jax.ShapeDtypeStruct((t, d), F32),
            jax.ShapeDtypeStruct((t, d), BF16),
            jax.ShapeDtypeStruct((t, V7X_LANES), F32),
            jax.ShapeDtypeStruct((t // rt, V7X_SUBLANES, V7X_LANES), F32),
        ],
        compiler_params=_cparams("parallel"),
        name="merge_route",
    )(x2d, ya, yb, yc, proj, proj, proj, wa, wb, wc, wo, nf, wr_hi, wr_lo, br)


RUN = V7X_SUBLANES
BLOCK_PIECES = MOE_ROWS // RUN
TAIL_SIZES = (32, 16, 8, 4, 2, 1)
assert MOE_TILE * TOP_K % V7X_LANES == 0 and N_EXPERTS * (RUN - 1) < 2 * TAIL_SIZES[0] * RUN


def _binary_pieces(n, body):
    for size in TAIL_SIZES:
        @pl.when((n & size) != 0)
        def _(size=size):
            body(n & ~(2 * size - 1), size)


def _dispatch_kernel(h_ref, route_ref, xs_ref, *, tm, ns):
    pos_t = route_ref[...].T
    q = lax.broadcasted_iota(jnp.int32, (ns, tm), 0).astype(F32)
    perm = jnp.zeros((ns, tm), F32)
    for k in range(TOP_K):
        perm = perm + jnp.where(q == pos_t[k:k + 1, :], 1.0, 0.0)
    xs_ref[...] = jnp.dot(perm.astype(BF16), h_ref[...], preferred_element_type=F32)


def _dispatch(h2, route, tm, ns):
    t, d = h2.shape
    return pl.pallas_call(
        functools.partial(_dispatch_kernel, tm=tm, ns=ns),
        grid=(t // tm,),
        in_specs=[
            pl.BlockSpec((tm, d), lambda i: (i, 0)),
            pl.BlockSpec((tm, V7X_LANES), lambda i: (i, 0)),
        ],
        out_specs=pl.BlockSpec((ns, d), lambda i: (i, 0)),
        out_shape=jax.ShapeDtypeStruct((t // tm * ns, d), F32),
        compiler_params=_cparams("parallel"),
        name="moe_dispatch",
    )(h2, route)


def _experts_kernel(bexp_ref, nused_ref, tail_ref, tab_ref, next_tab_ref, xs_ref, wg_ref, bg_ref,
                    wu_ref, bu_ref, wd_ref, bd_ref, yb_ref,
                    wgb_ref, wub_ref, wdb_ref, xbuf_ref, ybuf_ref, zero_ref, gsem, ssem, zsem, *,
                    ns, n_tiles):
    b = pl.program_id(0)
    n_used = nused_ref[0]
    slot = lax.rem(b, 2)

    def gather(t_ref, sl, j):
        src = pl.ds(pl.multiple_of(t_ref[0, j] * RUN, RUN), RUN)
        return pltpu.make_async_copy(xs_ref.at[src], xbuf_ref.at[sl, pl.ds(j * RUN, RUN)], gsem.at[sl])

    def scatter(t_ref, sl, j):
        dst = pl.ds(pl.multiple_of(t_ref[0, BLOCK_PIECES + j] * RUN, RUN), RUN)
        return pltpu.make_async_copy(ybuf_ref.at[sl, pl.ds(j * RUN, RUN)], yb_ref.at[dst], ssem.at[sl])

    def zero_copy(rows):
        return pltpu.make_async_copy(zero_ref.at[pl.ds(0, rows.size)], yb_ref.at[rows], zsem)

    def zero_unwritten(fn):
        for half in range(2):
            fn(zero_copy(pl.ds(n_tiles * ns + half * MOE_ROWS, MOE_ROWS)))

        def per_tile(i, carry):
            n = tail_ref[i]
            first = (i + 1) * ns - n * RUN
            _binary_pieces(n, lambda off, size: fn(zero_copy(
                pl.ds(pl.multiple_of(first + off * RUN, RUN), size * RUN))))
            return carry
        lax.fori_loop(0, n_tiles, per_tile, 0)

    @pl.when(b == 0)
    def _():
        zero_ref[...] = jnp.zeros_like(zero_ref)
        zero_unwritten(lambda cp: cp.start())
        zero_unwritten(lambda cp: cp.wait())
        for j in range(BLOCK_PIECES):
            gather(tab_ref, 0, j).start(priority=j % 2)

    @pl.when(b + 1 < n_used)
    def _():
        for j in range(BLOCK_PIECES):
            gather(next_tab_ref, 1 - slot, j).start(priority=j % 2)

    new_expert = jnp.logical_or(b == 0, bexp_ref[b] != bexp_ref[jnp.maximum(b - 1, 0)])

    @pl.when(new_expert)
    def _():
        wgb_ref[...] = wg_ref[...].astype(BF16)
        wub_ref[...] = wu_ref[...].astype(BF16)
        wdb_ref[...] = wd_ref[...].astype(BF16)

    @pl.when(b < n_used)
    def _():
        for j in range(BLOCK_PIECES):
            gather(tab_ref, slot, j).wait()

        @pl.when(b >= 2)
        def _():
            for j in range(BLOCK_PIECES):
                scatter(tab_ref, slot, j).wait()

        xb = xbuf_ref[slot].astype(BF16)
        g = jnp.dot(xb, wgb_ref[...], preferred_element_type=F32) + bg_ref[...]
        u = jnp.dot(xb, wub_ref[...], preferred_element_type=F32) + bu_ref[...]
        g = jnp.minimum(g, SWIGLU_LIMIT)
        u = jnp.clip(u, -SWIGLU_LIMIT, SWIGLU_LIMIT)
        act = (u + 1.0) * g * jax.nn.sigmoid(SWIGLU_ALPHA * g)
        ybuf_ref[slot] = (jnp.dot(act.astype(BF16), wdb_ref[...], preferred_element_type=F32)
                          + bd_ref[...])
        for j in range(BLOCK_PIECES):
            scatter(tab_ref, slot, j).start(priority=j % 2)

    @pl.when(b == n_used - 1)
    def _():
        for j in range(BLOCK_PIECES):
            scatter(tab_ref, slot, j).wait()

        @pl.when(b >= 1)
        def _():
            for j in range(BLOCK_PIECES):
                scatter(tab_ref, 1 - slot, j).wait()


def _experts(block_exp, n_used, tail, tab, xs, wg, bg, wu, bu, wd, bd, ns):
    d, de = wg.shape[1], wg.shape[2]
    n_blocks = tab.shape[0]
    n_tiles = xs.shape[0] // ns
    assert tab.shape[2] == 2 * BLOCK_PIECES == V7X_LANES
    wmap = lambda b, be, nu, tl: (be[b], 0, 0)
    smem_tab = lambda f: pl.BlockSpec((None, 1, V7X_LANES), f, memory_space=pltpu.SMEM)
    any_spec = pl.BlockSpec(memory_space=pl.ANY)
    return pl.pallas_call(
        functools.partial(_experts_kernel, ns=ns, n_tiles=n_tiles),
        grid_spec=pltpu.PrefetchScalarGridSpec(
            num_scalar_prefetch=3,
            grid=(n_blocks,),
            in_specs=[
                smem_tab(lambda b, be, nu, tl: (b, 0, 0)),
                smem_tab(lambda b, be, nu, tl: (jnp.minimum(b + 1, n_blocks - 1), 0, 0)),
                any_spec,
                pl.BlockSpec((None, d, de), wmap), pl.BlockSpec((None, 1, de), wmap),
                pl.BlockSpec((None, d, de), wmap), pl.BlockSpec((None, 1, de), wmap),
                pl.BlockSpec((None, de, d), wmap), pl.BlockSpec((None, 1, d), wmap),
            ],
            out_specs=any_spec,
            scratch_shapes=[
                pltpu.VMEM((d, de), BF16), pltpu.VMEM((d, de), BF16), pltpu.VMEM((de, d), BF16),
                pltpu.VMEM((2, MOE_ROWS, d), F32), pltpu.VMEM((2, MOE_ROWS, d), F32),
                pltpu.VMEM((MOE_ROWS, d), F32),
                pltpu.SemaphoreType.DMA((2,)), pltpu.SemaphoreType.DMA((2,)), pltpu.SemaphoreType.DMA,
            ],
        ),
        out_shape=jax.ShapeDtypeStruct((n_tiles * ns + 2 * MOE_ROWS, d), F32),
        compiler_params=_cparams("arbitrary"),
        name="moe_experts",
    )(block_exp, n_used, tail, tab, tab, xs, wg, bg, wu, bu, wd, bd)


def _combine_kernel(extra_ref, route_ref, x1_ref, yb_ref, o_ref, sorted_ref, sems, *, tm, ns):
    i = pl.program_id(0)
    slot = lax.rem(i, 2)
    base_rows = tm * TOP_K

    def copies(tile, sl, fn):
        def rows_copy(first, n_rows):
            src = pl.ds(pl.multiple_of(tile * ns + first, RUN), n_rows)
            dst = pl.ds(pl.multiple_of(first, RUN), n_rows)
            return pltpu.make_async_copy(yb_ref.at[src], sorted_ref.at[sl, dst], sems.at[sl])

        fn(rows_copy(0, base_rows))
        _binary_pieces(extra_ref[tile],
                       lambda off, size: fn(rows_copy(base_rows + off * RUN, size * RUN)))

    @pl.when(i == 0)
    def _():
        sorted_ref[...] = jnp.zeros_like(sorted_ref)
        copies(i, slot, lambda cp: cp.start())

    @pl.when(i + 1 < pl.num_programs(0))
    def _():
        copies(i + 1, 1 - slot, lambda cp: cp.start())

    copies(i, slot, lambda cp: cp.wait())

    ys = sorted_ref[slot].astype(BF16)
    route = route_ref[...]
    q = lax.broadcasted_iota(jnp.int32, (tm, ns), 1).astype(F32)
    wmat = jnp.zeros((tm, ns), F32)
    for k in range(TOP_K):
        wmat = wmat + jnp.where(q == route[:, k:k + 1], route[:, TOP_K + k:TOP_K + k + 1], 0.0)
    o_ref[...] = x1_ref[...] + jnp.dot(wmat.astype(BF16), ys, preferred_element_type=F32)


def _combine(extra, route, x1, yb, tm, ns):
    t, d = x1.shape
    return pl.pallas_call(
        functools.partial(_combine_kernel, tm=tm, ns=ns),
        grid=(t // tm,),
        in_specs=[
            pl.BlockSpec(memory_space=pltpu.SMEM),
            pl.BlockSpec((tm, V7X_LANES), lambda i: (i, 0)),
            pl.BlockSpec((tm, d), lambda i: (i, 0)),
            pl.BlockSpec(memory_space=pl.ANY),
        ],
        out_specs=pl.BlockSpec((tm, d), lambda i: (i, 0)),
        out_shape=jax.ShapeDtypeStruct((t, d), F32),
        scratch_shapes=[pltpu.VMEM((2, ns, d), F32), pltpu.SemaphoreType.DMA((2,))],
        compiler_params=_cparams("arbitrary"),
        name="moe_combine",
    )(extra, route, x1, yb)


def _head_indicator(width, head_dim):
    lane_head = jnp.arange(width) // head_dim
    return (lane_head[:, None] == jnp.arange(V7X_LANES)[None, :]).astype(BF16)


def _layer(x, mem, norm_mix, w_in, a_q_gain, a_k_gain, a_rel_bias, conv_w, conv_b, dt_bias, a_log,
           d_skip, ssm_norm, norm_mem, w_mem_kv, x_q_gain, x_k_gain, w_br_a, w_br_b, w_br_c, w_out,
           norm_ffn, w_router, b_router, w_gate, b_gate, w_up, b_up, w_down, b_down):
    b, s, d = x.shape
    t = b * s
    a_width = A_HEADS * A_HEAD_DIM
    inner = SSM_HEADS * SSM_HEAD_DIM
    gs = SSM_GROUPS * SSM_STATE
    x_width = X_HEADS * X_HEAD_DIM
    assert d == a_width == x_width and inner == 2 * d and 2 * gs == d

    o_dt = 3 * a_width + inner + inner + 2 * gs
    assert o_dt % d == 0
    w_main = jnp.concatenate([w_in[:, :o_dt], w_in[:, o_dt + SSM_HEADS:]], axis=1).astype(BF16)
    roles = ("qa", "ka", "plain", "plain", "plain", "plain", "plain", "plain", "qx", "sig", "sig", "sig")
    w_dt = jnp.pad(w_in[:, o_dt:o_dt + SSM_HEADS], ((0, 0), (0, V7X_LANES - SSM_HEADS))).astype(BF16)
    ind_a = _head_indicator(a_width, A_HEAD_DIM)
    ind_x = _head_indicator(x_width, X_HEAD_DIM)
    gains = jnp.zeros((V7X_SUBLANES, d), F32)
    gains = gains.at[0].set(jnp.tile(a_q_gain, A_HEADS) * (A_HEAD_DIM ** -0.5 * LOG2_E))
    gains = gains.at[1].set(jnp.tile(a_k_gain, A_HEADS))
    gains = gains.at[2].set(jnp.tile(x_q_gain, X_HEADS) * X_HEAD_DIM ** -0.5)

    x2d = x.reshape(t, d)
    proj, dt_raw = _in_proj(x2d, norm_mix.reshape(1, d), w_main, w_dt, ind_a, ind_a.T, ind_x, ind_x.T,
                            gains, roles, tm=min(1024, t), tn=d)
    proj3 = proj.reshape(b, s, proj.shape[1])

    y_a = _attention(proj3, _attn_bias(a_rel_bias), q_tile=0, k_tile=1, v_tile=2)

    pad_h = lambda v: jnp.pad(v.astype(F32), (0, V7X_LANES - SSM_HEADS)).reshape(1, V7X_LANES)
    e_mat = _head_indicator(inner, SSM_HEAD_DIM).T
    y_b = _ssd(proj3, dt_raw.reshape(b, s, V7X_LANES),
               conv_w[:, :inner], conv_b[:inner].reshape(1, inner),
               conv_w[:, inner:], conv_b[inner:].reshape(1, 2 * gs),
               pad_h(dt_bias), pad_h(-jnp.exp(a_log.astype(F32))),
               jnp.repeat(d_skip.astype(F32), SSM_HEAD_DIM).reshape(1, inner),
               ssm_norm.reshape(1, inner), e_mat, z_tile=3, x_tile=5, bc_tile=7)

    k_mem, v_mem = _mem_kv(mem, norm_mem.reshape(1, d), w_mem_kv.astype(BF16),
                           x_k_gain.reshape(1, X_HEAD_DIM))
    y_c = _mem_attn(proj3, k_mem, v_mem, q_blk=8, tq=min(512, s))

    w_r = jnp.pad(w_router, ((0, 0), (0, V7X_LANES - N_EXPERTS)))
    b_r = jnp.pad(b_router, (0, V7X_LANES - N_EXPERTS)).reshape(1, V7X_LANES)
    tm_moe = min(MOE_TILE, t)
    n_tiles = t // tm_moe
    x1, h2, route, tile_cnt = _merge(
        x2d, y_a.reshape(t, a_width), y_b.reshape(t, inner), y_c.reshape(t, x_width), proj,
        w_br_a.astype(BF16), w_br_b.astype(BF16), w_br_c.astype(BF16), w_out.astype(BF16),
        norm_ffn.reshape(1, d), w_r, b_r, gate_blk=9, tm=min(MERGE_TM, t), rt=tm_moe)

    ns = -(-(tm_moe * TOP_K + N_EXPERTS * (RUN - 1)) // V7X_LANES) * V7X_LANES
    units = ns // RUN
    n_blocks = -(-(t * TOP_K + n_tiles * N_EXPERTS * (RUN - 1) + N_EXPERTS * (MOE_ROWS - 1))
                 // MOE_ROWS)
    n8 = tile_cnt[:, 0, :N_EXPERTS].astype(jnp.int32) // RUN
    total = jnp.sum(n8, axis=0)
    padded = (total + BLOCK_PIECES - 1) // BLOCK_PIECES * BLOCK_PIECES
    pad_ends = jnp.cumsum(padded)
    pad_starts = pad_ends - padded
    n_used = (pad_ends[-1] // BLOCK_PIECES).reshape(1).astype(jnp.int32)
    blk = jnp.minimum(jnp.arange(n_blocks, dtype=jnp.int32), n_used[0] - 1)
    block_exp = jnp.minimum(jnp.sum(pad_ends[None, :] <= (blk * BLOCK_PIECES)[:, None], axis=1),
                            N_EXPERTS - 1).astype(jnp.int32)
    slot_j = jnp.arange(BLOCK_PIECES, dtype=jnp.int32)[None, :]
    onehot_e = (block_exp[:, None] == jnp.arange(N_EXPERTS)[None, :]).astype(jnp.int32)
    q = blk[:, None] * BLOCK_PIECES + slot_j - (onehot_e @ pad_starts)[:, None]
    real = q < (onehot_e @ total)[:, None]
    ends_b = onehot_e @ jnp.cumsum(n8, axis=0).T
    tile_of = jnp.minimum(jnp.sum(ends_b[:, None, :] <= q[:, :, None], axis=-1), n_tiles - 1)
    tile_1h = (tile_of[:, :, None] == jnp.arange(n_tiles)[None, None, :]).astype(jnp.int32)
    starts_b = ends_b - onehot_e @ n8.T
    in_tile_b = onehot_e @ (jnp.cumsum(n8, axis=1) - n8).T
    piece = (tile_of * units + jnp.sum(tile_1h * (in_tile_b - starts_b)[:, None, :], axis=-1) + q)
    zero_piece = (tm_moe * TOP_K + N_EXPERTS * (RUN - 1)) // RUN
    spare = n_tiles * units + (blk % 2)[:, None] * BLOCK_PIECES + slot_j
    tab = jnp.concatenate([jnp.where(real, piece, zero_piece), jnp.where(real, piece, spare)], axis=1)
    tab = tab.reshape(n_blocks, 1, 2 * BLOCK_PIECES).astype(jnp.int32)
    used = jnp.sum(n8, axis=1)
    tail = (units - used).astype(jnp.int32)
    extra = (used - tm_moe * TOP_K // RUN).astype(jnp.int32)

    xs = _dispatch(h2, route, tm_moe, ns)
    yb = _experts(block_exp, n_used, tail, tab, xs,
                  w_gate, b_gate.reshape(N_EXPERTS, 1, -1),
                  w_up, b_up.reshape(N_EXPERTS, 1, -1),
                  w_down, b_down.reshape(N_EXPERTS, 1, -1), ns)
    out = _combine(extra, route, x1, yb, tm_moe, ns)
    return out.reshape(b, s, d)


def kernel(x, mem, norm_mix, w_in, a_q_gain, a_k_gain, a_rel_bias, conv_w, conv_b, dt_bias, a_log, d_skip, ssm_norm, norm_mem, w_mem_kv, x_q_gain, x_k_gain, w_br_a, w_br_b, w_br_c, w_out, norm_ffn, w_router, b_router, w_gate, b_gate, w_up, b_up, w_down, b_down):
    for l in range(norm_mix.shape[0]):
        x = _layer(x, mem, norm_mix[l], w_in[l], a_q_gain[l], a_k_gain[l], a_rel_bias[l], conv_w[l],
                   conv_b[l], dt_bias[l], a_log[l], d_skip[l], ssm_norm[l], norm_mem[l], w_mem_kv[l],
                   x_q_gain[l], x_k_gain[l], w_br_a[l], w_br_b[l], w_br_c[l], w_out[l], norm_ffn[l],
                   w_router[l], b_router[l], w_gate[l], b_gate[l], w_up[l], b_up[l], w_down[l],
                   b_down[l])
    return x
```

```python
import functools

import jax
import jax.numpy as jnp
from jax import lax
from jax.experimental import pallas as pl
from jax.experimental.pallas import tpu as pltpu

F32 = jnp.float32
BF16 = jnp.bfloat16
HIGHEST = lax.Precision.HIGHEST

V7X_LANES = 128
V7X_SUBLANES = 8
V7X_VMEM_LIMIT_BYTES = 56 * 1024 * 1024

EPS = 1e-6
LOG2_E = 1.4426950408889634
NEG = -1e30

CHUNK = 64
A_HEADS = 16
A_HEAD_DIM = 64
LEFT_CHUNKS = 8
REL_CLIP = 128
SSM_HEADS = 32
SSM_HEAD_DIM = 64
SSM_GROUPS = 4
SSM_STATE = 128
CONV_WIDTH = 4
X_HEADS = 4
X_HEAD_DIM = 256
N_EXPERTS = 32
TOP_K = 4
SWIGLU_LIMIT = 7.0
SWIGLU_ALPHA = 1.702

ATTN_TQ = 256
SSD_L = 256
MOE_ROWS = 512
MOE_TILE = 256
MERGE_TM = 512


def _cparams(*sem):
    return pltpu.CompilerParams(dimension_semantics=sem, vmem_limit_bytes=V7X_VMEM_LIMIT_BYTES)


def _split_bf16(v):
    hi = v.astype(BF16)
    lo = (v - hi.astype(F32)).astype(BF16)
    return hi, lo


def _in_proj_kernel(x_ref, nw_ref, w_ref, wdt_ref, gains_ref, o_ref, dt_ref, h_ref, *, roles):
    j = pl.program_id(1)

    @pl.when(j == 0)
    def _():
        x = x_ref[...]
        ms = jnp.mean(x * x, axis=-1, keepdims=True)
        hb = (x * lax.rsqrt(ms + EPS) * nw_ref[...]).astype(BF16)
        h_ref[...] = hb
        dt_ref[...] = jnp.dot(hb, wdt_ref[...], preferred_element_type=F32)

    def head_norm(acc, head_dim, gain_row):
        outs = []
        if head_dim >= V7X_LANES:
            for c0 in range(0, acc.shape[1], head_dim):
                blk = acc[:, c0:c0 + head_dim]
                s = jnp.sum(blk * blk, axis=-1, keepdims=True)
                scale = lax.rsqrt(s * (1.0 / head_dim) + EPS)
                outs.append(blk * scale * gain_row[:, c0:c0 + head_dim])
        else:
            assert 2 * head_dim == V7X_LANES
            lo = lax.broadcasted_iota(jnp.int32, (1, V7X_LANES), 1) < head_dim
            for c0 in range(0, acc.shape[1], V7X_LANES):
                blk = acc[:, c0:c0 + V7X_LANES]
                sq = blk * blk
                s_lo = jnp.sum(jnp.where(lo, sq, 0.0), axis=-1, keepdims=True)
                s_hi = jnp.sum(jnp.where(lo, 0.0, sq), axis=-1, keepdims=True)
                scale = jnp.where(lo, lax.rsqrt(s_lo * (1.0 / head_dim) + EPS),
                                  lax.rsqrt(s_hi * (1.0 / head_dim) + EPS))
                outs.append(blk * scale * gain_row[:, c0:c0 + V7X_LANES])
        return jnp.concatenate(outs, axis=1)

    def cond_for(role):
        c = None
        for jj, r in enumerate(roles):
            if r == role:
                c = (j == jj) if c is None else jnp.logical_or(c, j == jj)
        return c

    for role in sorted(set(roles)):
        @pl.when(cond_for(role))
        def _(role=role):
            acc = jnp.dot(h_ref[...], w_ref[...], preferred_element_type=F32)
            if role == "qa":
                out = head_norm(acc, A_HEAD_DIM, gains_ref[0:1, :])
            elif role == "ka":
                out = head_norm(acc, A_HEAD_DIM, gains_ref[1:2, :])
            elif role == "qx":
                out = head_norm(acc, X_HEAD_DIM, gains_ref[2:3, :])
            elif role == "sig":
                out = 0.5 * jnp.tanh(0.5 * acc) + 0.5
            else:
                out = acc
            o_ref[...] = out.astype(o_ref.dtype)


def _in_proj(x2d, norm_w, w_main, w_dt, gains, roles, tm, tn):
    t, d = x2d.shape
    n = w_main.shape[1]
    assert t % tm == 0 and n % tn == 0 and len(roles) == n // tn
    const = lambda i, j: (0, 0)
    return pl.pallas_call(
        functools.partial(_in_proj_kernel, roles=roles),
        grid=(t // tm, n // tn),
        in_specs=[
            pl.BlockSpec((tm, d), lambda i, j: (i, 0)),
            pl.BlockSpec((1, d), const),
            pl.BlockSpec((d, tn), lambda i, j: (0, j)),
            pl.BlockSpec((d, V7X_LANES), const),
            pl.BlockSpec(gains.shape, const),
        ],
        out_specs=[
            pl.BlockSpec((tm, tn), lambda i, j: (i, j)),
            pl.BlockSpec((tm, V7X_LANES), lambda i, j: (i, 0)),
        ],
        out_shape=[
            jax.ShapeDtypeStruct((t, n), BF16),
            jax.ShapeDtypeStruct((t, V7X_LANES), F32),
        ],
        scratch_shapes=[pltpu.VMEM((tm, d), BF16)],
        compiler_params=_cparams("parallel", "arbitrary"),
        name="in_proj",
    )(x2d, norm_w, w_main, w_dt, gains)


def _attn_kernel(q_ref, *refs, tq, nprev):
    k_refs = refs[:nprev + 1]
    v_refs = refs[nprev + 1:2 * nprev + 2]
    bias_ref, o_ref = refs[2 * nprev + 2:]
    qb = pl.program_id(1)
    nk = (nprev + 1) * tq
    lane = lax.broadcasted_iota(jnp.int32, (1, V7X_LANES), 1)
    col = lax.broadcasted_iota(jnp.int32, (1, nk), 1)
    before_start = col < (nprev - qb) * tq
    for hp in range(A_HEADS // 2):
        ls = slice(hp * V7X_LANES, (hp + 1) * V7X_LANES)
        q2 = q_ref[:, ls]
        kk = jnp.concatenate([r[:, ls] for r in k_refs], axis=0)
        vv = jnp.concatenate([r[:, ls] for r in v_refs], axis=0)
        outs = []
        for hh in range(2):
            sel = (lane < A_HEAD_DIM) if hh == 0 else (lane >= A_HEAD_DIM)
            qm = jnp.where(sel, q2, jnp.zeros_like(q2))
            s = lax.dot_general(qm, kk, (((1,), (1,)), ((), ())), preferred_element_type=F32)
            s = jnp.where(before_start, NEG, s + bias_ref[2 * hp + hh])
            m = jnp.max(s, axis=-1, keepdims=True)
            p = jnp.exp2(s - m)
            l = jnp.sum(p, axis=-1, keepdims=True)
            o = jnp.dot(p.astype(BF16), vv, preferred_element_type=F32)
            outs.append(o / l)
        o_ref[:, ls] = jnp.where(lane < A_HEAD_DIM, outs[0], outs[1]).astype(o_ref.dtype)


def _attention(proj3, bias, q_tile, k_tile, v_tile):
    b, s, _ = proj3.shape
    tq = ATTN_TQ
    width = A_HEADS * A_HEAD_DIM
    left = LEFT_CHUNKS * CHUNK
    assert left % tq == 0 and s % tq == 0
    nprev = left // tq

    def kv_spec(tile, back):
        return pl.BlockSpec((None, tq, width), lambda bi, qi: (bi, jnp.maximum(qi - back, 0), tile))

    in_specs = [pl.BlockSpec((None, tq, width), lambda bi, qi: (bi, qi, q_tile))]
    in_specs += [kv_spec(k_tile, nprev - i) for i in range(nprev + 1)]
    in_specs += [kv_spec(v_tile, nprev - i) for i in range(nprev + 1)]
    in_specs += [pl.BlockSpec(bias.shape, lambda bi, qi: (0, 0, 0), pipeline_mode=pl.Buffered(1))]
    return pl.pallas_call(
        functools.partial(_attn_kernel, tq=tq, nprev=nprev),
        grid=(b, s // tq),
        in_specs=in_specs,
        out_specs=pl.BlockSpec((None, tq, width), lambda bi, qi: (bi, qi, 0)),
        out_shape=jax.ShapeDtypeStruct((b, s, width), BF16),
        compiler_params=_cparams("parallel", "parallel"),
        name="chunk_attn",
    )(proj3, *([proj3] * (2 * nprev + 2)), bias)


def _attn_bias_kernel(v_ref, o_ref, *, tq, nk):
    x = jnp.broadcast_to(v_ref[...], (tq, v_ref.shape[-1]))
    toeplitz = pltpu.roll(x, 0, 1, stride=1, stride_axis=0)[:, :nk]
    qc = lax.broadcasted_iota(jnp.int32, (tq, nk), 0) // CHUNK
    kc = lax.broadcasted_iota(jnp.int32, (tq, nk), 1) // CHUNK
    in_band = jnp.where(kc >= qc, kc - qc, LEFT_CHUNKS + 1) <= LEFT_CHUNKS
    o_ref[...] = jnp.where(in_band, toeplitz * LOG2_E, NEG)


def _attn_bias(rel_bias):
    tq = ATTN_TQ
    left = LEFT_CHUNKS * CHUNK
    nk = left + tq
    m_len = 1 << (tq + nk - 1).bit_length()
    m = jnp.arange(m_len)
    diff = jnp.where(m < nk, m, m - m_len)
    v = rel_bias[:, jnp.clip(left - diff, -REL_CLIP, REL_CLIP) + REL_CLIP].astype(F32)
    h = v.shape[0]
    return pl.pallas_call(
        functools.partial(_attn_bias_kernel, tq=tq, nk=nk),
        grid=(h,),
        in_specs=[pl.BlockSpec((None, 1, m_len), lambda i: (i, 0, 0))],
        out_specs=pl.BlockSpec((None, tq, nk), lambda i: (i, 0, 0)),
        out_shape=jax.ShapeDtypeStruct((h, tq, nk), F32),
        compiler_params=_cparams("parallel"),
        name="attn_bias",
    )(v.reshape(h, 1, m_len))


def _ssd_kernel(z0_ref, z1_ref, x0_ref, x1_ref, bc_ref, dt_ref, cwx_ref, cbx_ref, cwbc_ref, cbbc_ref,
                dtb_ref, aneg_ref, dskip_ref, gain_ref, e_ref, o_ref,
                xf_ref, bcf_ref, st_ref, xs_ref, y_ref, *, L):
    c = pl.program_id(1)
    inner = SSM_HEADS * SSM_HEAD_DIM
    gw = inner // SSM_GROUPS
    gs = SSM_GROUPS * SSM_STATE
    tail = V7X_SUBLANES
    half = inner // 2

    @pl.when(c == 0)
    def _():
        xf_ref[0:tail, :] = jnp.zeros((tail, inner), F32)
        bcf_ref[0:tail, :] = jnp.zeros((tail, 2 * gs), F32)
        st_ref[...] = jnp.zeros_like(st_ref)

    xf_ref[tail:, 0:half] = x0_ref[...].astype(F32)
    xf_ref[tail:, half:inner] = x1_ref[...].astype(F32)
    bcf_ref[tail:, :] = bc_ref[...].astype(F32)

    def conv_silu(src_ref, w_ref, b_ref, c0, c1):
        acc = b_ref[:, c0:c1] + w_ref[CONV_WIDTH - 1:CONV_WIDTH, c0:c1] * src_ref[tail:tail + L, c0:c1]
        for k in range(1, CONV_WIDTH):
            acc = acc + (w_ref[CONV_WIDTH - 1 - k:CONV_WIDTH - k, c0:c1]
                         * src_ref[tail - k:tail - k + L, c0:c1])
        return acc * jax.nn.sigmoid(acc)

    for g in range(SSM_GROUPS):
        xs_ref[:, g * gw:(g + 1) * gw] = conv_silu(xf_ref, cwx_ref, cbx_ref, g * gw, (g + 1) * gw)
    bmat = conv_silu(bcf_ref, cwbc_ref, cbbc_ref, 0, gs)
    cmat = conv_silu(bcf_ref, cwbc_ref, cbbc_ref, gs, 2 * gs)
    xf_ref[0:tail, :] = xf_ref[L:L + tail, :]
    bcf_ref[0:tail, :] = bcf_ref[L:L + tail, :]

    pre = dt_ref[...] + dtb_ref[...]
    dt = jnp.maximum(pre, 0.0) + jnp.log1p(jnp.exp(-jnp.abs(pre)))
    a = dt * aneg_ref[...]
    row = lax.broadcasted_iota(jnp.int32, (L, L), 0)
    colm = lax.broadcasted_iota(jnp.int32, (L, L), 1)
    lower = colm <= row
    tri = jnp.where(lower, 1.0, 0.0).astype(F32)
    cs = jnp.dot(tri, a, precision=HIGHEST, preferred_element_type=F32)
    cs_t = cs.T
    dt_t = dt.T
    cs_last = cs[L - 1:L, :]
    w_state = dt * jnp.exp(cs_last - cs)
    e_cs = jnp.exp(cs)
    chunk_decay = jnp.broadcast_to(jnp.exp(cs_last), (tail, V7X_LANES))
    stacked = jnp.concatenate([w_state, e_cs, chunk_decay], axis=0)
    s_hi, s_lo = _split_bf16(stacked)
    expanded = (jnp.dot(s_hi, e_ref[...], preferred_element_type=F32)
                + jnp.dot(s_lo, e_ref[...], preferred_element_type=F32))
    w_state_e = expanded[0:L]
    e_cs_e = expanded[L:2 * L]
    decay_e = expanded[2 * L:2 * L + 1]

    lane = lax.broadcasted_iota(jnp.int32, (1, V7X_LANES), 1)
    pairs_per_group = gw // V7X_LANES
    for g in range(SSM_GROUPS):
        bg = bmat[:, g * SSM_STATE:(g + 1) * SSM_STATE]
        cg = cmat[:, g * SSM_STATE:(g + 1) * SSM_STATE].astype(BF16)
        cb = lax.dot_general(cg, bg.astype(BF16), (((1,), (1,)), ((), ())),
                             preferred_element_type=F32)
        state_b = st_ref[g].astype(BF16)
        y_off = jnp.dot(cg, state_b, preferred_element_type=F32) * e_cs_e[:, g * gw:(g + 1) * gw]
        for pr in range(pairs_per_group):
            c0 = g * gw + pr * V7X_LANES
            xp = xs_ref[:, c0:c0 + V7X_LANES]
            xpb = xp.astype(BF16)
            acc = y_off[:, pr * V7X_LANES:(pr + 1) * V7X_LANES] + dskip_ref[:, c0:c0 + V7X_LANES] * xp
            for hh in range(2):
                h = c0 // SSM_HEAD_DIM + hh
                d = cs[:, h:h + 1] - cs_t[h:h + 1, :]
                m = cb * jnp.exp(jnp.where(lower, d, NEG)) * dt_t[h:h + 1, :]
                sel = (lane < SSM_HEAD_DIM) if hh == 0 else (lane >= SSM_HEAD_DIM)
                xm = jnp.where(sel, xpb, jnp.zeros_like(xpb))
                acc = acc + jnp.dot(m.astype(BF16), xm, preferred_element_type=F32)
            y_ref[:, c0:c0 + V7X_LANES] = acc
        xw = (xs_ref[:, g * gw:(g + 1) * gw] * w_state_e[:, g * gw:(g + 1) * gw]).astype(BF16)
        new = jnp.dot(bg.T.astype(BF16), xw, preferred_element_type=F32)
        st_ref[g] = st_ref[g] * decay_e[:, g * gw:(g + 1) * gw] + new

    for g in range(SSM_GROUPS):
        sl = slice(g * gw, (g + 1) * gw)
        z_ref, z0 = (z0_ref, g * gw) if g * gw < half else (z1_ref, g * gw - half)
        zz = z_ref[:, z0:z0 + gw].astype(F32)
        yz = y_ref[:, sl] * (zz * jax.nn.sigmoid(zz))
        ms = jnp.mean(yz * yz, axis=-1, keepdims=True)
        o_ref[:, sl] = (yz * lax.rsqrt(ms + EPS) * gain_ref[:, sl]).astype(o_ref.dtype)


def _ssd(proj3, dt3, cwx, cbx, cwbc, cbbc, dtb, aneg, dskip_e, gain, e_mat, z_tile, x_tile, bc_tile):
    b, s, _ = proj3.shape
    L = SSD_L
    assert s % L == 0
    inner = SSM_HEADS * SSM_HEAD_DIM
    gs2 = 2 * SSM_GROUPS * SSM_STATE
    assert gs2 == inner // 2
    const = lambda bi, ci: (0, 0)
    full = lambda a: pl.BlockSpec(a.shape, const)
    tile = lambda k: pl.BlockSpec((None, L, gs2), lambda bi, ci: (bi, ci, k))
    return pl.pallas_call(
        functools.partial(_ssd_kernel, L=L),
        grid=(b, s // L),
        in_specs=[
            tile(z_tile), tile(z_tile + 1), tile(x_tile), tile(x_tile + 1), tile(bc_tile),
            pl.BlockSpec((None, L, V7X_LANES), lambda bi, ci: (bi, ci, 0)),
            full(cwx), full(cbx), full(cwbc), full(cbbc), full(dtb), full(aneg), full(dskip_e),
            full(gain), full(e_mat),
        ],
        out_specs=pl.BlockSpec((None, L, inner), lambda bi, ci: (bi, ci, 0)),
        out_shape=jax.ShapeDtypeStruct((b, s, inner), BF16),
        scratch_shapes=[
            pltpu.VMEM((L + V7X_SUBLANES, inner), F32),
            pltpu.VMEM((L + V7X_SUBLANES, gs2), F32),
            pltpu.VMEM((SSM_GROUPS, SSM_STATE, inner // SSM_GROUPS), F32),
            pltpu.VMEM((L, inner), F32),
            pltpu.VMEM((L, inner), F32),
        ],
        compiler_params=_cparams("parallel", "arbitrary"),
        name="ssd",
    )(proj3, proj3, proj3, proj3, proj3, dt3, cwx, cbx, cwbc, cbbc, dtb, aneg, dskip_e, gain, e_mat)


def _mem_kv_kernel(mem_ref, g_ref, w_ref, kg_ref, k_ref, v_ref):
    m = mem_ref[...]
    ms = jnp.mean(m * m, axis=-1, keepdims=True)
    mn = (m * lax.rsqrt(ms + EPS) * g_ref[...]).astype(BF16)
    kv = jnp.dot(mn, w_ref[...], preferred_element_type=F32)
    width = X_HEADS * X_HEAD_DIM
    for h in range(X_HEADS):
        sl = slice(h * X_HEAD_DIM, (h + 1) * X_HEAD_DIM)
        kh = kv[:, sl]
        r = lax.rsqrt(jnp.mean(kh * kh, axis=-1, keepdims=True) + EPS)
        k_ref[:, sl] = (kh * r * kg_ref[...]).astype(k_ref.dtype)
    v_ref[...] = kv[:, width:].astype(v_ref.dtype)


def _mem_kv(mem, norm_mem, w_kv, k_gain):
    b, m, d = mem.shape
    width = X_HEADS * X_HEAD_DIM
    const = lambda bi: (0, 0)
    return pl.pallas_call(
        _mem_kv_kernel,
        grid=(b,),
        in_specs=[
            pl.BlockSpec((None, m, d), lambda bi: (bi, 0, 0)),
            pl.BlockSpec((1, d), const),
            pl.BlockSpec((d, 2 * width), const),
            pl.BlockSpec((1, X_HEAD_DIM), const),
        ],
        out_specs=[pl.BlockSpec((None, m, width), lambda bi: (bi, 0, 0))] * 2,
        out_shape=[jax.ShapeDtypeStruct((b, m, width), BF16)] * 2,
        compiler_params=_cparams("parallel"),
        name="mem_kv",
    )(mem, norm_mem, w_kv, k_gain)


def _mem_attn_kernel(q_ref, k_ref, v_ref, o_ref):
    for h in range(X_HEADS):
        sl = slice(h * X_HEAD_DIM, (h + 1) * X_HEAD_DIM)
        s = lax.dot_general(q_ref[:, sl], k_ref[:, sl], (((1,), (1,)), ((), ())),
                            preferred_element_type=F32)
        m = jnp.max(s, axis=-1, keepdims=True)
        p = jnp.exp(s - m)
        l = jnp.sum(p, axis=-1, keepdims=True)
        o = jnp.dot(p.astype(BF16), v_ref[:, sl], preferred_element_type=F32)
        o_ref[:, sl] = (o / l).astype(o_ref.dtype)


def _mem_attn(proj3, k, v, q_blk, tq):
    b, s, _ = proj3.shape
    m = k.shape[1]
    width = X_HEADS * X_HEAD_DIM
    assert s % tq == 0
    return pl.pallas_call(
        _mem_attn_kernel,
        grid=(b, s // tq),
        in_specs=[
            pl.BlockSpec((None, tq, width), lambda bi, qi: (bi, qi, q_blk)),
            pl.BlockSpec((None, m, width), lambda bi, qi: (bi, 0, 0)),
            pl.BlockSpec((None, m, width), lambda bi, qi: (bi, 0, 0)),
        ],
        out_specs=pl.BlockSpec((None, tq, width), lambda bi, qi: (bi, qi, 0)),
        out_shape=jax.ShapeDtypeStruct((b, s, width), BF16),
        compiler_params=_cparams("parallel", "parallel"),
        name="mem_attn",
    )(proj3, k, v)


def _merge_kernel(x_ref, ya_ref, yb_ref, yc_ref, g0_ref, g1_ref, g2_ref, wa_ref, wb_ref, wc_ref,
                  wo_ref, nf_ref, wrh_ref, wrl_ref, br_ref,
                  x1_ref, h2_ref, route_ref, cnt_ref, *, tm, rt):
    merged = (g0_ref[...].astype(F32) * jnp.dot(ya_ref[...], wa_ref[...], preferred_element_type=F32)
              + g1_ref[...].astype(F32) * jnp.dot(yb_ref[...], wb_ref[...], preferred_element_type=F32)
              + g2_ref[...].astype(F32) * jnp.dot(yc_ref[...], wc_ref[...], preferred_element_type=F32))
    x1 = x_ref[...] + jnp.dot(merged.astype(BF16), wo_ref[...], preferred_element_type=F32)
    x1_ref[...] = x1
    ms = jnp.mean(x1 * x1, axis=-1, keepdims=True)
    h2 = x1 * lax.rsqrt(ms + EPS) * nf_ref[...]
    h2_ref[...] = h2.astype(h2_ref.dtype)

    h_hi, h_lo = _split_bf16(h2)
    logits_all = (jnp.dot(h_hi, wrh_ref[...], preferred_element_type=F32)
                  + jnp.dot(h_lo, wrh_ref[...], preferred_element_type=F32)
                  + jnp.dot(h_hi, wrl_ref[...], preferred_element_type=F32)) + br_ref[...]
    for sub in range(tm // rt):
        _route_tile(logits_all[sub * rt:(sub + 1) * rt], route_ref.at[pl.ds(sub * rt, rt)],
                    cnt_ref.at[sub], rt)


def _route_tile(logits, route_ref, cnt_ref, tm):
    lane = lax.broadcasted_iota(jnp.int32, (tm, V7X_LANES), 1)
    lane_f = lane.astype(F32)
    work = jnp.where(lane < N_EXPERTS, logits, NEG)
    sel_val, sel_oh = [], []
    for _ in range(TOP_K):
        mval = jnp.max(work, axis=-1, keepdims=True)
        ik = jnp.min(jnp.where(work == mval, lane_f, float(V7X_LANES)), axis=-1, keepdims=True)
        oh = lane_f == ik
        work = jnp.where(oh, NEG, work)
        sel_val.append(mval)
        sel_oh.append(oh)
    ex = [jnp.exp(v - sel_val[0]) for v in sel_val]
    denom = ex[0] + ex[1] + ex[2] + ex[3]

    oh_all = jnp.zeros((tm, V7X_LANES), F32)
    for oh in sel_oh:
        oh_all = oh_all + jnp.where(oh, 1.0, 0.0)
    row = lax.broadcasted_iota(jnp.int32, (tm, tm), 0)
    colm = lax.broadcasted_iota(jnp.int32, (tm, tm), 1)
    strict = jnp.where(colm < row, 1.0, 0.0).astype(BF16)
    before = jnp.dot(strict, oh_all.astype(BF16), preferred_element_type=F32)
    cnt = jnp.sum(oh_all, axis=0, keepdims=True)
    cnt8 = jnp.floor((cnt + (V7X_SUBLANES - 1.0)) * (1.0 / V7X_SUBLANES)) * V7X_SUBLANES
    cnt8 = jnp.broadcast_to(cnt8, (V7X_SUBLANES, V7X_LANES))
    er = lax.broadcasted_iota(jnp.int32, (V7X_LANES, V7X_LANES), 0)
    ec = lax.broadcasted_iota(jnp.int32, (V7X_LANES, V7X_LANES), 1)
    earlier = jnp.where(er < ec, 1.0, 0.0).astype(BF16)
    run_start = jnp.dot(cnt8.astype(BF16), earlier, preferred_element_type=F32)[0:1, :]
    slot = before + run_start
    route = jnp.zeros((tm, V7X_LANES), F32)
    for k in range(TOP_K):
        pos = jnp.sum(jnp.where(sel_oh[k], slot, 0.0), axis=-1, keepdims=True)
        route = jnp.where(lane == k, pos, route)
        route = jnp.where(lane == TOP_K + k, ex[k] / denom, route)
    route_ref[...] = route
    cnt_ref[...] = cnt8


def _merge(x2d, ya, yb, yc, proj, wa, wb, wc, wo, nf, wr, br, gate_blk, tm, rt):
    t, d = x2d.shape
    assert t % tm == 0 and tm % rt == 0
    wr_hi, wr_lo = _split_bf16(wr)
    const = lambda i: (0, 0)
    full = lambda a: pl.BlockSpec(a.shape, const)
    rows = lambda w: pl.BlockSpec((tm, w), lambda i: (i, 0))
    return pl.pallas_call(
        functools.partial(_merge_kernel, tm=tm, rt=rt),
        grid=(t // tm,),
        in_specs=[
            rows(d), rows(ya.shape[1]), rows(yb.shape[1]), rows(yc.shape[1]),
            pl.BlockSpec((tm, d), lambda i: (i, gate_blk)),
            pl.BlockSpec((tm, d), lambda i: (i, gate_blk + 1)),
            pl.BlockSpec((tm, d), lambda i: (i, gate_blk + 2)),
            full(wa), full(wb), full(wc), full(wo), full(nf), full(wr_hi), full(wr_lo), full(br),
        ],
        out_specs=[rows(d), rows(d), rows(V7X_LANES),
                   pl.BlockSpec((tm // rt, V7X_SUBLANES, V7X_LANES), lambda i: (i, 0, 0))],
        out_shape=[
            jax.ShapeDtypeStruct((t, d), F32),
            jax.ShapeDtypeStruct((t, d), BF16),
            jax.ShapeDtypeStruct((t, V7X_LANES), F32),
            jax.ShapeDtypeStruct((t // rt, V7X_SUBLANES, V7X_LANES), F32),
        ],
        compiler_params=_cparams("parallel"),
        name="merge_route",
    )(x2d, ya, yb, yc, proj, proj, proj, wa, wb, wc, wo, nf, wr_hi, wr_lo, br)


RUN = V7X_SUBLANES
BLOCK_PIECES = MOE_ROWS // RUN
TAIL_SIZES = (32, 16, 8, 4, 2, 1)
assert MOE_TILE * TOP_K % V7X_LANES == 0 and N_EXPERTS * (RUN - 1) < 2 * TAIL_SIZES[0] * RUN


def _binary_pieces(n, body):
    for size in TAIL_SIZES:
        @pl.when((n & size) != 0)
        def _(size=size):
            body(n & ~(2 * size - 1), size)


def _dispatch_kernel(h_ref, route_ref, xs_ref, *, tm, ns):
    pos_t = route_ref[...].T
    q = lax.broadcasted_iota(jnp.int32, (ns, tm), 0).astype(F32)
    perm = jnp.zeros((ns, tm), F32)
    for k in range(TOP_K):
        perm = perm + jnp.where(q == pos_t[k:k + 1, :], 1.0, 0.0)
    xs_ref[...] = jnp.dot(perm.astype(BF16), h_ref[...], preferred_element_type=F32)


def _dispatch(h2, route, tm, ns):
    t, d = h2.shape
    return pl.pallas_call(
        functools.partial(_dispatch_kernel, tm=tm, ns=ns),
        grid=(t // tm,),
        in_specs=[
            pl.BlockSpec((tm, d), lambda i: (i, 0)),
            pl.BlockSpec((tm, V7X_LANES), lambda i: (i, 0)),
        ],
        out_specs=pl.BlockSpec((ns, d), lambda i: (i, 0)),
        out_shape=jax.ShapeDtypeStruct((t // tm * ns, d), F32),
        compiler_params=_cparams("parallel"),
        name="moe_dispatch",
    )(h2, route)


def _experts_kernel(bexp_ref, nused_ref, tail_ref, tab_ref, next_tab_ref, xs_ref, wg_ref, bg_ref,
                    wu_ref, bu_ref, wd_ref, bd_ref, yb_ref,
                    wgb_ref, wub_ref, wdb_ref, xbuf_ref, ybuf_ref, zero_ref, gsem, ssem, zsem, *,
                    ns, n_tiles):
    b = pl.program_id(0)
    n_used = nused_ref[0]
    slot = lax.rem(b, 2)

    def gather(t_ref, sl, j):
        src = pl.ds(pl.multiple_of(t_ref[0, j] * RUN, RUN), RUN)
        return pltpu.make_async_copy(xs_ref.at[src], xbuf_ref.at[sl, pl.ds(j * RUN, RUN)], gsem.at[sl])

    def scatter(t_ref, sl, j):
        dst = pl.ds(pl.multiple_of(t_ref[0, BLOCK_PIECES + j] * RUN, RUN), RUN)
        return pltpu.make_async_copy(ybuf_ref.at[sl, pl.ds(j * RUN, RUN)], yb_ref.at[dst], ssem.at[sl])

    def zero_copy(rows):
        return pltpu.make_async_copy(zero_ref.at[pl.ds(0, rows.size)], yb_ref.at[rows], zsem)

    def zero_unwritten(fn):
        for half in range(2):
            fn(zero_copy(pl.ds(n_tiles * ns + half * MOE_ROWS, MOE_ROWS)))

        def per_tile(i, carry):
            n = tail_ref[i]
            first = (i + 1) * ns - n * RUN
            _binary_pieces(n, lambda off, size: fn(zero_copy(
                pl.ds(pl.multiple_of(first + off * RUN, RUN), size * RUN))))
            return carry
        lax.fori_loop(0, n_tiles, per_tile, 0)

    @pl.when(b == 0)
    def _():
        zero_ref[...] = jnp.zeros_like(zero_ref)
        zero_unwritten(lambda cp: cp.start())
        zero_unwritten(lambda cp: cp.wait())
        for j in range(BLOCK_PIECES):
            gather(tab_ref, 0, j).start(priority=j % 2)

    @pl.when(b + 1 < n_used)
    def _():
        for j in range(BLOCK_PIECES):
            gather(next_tab_ref, 1 - slot, j).start(priority=j % 2)

    new_expert = jnp.logical_or(b == 0, bexp_ref[b] != bexp_ref[jnp.maximum(b - 1, 0)])

    @pl.when(new_expert)
    def _():
        wgb_ref[...] = wg_ref[...].astype(BF16)
        wub_ref[...] = wu_ref[...].astype(BF16)
        wdb_ref[...] = wd_ref[...].astype(BF16)

    @pl.when(b < n_used)
    def _():
        for j in range(BLOCK_PIECES):
            gather(tab_ref, slot, j).wait()

        @pl.when(b >= 2)
        def _():
            for j in range(BLOCK_PIECES):
                scatter(tab_ref, slot, j).wait()

        xb = xbuf_ref[slot].astype(BF16)
        g = jnp.dot(xb, wgb_ref[...], preferred_element_type=F32) + bg_ref[...]
        u = jnp.dot(xb, wub_ref[...], preferred_element_type=F32) + bu_ref[...]
        g = jnp.minimum(g, SWIGLU_LIMIT)
        u = jnp.clip(u, -SWIGLU_LIMIT, SWIGLU_LIMIT)
        act = (u + 1.0) * g * jax.nn.sigmoid(SWIGLU_ALPHA * g)
        ybuf_ref[slot] = (jnp.dot(act.astype(BF16), wdb_ref[...], preferred_element_type=F32)
                          + bd_ref[...])
        for j in range(BLOCK_PIECES):
            scatter(tab_ref, slot, j).start(priority=j % 2)

    @pl.when(b == n_used - 1)
    def _():
        for j in range(BLOCK_PIECES):
            scatter(tab_ref, slot, j).wait()

        @pl.when(b >= 1)
        def _():
            for j in range(BLOCK_PIECES):
                scatter(tab_ref, 1 - slot, j).wait()


def _experts(block_exp, n_used, tail, tab, xs, wg, bg, wu, bu, wd, bd, ns):
    d, de = wg.shape[1], wg.shape[2]
    n_blocks = tab.shape[0]
    n_tiles = xs.shape[0] // ns
    assert tab.shape[2] == 2 * BLOCK_PIECES == V7X_LANES
    wmap = lambda b, be, nu, tl: (be[b], 0, 0)
    smem_tab = lambda f: pl.BlockSpec((None, 1, V7X_LANES), f, memory_space=pltpu.SMEM)
    any_spec = pl.BlockSpec(memory_space=pl.ANY)
    return pl.pallas_call(
        functools.partial(_experts_kernel, ns=ns, n_tiles=n_tiles),
        grid_spec=pltpu.PrefetchScalarGridSpec(
            num_scalar_prefetch=3,
            grid=(n_blocks,),
            in_specs=[
                smem_tab(lambda b, be, nu, tl: (b, 0, 0)),
                smem_tab(lambda b, be, nu, tl: (jnp.minimum(b + 1, n_blocks - 1), 0, 0)),
                any_spec,
                pl.BlockSpec((None, d, de), wmap), pl.BlockSpec((None, 1, de), wmap),
                pl.BlockSpec((None, d, de), wmap), pl.BlockSpec((None, 1, de), wmap),
                pl.BlockSpec((None, de, d), wmap), pl.BlockSpec((None, 1, d), wmap),
            ],
            out_specs=any_spec,
            scratch_shapes=[
                pltpu.VMEM((d, de), BF16), pltpu.VMEM((d, de), BF16), pltpu.VMEM((de, d), BF16),
                pltpu.VMEM((2, MOE_ROWS, d), F32), pltpu.VMEM((2, MOE_ROWS, d), F32),
                pltpu.VMEM((MOE_ROWS, d), F32),
                pltpu.SemaphoreType.DMA((2,)), pltpu.SemaphoreType.DMA((2,)), pltpu.SemaphoreType.DMA,
            ],
        ),
        out_shape=jax.ShapeDtypeStruct((n_tiles * ns + 2 * MOE_ROWS, d), F32),
        compiler_params=_cparams("arbitrary"),
        name="moe_experts",
    )(block_exp, n_used, tail, tab, tab, xs, wg, bg, wu, bu, wd, bd)


def _combine_kernel(extra_ref, route_ref, x1_ref, yb_ref, o_ref, sorted_ref, sems, *, tm, ns):
    i = pl.program_id(0)
    slot = lax.rem(i, 2)
    base_rows = tm * TOP_K

    def copies(tile, sl, fn):
        def rows_copy(first, n_rows):
            src = pl.ds(pl.multiple_of(tile * ns + first, RUN), n_rows)
            dst = pl.ds(pl.multiple_of(first, RUN), n_rows)
            return pltpu.make_async_copy(yb_ref.at[src], sorted_ref.at[sl, dst], sems.at[sl])

        fn(rows_copy(0, base_rows))
        _binary_pieces(extra_ref[tile],
                       lambda off, size: fn(rows_copy(base_rows + off * RUN, size * RUN)))

    @pl.when(i == 0)
    def _():
        sorted_ref[...] = jnp.zeros_like(sorted_ref)
        copies(i, slot, lambda cp: cp.start())

    @pl.when(i + 1 < pl.num_programs(0))
    def _():
        copies(i + 1, 1 - slot, lambda cp: cp.start())

    copies(i, slot, lambda cp: cp.wait())

    ys = sorted_ref[slot].astype(BF16)
    route = route_ref[...]
    q = lax.broadcasted_iota(jnp.int32, (tm, ns), 1).astype(F32)
    wmat = jnp.zeros((tm, ns), F32)
    for k in range(TOP_K):
        wmat = wmat + jnp.where(q == route[:, k:k + 1], route[:, TOP_K + k:TOP_K + k + 1], 0.0)
    o_ref[...] = x1_ref[...] + jnp.dot(wmat.astype(BF16), ys, preferred_element_type=F32)


def _combine(extra, route, x1, yb, tm, ns):
    t, d = x1.shape
    return pl.pallas_call(
        functools.partial(_combine_kernel, tm=tm, ns=ns),
        grid=(t // tm,),
        in_specs=[
            pl.BlockSpec(memory_space=pltpu.SMEM),
            pl.BlockSpec((tm, V7X_LANES), lambda i: (i, 0)),
            pl.BlockSpec((tm, d), lambda i: (i, 0)),
            pl.BlockSpec(memory_space=pl.ANY),
        ],
        out_specs=pl.BlockSpec((tm, d), lambda i: (i, 0)),
        out_shape=jax.ShapeDtypeStruct((t, d), F32),
        scratch_shapes=[pltpu.VMEM((2, ns, d), F32), pltpu.SemaphoreType.DMA((2,))],
        compiler_params=_cparams("arbitrary"),
        name="moe_combine",
    )(extra, route, x1, yb)


def _head_indicator(width, head_dim):
    lane_head = jnp.arange(width) // head_dim
    return (lane_head[:, None] == jnp.arange(V7X_LANES)[None, :]).astype(BF16)


def _layer(x, mem, norm_mix, w_in, a_q_gain, a_k_gain, a_rel_bias, conv_w, conv_b, dt_bias, a_log,
           d_skip, ssm_norm, norm_mem, w_mem_kv, x_q_gain, x_k_gain, w_br_a, w_br_b, w_br_c, w_out,
           norm_ffn, w_router, b_router, w_gate, b_gate, w_up, b_up, w_down, b_down):
    b, s, d = x.shape
    t = b * s
    a_width = A_HEADS * A_HEAD_DIM
    inner = SSM_HEADS * SSM_HEAD_DIM
    gs = SSM_GROUPS * SSM_STATE
    x_width = X_HEADS * X_HEAD_DIM
    assert d == a_width == x_width and inner == 2 * d and 2 * gs == d

    o_dt = 3 * a_width + inner + inner + 2 * gs
    assert o_dt % d == 0
    w_main = jnp.concatenate([w_in[:, :o_dt], w_in[:, o_dt + SSM_HEADS:]], axis=1).astype(BF16)
    roles = ("qa", "ka", "plain", "plain", "plain", "plain", "plain", "plain", "qx", "sig", "sig", "sig")
    w_dt = jnp.pad(w_in[:, o_dt:o_dt + SSM_HEADS], ((0, 0), (0, V7X_LANES - SSM_HEADS))).astype(BF16)
    gains = jnp.zeros((V7X_SUBLANES, d), F32)
    gains = gains.at[0].set(jnp.tile(a_q_gain, A_HEADS) * (A_HEAD_DIM ** -0.5 * LOG2_E))
    gains = gains.at[1].set(jnp.tile(a_k_gain, A_HEADS))
    gains = gains.at[2].set(jnp.tile(x_q_gain, X_HEADS) * X_HEAD_DIM ** -0.5)

    x2d = x.reshape(t, d)
    proj, dt_raw = _in_proj(x2d, norm_mix.reshape(1, d), w_main, w_dt, gains, roles,
                            tm=min(1024, t), tn=d)
    proj3 = proj.reshape(b, s, proj.shape[1])

    y_a = _attention(proj3, _attn_bias(a_rel_bias), q_tile=0, k_tile=1, v_tile=2)

    pad_h = lambda v: jnp.pad(v.astype(F32), (0, V7X_LANES - SSM_HEADS)).reshape(1, V7X_LANES)
    e_mat = _head_indicator(inner, SSM_HEAD_DIM).T
    y_b = _ssd(proj3, dt_raw.reshape(b, s, V7X_LANES),
               conv_w[:, :inner], conv_b[:inner].reshape(1, inner),
               conv_w[:, inner:], conv_b[inner:].reshape(1, 2 * gs),
               pad_h(dt_bias), pad_h(-jnp.exp(a_log.astype(F32))),
               jnp.repeat(d_skip.astype(F32), SSM_HEAD_DIM).reshape(1, inner),
               ssm_norm.reshape(1, inner), e_mat, z_tile=3, x_tile=5, bc_tile=7)

    k_mem, v_mem = _mem_kv(mem, norm_mem.reshape(1, d), w_mem_kv.astype(BF16),
                           x_k_gain.reshape(1, X_HEAD_DIM))
    y_c = _mem_attn(proj3, k_mem, v_mem, q_blk=8, tq=min(512, s))

    w_r = jnp.pad(w_router, ((0, 0), (0, V7X_LANES - N_EXPERTS)))
    b_r = jnp.pad(b_router, (0, V7X_LANES - N_EXPERTS)).reshape(1, V7X_LANES)
    tm_moe = min(MOE_TILE, t)
    n_tiles = t // tm_moe
    x1, h2, route, tile_cnt = _merge(
        x2d, y_a.reshape(t, a_width), y_b.reshape(t, inner), y_c.reshape(t, x_width), proj,
        w_br_a.astype(BF16), w_br_b.astype(BF16), w_br_c.astype(BF16), w_out.astype(BF16),
        norm_ffn.reshape(1, d), w_r, b_r, gate_blk=9, tm=min(MERGE_TM, t), rt=tm_moe)

    ns = -(-(tm_moe * TOP_K + N_EXPERTS * (RUN - 1)) // V7X_LANES) * V7X_LANES
    units = ns // RUN
    n_blocks = -(-(t * TOP_K + n_tiles * N_EXPERTS * (RUN - 1) + N_EXPERTS * (MOE_ROWS - 1))
                 // MOE_ROWS)
    n8 = tile_cnt[:, 0, :N_EXPERTS].astype(jnp.int32) // RUN
    total = jnp.sum(n8, axis=0)
    padded = (total + BLOCK_PIECES - 1) // BLOCK_PIECES * BLOCK_PIECES
    pad_ends = jnp.cumsum(padded)
    pad_starts = pad_ends - padded
    n_used = (pad_ends[-1] // BLOCK_PIECES).reshape(1).astype(jnp.int32)
    blk = jnp.minimum(jnp.arange(n_blocks, dtype=jnp.int32), n_used[0] - 1)
    block_exp = jnp.minimum(jnp.sum(pad_ends[None, :] <= (blk * BLOCK_PIECES)[:, None], axis=1),
                            N_EXPERTS - 1).astype(jnp.int32)
    slot_j = jnp.arange(BLOCK_PIECES, dtype=jnp.int32)[None, :]
    onehot_e = (block_exp[:, None] == jnp.arange(N_EXPERTS)[None, :]).astype(jnp.int32)
    q = blk[:, None] * BLOCK_PIECES + slot_j - (onehot_e @ pad_starts)[:, None]
    real = q < (onehot_e @ total)[:, None]
    ends_b = onehot_e @ jnp.cumsum(n8, axis=0).T
    tile_of = jnp.minimum(jnp.sum(ends_b[:, None, :] <= q[:, :, None], axis=-1), n_tiles - 1)
    tile_1h = (tile_of[:, :, None] == jnp.arange(n_tiles)[None, None, :]).astype(jnp.int32)
    starts_b = ends_b - onehot_e @ n8.T
    in_tile_b = onehot_e @ (jnp.cumsum(n8, axis=1) - n8).T
    piece = (tile_of * units + jnp.sum(tile_1h * (in_tile_b - starts_b)[:, None, :], axis=-1) + q)
    zero_piece = (tm_moe * TOP_K + N_EXPERTS * (RUN - 1)) // RUN
    spare = n_tiles * units + (blk % 2)[:, None] * BLOCK_PIECES + slot_j
    tab = jnp.concatenate([jnp.where(real, piece, zero_piece), jnp.where(real, piece, spare)], axis=1)
    tab = tab.reshape(n_blocks, 1, 2 * BLOCK_PIECES).astype(jnp.int32)
    used = jnp.sum(n8, axis=1)
    tail = (units - used).astype(jnp.int32)
    extra = (used - tm_moe * TOP_K // RUN).astype(jnp.int32)

    xs = _dispatch(h2, route, tm_moe, ns)
    yb = _experts(block_exp, n_used, tail, tab, xs,
                  w_gate, b_gate.reshape(N_EXPERTS, 1, -1),
                  w_up, b_up.reshape(N_EXPERTS, 1, -1),
                  w_down, b_down.reshape(N_EXPERTS, 1, -1), ns)
    out = _combine(extra, route, x1, yb, tm_moe, ns)
    return out.reshape(b, s, d)


def kernel(x, mem, norm_mix, w_in, a_q_gain, a_k_gain, a_rel_bias, conv_w, conv_b, dt_bias, a_log, d_skip, ssm_norm, norm_mem, w_mem_kv, x_q_gain, x_k_gain, w_br_a, w_br_b, w_br_c, w_out, norm_ffn, w_router, b_router, w_gate, b_gate, w_up, b_up, w_down, b_down):
    for l in range(norm_mix.shape[0]):
        x = _layer(x, mem, norm_mix[l], w_in[l], a_q_gain[l], a_k_gain[l], a_rel_bias[l], conv_w[l],
                   conv_b[l], dt_bias[l], a_log[l], d_skip[l], ssm_norm[l], norm_mem[l], w_mem_kv[l],
                   x_q_gain[l], x_k_gain[l], w_br_a[l], w_br_b[l], w_br_c[l], w_out[l], norm_ffn[l],
                   w_router[l], b_router[l], w_gate[l], b_gate[l], w_up[l], b_up[l], w_down[l],
                   b_down[l])
    return x
```

```python
import functools

import jax
import jax.numpy as jnp
from jax import lax
from jax.experimental import pallas as pl
from jax.experimental.pallas import tpu as pltpu

F32 = jnp.float32
BF16 = jnp.bfloat16
HIGHEST = lax.Precision.HIGHEST

V7X_LANES = 128
V7X_SUBLANES = 8
V7X_VMEM_LIMIT_BYTES = 56 * 1024 * 1024

EPS = 1e-6
LOG2_E = 1.4426950408889634
NEG = -1e30

CHUNK = 64
A_HEADS = 16
A_HEAD_DIM = 64
LEFT_CHUNKS = 8
REL_CLIP = 128
SSM_HEADS = 32
SSM_HEAD_DIM = 64
SSM_GROUPS = 4
SSM_STATE = 128
CONV_WIDTH = 4
X_HEADS = 4
X_HEAD_DIM = 256
N_EXPERTS = 32
TOP_K = 4
SWIGLU_LIMIT = 7.0
SWIGLU_ALPHA = 1.702

ATTN_TQ = 256
SSD_L = 256
MOE_ROWS = 512
MOE_TILE = 256
MERGE_TM = 512


def _cparams(*sem):
    return pltpu.CompilerParams(dimension_semantics=sem, vmem_limit_bytes=V7X_VMEM_LIMIT_BYTES)


def _split_bf16(v):
    hi = v.astype(BF16)
    lo = (v - hi.astype(F32)).astype(BF16)
    return hi, lo


def _in_proj_kernel(x_ref, nw_ref, w_ref, wdt_ref, gains_ref, o_ref, dt_ref, h_ref, *, roles):
    j = pl.program_id(1)

    @pl.when(j == 0)
    def _():
        x = x_ref[...]
        ms = jnp.mean(x * x, axis=-1, keepdims=True)
        hb = (x * lax.rsqrt(ms + EPS) * nw_ref[...]).astype(BF16)
        h_ref[...] = hb
        dt_ref[...] = jnp.dot(hb, wdt_ref[...], preferred_element_type=F32)

    def head_norm(acc, head_dim, gain_row):
        outs = []
        if head_dim >= V7X_LANES:
            for c0 in range(0, acc.shape[1], head_dim):
                blk = acc[:, c0:c0 + head_dim]
                s = jnp.sum(blk * blk, axis=-1, keepdims=True)
                scale = lax.rsqrt(s * (1.0 / head_dim) + EPS)
                outs.append(blk * scale * gain_row[:, c0:c0 + head_dim])
        else:
            assert 2 * head_dim == V7X_LANES
            lo = lax.broadcasted_iota(jnp.int32, (1, V7X_LANES), 1) < head_dim
            for c0 in range(0, acc.shape[1], V7X_LANES):
                blk = acc[:, c0:c0 + V7X_LANES]
                sq = blk * blk
                s_lo = jnp.sum(jnp.where(lo, sq, 0.0), axis=-1, keepdims=True)
                s_hi = jnp.sum(jnp.where(lo, 0.0, sq), axis=-1, keepdims=True)
                scale = jnp.where(lo, lax.rsqrt(s_lo * (1.0 / head_dim) + EPS),
                                  lax.rsqrt(s_hi * (1.0 / head_dim) + EPS))
                outs.append(blk * scale * gain_row[:, c0:c0 + V7X_LANES])
        return jnp.concatenate(outs, axis=1)

    def cond_for(role):
        c = None
        for jj, r in enumerate(roles):
            if r == role:
                c = (j == jj) if c is None else jnp.logical_or(c, j == jj)
        return c

    for role in sorted(set(roles)):
        @pl.when(cond_for(role))
        def _(role=role):
            acc = jnp.dot(h_ref[...], w_ref[...], preferred_element_type=F32)
            if role == "qa":
                out = head_norm(acc, A_HEAD_DIM, gains_ref[0:1, :])
            elif role == "ka":
                out = head_norm(acc, A_HEAD_DIM, gains_ref[1:2, :])
            elif role == "qx":
                out = head_norm(acc, X_HEAD_DIM, gains_ref[2:3, :])
            elif role == "sig":
                out = 0.5 * jnp.tanh(0.5 * acc) + 0.5
            else:
                out = acc
            o_ref[...] = out.astype(o_ref.dtype)


def _in_proj(x2d, norm_w, w_main, w_dt, gains, roles, tm, tn):
    t, d = x2d.shape
    n = w_main.shape[1]
    assert t % tm == 0 and n % tn == 0 and len(roles) == n // tn
    const = lambda i, j: (0, 0)
    return pl.pallas_call(
        functools.partial(_in_proj_kernel, roles=roles),
        grid=(t // tm, n // tn),
        in_specs=[
            pl.BlockSpec((tm, d), lambda i, j: (i, 0)),
            pl.BlockSpec((1, d), const),
            pl.BlockSpec((d, tn), lambda i, j: (0, j)),
            pl.BlockSpec((d, V7X_LANES), const),
            pl.BlockSpec(gains.shape, const),
        ],
        out_specs=[
            pl.BlockSpec((tm, tn), lambda i, j: (i, j)),
            pl.BlockSpec((tm, V7X_LANES), lambda i, j: (i, 0)),
        ],
        out_shape=[
            jax.ShapeDtypeStruct((t, n), BF16),
            jax.ShapeDtypeStruct((t, V7X_LANES), F32),
        ],
        scratch_shapes=[pltpu.VMEM((tm, d), BF16)],
        compiler_params=_cparams("parallel", "arbitrary"),
        name="in_proj",
    )(x2d, norm_w, w_main, w_dt, gains)


def _attn_kernel(q_ref, *refs, tq, nprev):
    k_refs = refs[:nprev + 1]
    v_refs = refs[nprev + 1:2 * nprev + 2]
    bias_ref, o_ref = refs[2 * nprev + 2:]
    qb = pl.program_id(1)
    nk = (nprev + 1) * tq
    lane = lax.broadcasted_iota(jnp.int32, (1, V7X_LANES), 1)
    col = lax.broadcasted_iota(jnp.int32, (1, nk), 1)
    before_start = col < (nprev - qb) * tq
    for hp in range(A_HEADS // 2):
        ls = slice(hp * V7X_LANES, (hp + 1) * V7X_LANES)
        q2 = q_ref[:, ls]
        kk = jnp.concatenate([r[:, ls] for r in k_refs], axis=0)
        vv = jnp.concatenate([r[:, ls] for r in v_refs], axis=0)
        outs = []
        for hh in range(2):
            sel = (lane < A_HEAD_DIM) if hh == 0 else (lane >= A_HEAD_DIM)
            qm = jnp.where(sel, q2, jnp.zeros_like(q2))
            s = lax.dot_general(qm, kk, (((1,), (1,)), ((), ())), preferred_element_type=F32)
            s = jnp.where(before_start, NEG, s + bias_ref[2 * hp + hh])
            m = jnp.max(s, axis=-1, keepdims=True)
            p = jnp.exp2(s - m)
            l = jnp.sum(p, axis=-1, keepdims=True)
            o = jnp.dot(p.astype(BF16), vv, preferred_element_type=F32)
            outs.append(o / l)
        o_ref[:, ls] = jnp.where(lane < A_HEAD_DIM, outs[0], outs[1]).astype(o_ref.dtype)


def _attention(proj3, bias, q_tile, k_tile, v_tile):
    b, s, _ = proj3.shape
    tq = ATTN_TQ
    width = A_HEADS * A_HEAD_DIM
    left = LEFT_CHUNKS * CHUNK
    assert left % tq == 0 and s % tq == 0
    nprev = left // tq

    def kv_spec(tile, back):
        return pl.BlockSpec((None, tq, width), lambda bi, qi: (bi, jnp.maximum(qi - back, 0), tile))

    in_specs = [pl.BlockSpec((None, tq, width), lambda bi, qi: (bi, qi, q_tile))]
    in_specs += [kv_spec(k_tile, nprev - i) for i in range(nprev + 1)]
    in_specs += [kv_spec(v_tile, nprev - i) for i in range(nprev + 1)]
    in_specs += [pl.BlockSpec(bias.shape, lambda bi, qi: (0, 0, 0), pipeline_mode=pl.Buffered(1))]
    return pl.pallas_call(
        functools.partial(_attn_kernel, tq=tq, nprev=nprev),
        grid=(b, s // tq),
        in_specs=in_specs,
        out_specs=pl.BlockSpec((None, tq, width), lambda bi, qi: (bi, qi, 0)),
        out_shape=jax.ShapeDtypeStruct((b, s, width), BF16),
        compiler_params=_cparams("parallel", "parallel"),
        name="chunk_attn",
    )(proj3, *([proj3] * (2 * nprev + 2)), bias)


def _attn_bias_kernel(v_ref, o_ref, *, tq, nk):
    x = jnp.broadcast_to(v_ref[...], (tq, v_ref.shape[-1]))
    toeplitz = pltpu.roll(x, 0, 1, stride=1, stride_axis=0)[:, :nk]
    qc = lax.broadcasted_iota(jnp.int32, (tq, nk), 0) // CHUNK
    kc = lax.broadcasted_iota(jnp.int32, (tq, nk), 1) // CHUNK
    in_band = jnp.where(kc >= qc, kc - qc, LEFT_CHUNKS + 1) <= LEFT_CHUNKS
    o_ref[...] = jnp.where(in_band, toeplitz * LOG2_E, NEG)


def _attn_bias(rel_bias):
    tq = ATTN_TQ
    left = LEFT_CHUNKS * CHUNK
    nk = left + tq
    m_len = 1 << (tq + nk - 1).bit_length()
    m = jnp.arange(m_len)
    diff = jnp.where(m < nk, m, m - m_len)
    v = rel_bias[:, jnp.clip(left - diff, -REL_CLIP, REL_CLIP) + REL_CLIP].astype(F32)
    h = v.shape[0]
    return pl.pallas_call(
        functools.partial(_attn_bias_kernel, tq=tq, nk=nk),
        grid=(h,),
        in_specs=[pl.BlockSpec((None, 1, m_len), lambda i: (i, 0, 0))],
        out_specs=pl.BlockSpec((None, tq, nk), lambda i: (i, 0, 0)),
        out_shape=jax.ShapeDtypeStruct((h, tq, nk), F32),
        compiler_params=_cparams("parallel"),
        name="attn_bias",
    )(v.reshape(h, 1, m_len))


def _ssd_kernel(z0_ref, z1_ref, x0_ref, x1_ref, bc_ref, dt_ref, cwx_ref, cbx_ref, cwbc_ref, cbbc_ref,
                dtb_ref, aneg_ref, dskip_ref, gain_ref, e_ref, o_ref,
                xf_ref, bcf_ref, st_ref, xs_ref, y_ref, *, L):
    c = pl.program_id(1)
    inner = SSM_HEADS * SSM_HEAD_DIM
    gw = inner // SSM_GROUPS
    gs = SSM_GROUPS * SSM_STATE
    tail = V7X_SUBLANES
    half = inner // 2

    @pl.when(c == 0)
    def _():
        xf_ref[0:tail, :] = jnp.zeros((tail, inner), F32)
        bcf_ref[0:tail, :] = jnp.zeros((tail, 2 * gs), F32)
        st_ref[...] = jnp.zeros_like(st_ref)

    xf_ref[tail:, 0:half] = x0_ref[...].astype(F32)
    xf_ref[tail:, half:inner] = x1_ref[...].astype(F32)
    bcf_ref[tail:, :] = bc_ref[...].astype(F32)

    def conv_silu(src_ref, w_ref, b_ref, c0, c1):
        acc = b_ref[:, c0:c1] + w_ref[CONV_WIDTH - 1:CONV_WIDTH, c0:c1] * src_ref[tail:tail + L, c0:c1]
        for k in range(1, CONV_WIDTH):
            acc = acc + (w_ref[CONV_WIDTH - 1 - k:CONV_WIDTH - k, c0:c1]
                         * src_ref[tail - k:tail - k + L, c0:c1])
        return acc * jax.nn.sigmoid(acc)

    for g in range(SSM_GROUPS):
        xs_ref[:, g * gw:(g + 1) * gw] = conv_silu(xf_ref, cwx_ref, cbx_ref, g * gw, (g + 1) * gw)
    bmat = conv_silu(bcf_ref, cwbc_ref, cbbc_ref, 0, gs)
    cmat = conv_silu(bcf_ref, cwbc_ref, cbbc_ref, gs, 2 * gs)
    xf_ref[0:tail, :] = xf_ref[L:L + tail, :]
    bcf_ref[0:tail, :] = bcf_ref[L:L + tail, :]

    pre = dt_ref[...] + dtb_ref[...]
    dt = jnp.maximum(pre, 0.0) + jnp.log1p(jnp.exp(-jnp.abs(pre)))
    a = dt * aneg_ref[...]
    row = lax.broadcasted_iota(jnp.int32, (L, L), 0)
    colm = lax.broadcasted_iota(jnp.int32, (L, L), 1)
    lower = colm <= row
    tri = jnp.where(lower, 1.0, 0.0).astype(F32)
    cs = jnp.dot(tri, a, precision=HIGHEST, preferred_element_type=F32)
    cs2 = cs * LOG2_E
    src_t = (cs2 - jnp.log2(dt)).T
    cs_last = cs[L - 1:L, :]
    w_state = dt * jnp.exp(cs_last - cs)
    e_cs = jnp.exp(cs)
    chunk_decay = jnp.broadcast_to(jnp.exp(cs_last), (tail, V7X_LANES))
    stacked = jnp.concatenate([w_state, e_cs, chunk_decay], axis=0)
    s_hi, s_lo = _split_bf16(stacked)
    expanded = (jnp.dot(s_hi, e_ref[...], preferred_element_type=F32)
                + jnp.dot(s_lo, e_ref[...], preferred_element_type=F32))
    w_state_e = expanded[0:L]
    e_cs_e = expanded[L:2 * L]
    decay_e = expanded[2 * L:2 * L + 1]

    lane = lax.broadcasted_iota(jnp.int32, (1, V7X_LANES), 1)
    pairs_per_group = gw // V7X_LANES
    for g in range(SSM_GROUPS):
        bg = bmat[:, g * SSM_STATE:(g + 1) * SSM_STATE]
        cg = cmat[:, g * SSM_STATE:(g + 1) * SSM_STATE].astype(BF16)
        cb = lax.dot_general(cg, bg.astype(BF16), (((1,), (1,)), ((), ())),
                             preferred_element_type=F32)
        state_b = st_ref[g].astype(BF16)
        y_off = jnp.dot(cg, state_b, preferred_element_type=F32) * e_cs_e[:, g * gw:(g + 1) * gw]
        for pr in range(pairs_per_group):
            c0 = g * gw + pr * V7X_LANES
            xp = xs_ref[:, c0:c0 + V7X_LANES]
            xpb = xp.astype(BF16)
            acc = y_off[:, pr * V7X_LANES:(pr + 1) * V7X_LANES] + dskip_ref[:, c0:c0 + V7X_LANES] * xp
            for hh in range(2):
                h = c0 // SSM_HEAD_DIM + hh
                d = cs2[:, h:h + 1] - src_t[h:h + 1, :]
                m = cb * jnp.exp2(jnp.where(lower, d, NEG))
                sel = (lane < SSM_HEAD_DIM) if hh == 0 else (lane >= SSM_HEAD_DIM)
                xm = jnp.where(sel, xpb, jnp.zeros_like(xpb))
                acc = acc + jnp.dot(m.astype(BF16), xm, preferred_element_type=F32)
            y_ref[:, c0:c0 + V7X_LANES] = acc
        xw = (xs_ref[:, g * gw:(g + 1) * gw] * w_state_e[:, g * gw:(g + 1) * gw]).astype(BF16)
        new = jnp.dot(bg.T.astype(BF16), xw, preferred_element_type=F32)
        st_ref[g] = st_ref[g] * decay_e[:, g * gw:(g + 1) * gw] + new

    for g in range(SSM_GROUPS):
        sl = slice(g * gw, (g + 1) * gw)
        z_ref, z0 = (z0_ref, g * gw) if g * gw < half else (z1_ref, g * gw - half)
        zz = z_ref[:, z0:z0 + gw].astype(F32)
        yz = y_ref[:, sl] * (zz * jax.nn.sigmoid(zz))
        ms = jnp.mean(yz * yz, axis=-1, keepdims=True)
        o_ref[:, sl] = (yz * lax.rsqrt(ms + EPS) * gain_ref[:, sl]).astype(o_ref.dtype)


def _ssd(proj3, dt3, cwx, cbx, cwbc, cbbc, dtb, aneg, dskip_e, gain, e_mat, z_tile, x_tile, bc_tile):
    b, s, _ = proj3.shape
    L = SSD_L
    assert s % L == 0
    inner = SSM_HEADS * SSM_HEAD_DIM
    gs2 = 2 * SSM_GROUPS * SSM_STATE
    assert gs2 == inner // 2
    const = lambda bi, ci: (0, 0)
    full = lambda a: pl.BlockSpec(a.shape, const)
    tile = lambda k: pl.BlockSpec((None, L, gs2), lambda bi, ci: (bi, ci, k))
    return pl.pallas_call(
        functools.partial(_ssd_kernel, L=L),
        grid=(b, s // L),
        in_specs=[
            tile(z_tile), tile(z_tile + 1), tile(x_tile), tile(x_tile + 1), tile(bc_tile),
            pl.BlockSpec((None, L, V7X_LANES), lambda bi, ci: (bi, ci, 0)),
            full(cwx), full(cbx), full(cwbc), full(cbbc), full(dtb), full(aneg), full(dskip_e),
            full(gain), full(e_mat),
        ],
        out_specs=pl.BlockSpec((None, L, inner), lambda bi, ci: (bi, ci, 0)),
        out_shape=jax.ShapeDtypeStruct((b, s, inner), BF16),
        scratch_shapes=[
            pltpu.VMEM((L + V7X_SUBLANES, inner), F32),
            pltpu.VMEM((L + V7X_SUBLANES, gs2), F32),
            pltpu.VMEM((SSM_GROUPS, SSM_STATE, inner // SSM_GROUPS), F32),
            pltpu.VMEM((L, inner), F32),
            pltpu.VMEM((L, inner), F32),
        ],
        compiler_params=_cparams("parallel", "arbitrary"),
        name="ssd",
    )(proj3, proj3, proj3, proj3, proj3, dt3, cwx, cbx, cwbc, cbbc, dtb, aneg, dskip_e, gain, e_mat)


def _mem_kv_kernel(mem_ref, g_ref, w_ref, kg_ref, k_ref, v_ref):
    m = mem_ref[...]
    ms = jnp.mean(m * m, axis=-1, keepdims=True)
    mn = (m * lax.rsqrt(ms + EPS) * g_ref[...]).astype(BF16)
    kv = jnp.dot(mn, w_ref[...], preferred_element_type=F32)
    width = X_HEADS * X_HEAD_DIM
    for h in range(X_HEADS):
        sl = slice(h * X_HEAD_DIM, (h + 1) * X_HEAD_DIM)
        kh = kv[:, sl]
        r = lax.rsqrt(jnp.mean(kh * kh, axis=-1, keepdims=True) + EPS)
        k_ref[:, sl] = (kh * r * kg_ref[...]).astype(k_ref.dtype)
    v_ref[...] = kv[:, width:].astype(v_ref.dtype)


def _mem_kv(mem, norm_mem, w_kv, k_gain):
    b, m, d = mem.shape
    width = X_HEADS * X_HEAD_DIM
    const = lambda bi: (0, 0)
    return pl.pallas_call(
        _mem_kv_kernel,
        grid=(b,),
        in_specs=[
            pl.BlockSpec((None, m, d), lambda bi: (bi, 0, 0)),
            pl.BlockSpec((1, d), const),
            pl.BlockSpec((d, 2 * width), const),
            pl.BlockSpec((1, X_HEAD_DIM), const),
        ],
        out_specs=[pl.BlockSpec((None, m, width), lambda bi: (bi, 0, 0))] * 2,
        out_shape=[jax.ShapeDtypeStruct((b, m, width), BF16)] * 2,
        compiler_params=_cparams("parallel"),
        name="mem_kv",
    )(mem, norm_mem, w_kv, k_gain)


def _mem_attn_kernel(q_ref, k_ref, v_ref, o_ref):
    for h in range(X_HEADS):
        sl = slice(h * X_HEAD_DIM, (h + 1) * X_HEAD_DIM)
        s = lax.dot_general(q_ref[:, sl], k_ref[:, sl], (((1,), (1,)), ((), ())),
                            preferred_element_type=F32)
        m = jnp.max(s, axis=-1, keepdims=True)
        p = jnp.exp(s - m)
        l = jnp.sum(p, axis=-1, keepdims=True)
        o = jnp.dot(p.astype(BF16), v_ref[:, sl], preferred_element_type=F32)
        o_ref[:, sl] = (o / l).astype(o_ref.dtype)


def _mem_attn(proj3, k, v, q_blk, tq):
    b, s, _ = proj3.shape
    m = k.shape[1]
    width = X_HEADS * X_HEAD_DIM
    assert s % tq == 0
    return pl.pallas_call(
        _mem_attn_kernel,
        grid=(b, s // tq),
        in_specs=[
            pl.BlockSpec((None, tq, width), lambda bi, qi: (bi, qi, q_blk)),
            pl.BlockSpec((None, m, width), lambda bi, qi: (bi, 0, 0)),
            pl.BlockSpec((None, m, width), lambda bi, qi: (bi, 0, 0)),
        ],
        out_specs=pl.BlockSpec((None, tq, width), lambda bi, qi: (bi, qi, 0)),
        out_shape=jax.ShapeDtypeStruct((b, s, width), BF16),
        compiler_params=_cparams("parallel", "parallel"),
        name="mem_attn",
    )(proj3, k, v)


def _merge_kernel(x_ref, ya_ref, yb_ref, yc_ref, g0_ref, g1_ref, g2_ref, wa_ref, wb_ref, wc_ref,
                  wo_ref, nf_ref, wrh_ref, wrl_ref, br_ref,
                  x1_ref, h2_ref, route_ref, cnt_ref, *, tm, rt):
    merged = (g0_ref[...].astype(F32) * jnp.dot(ya_ref[...], wa_ref[...], preferred_element_type=F32)
              + g1_ref[...].astype(F32) * jnp.dot(yb_ref[...], wb_ref[...], preferred_element_type=F32)
              + g2_ref[...].astype(F32) * jnp.dot(yc_ref[...], wc_ref[...], preferred_element_type=F32))
    x1 = x_ref[...] + jnp.dot(merged.astype(BF16), wo_ref[...], preferred_element_type=F32)
    x1_ref[...] = x1
    ms = jnp.mean(x1 * x1, axis=-1, keepdims=True)
    h2 = x1 * lax.rsqrt(ms + EPS) * nf_ref[...]
    h2_ref[...] = h2.astype(h2_ref.dtype)

    h_hi, h_lo = _split_bf16(h2)
    logits_all = (jnp.dot(h_hi, wrh_ref[...], preferred_element_type=F32)
                  + jnp.dot(h_lo, wrh_ref[...], preferred_element_type=F32)
                  + jnp.dot(h_hi, wrl_ref[...], preferred_element_type=F32)) + br_ref[...]
    for sub in range(tm // rt):
        _route_tile(logits_all[sub * rt:(sub + 1) * rt], route_ref.at[pl.ds(sub * rt, rt)],
                    cnt_ref.at[sub], rt)


def _route_tile(logits, route_ref, cnt_ref, tm):
    lane = lax.broadcasted_iota(jnp.int32, (tm, V7X_LANES), 1)
    lane_f = lane.astype(F32)
    work = jnp.where(lane < N_EXPERTS, logits, NEG)
    sel_val, sel_oh = [], []
    for _ in range(TOP_K):
        mval = jnp.max(work, axis=-1, keepdims=True)
        ik = jnp.min(jnp.where(work == mval, lane_f, float(V7X_LANES)), axis=-1, keepdims=True)
        oh = lane_f == ik
        work = jnp.where(oh, NEG, work)
        sel_val.append(mval)
        sel_oh.append(oh)
    ex = [jnp.exp(v - sel_val[0]) for v in sel_val]
    denom = ex[0] + ex[1] + ex[2] + ex[3]

    oh_all = jnp.zeros((tm, V7X_LANES), F32)
    for oh in sel_oh:
        oh_all = oh_all + jnp.where(oh, 1.0, 0.0)
    row = lax.broadcasted_iota(jnp.int32, (tm, tm), 0)
    colm = lax.broadcasted_iota(jnp.int32, (tm, tm), 1)
    strict = jnp.where(colm < row, 1.0, 0.0).astype(BF16)
    before = jnp.dot(strict, oh_all.astype(BF16), preferred_element_type=F32)
    cnt = jnp.sum(oh_all, axis=0, keepdims=True)
    cnt8 = jnp.floor((cnt + (V7X_SUBLANES - 1.0)) * (1.0 / V7X_SUBLANES)) * V7X_SUBLANES
    cnt8 = jnp.broadcast_to(cnt8, (V7X_SUBLANES, V7X_LANES))
    er = lax.broadcasted_iota(jnp.int32, (V7X_LANES, V7X_LANES), 0)
    ec = lax.broadcasted_iota(jnp.int32, (V7X_LANES, V7X_LANES), 1)
    earlier = jnp.where(er < ec, 1.0, 0.0).astype(BF16)
    run_start = jnp.dot(cnt8.astype(BF16), earlier, preferred_element_type=F32)[0:1, :]
    slot = before + run_start
    route = jnp.zeros((tm, V7X_LANES), F32)
    for k in range(TOP_K):
        pos = jnp.sum(jnp.where(sel_oh[k], slot, 0.0), axis=-1, keepdims=True)
        route = jnp.where(lane == k, pos, route)
        route = jnp.where(lane == TOP_K + k, ex[k] / denom, route)
    route_ref[...] = route
    cnt_ref[...] = cnt8


def _merge(x2d, ya, yb, yc, proj, wa, wb, wc, wo, nf, wr, br, gate_blk, tm, rt):
    t, d = x2d.shape
    assert t % tm == 0 and tm % rt == 0
    wr_hi, wr_lo = _split_bf16(wr)
    const = lambda i: (0, 0)
    full = lambda a: pl.BlockSpec(a.shape, const)
    rows = lambda w: pl.BlockSpec((tm, w), lambda i: (i, 0))
    return pl.pallas_call(
        functools.partial(_merge_kernel, tm=tm, rt=rt),
        grid=(t // tm,),
        in_specs=[
            rows(d), rows(ya.shape[1]), rows(yb.shape[1]), rows(yc.shape[1]),
            pl.BlockSpec((tm, d), lambda i: (i, gate_blk)),
            pl.BlockSpec((tm, d), lambda i: (i, gate_blk + 1)),
            pl.BlockSpec((tm, d), lambda i: (i, gate_blk + 2)),
            full(wa), full(wb), full(wc), full(wo), full(nf), full(wr_hi), full(wr_lo), full(br),
        ],
        out_specs=[rows(d), rows(d), rows(V7X_LANES),
                   pl.BlockSpec((tm // rt, V7X_SUBLANES, V7X_LANES), lambda i: (i, 0, 0))],
        out_shape=[
            jax.ShapeDtypeStruct((t, d), F32),
            jax.ShapeDtypeStruct((t, d), BF16),
            jax.ShapeDtypeStruct((t, V7X_LANES), F32),
            jax.ShapeDtypeStruct((t // rt, V7X_SUBLANES, V7X_LANES), F32),
        ],
        compiler_params=_cparams("parallel"),
        name="merge_route",
    )(x2d, ya, yb, yc, proj, proj, proj, wa, wb, wc, wo, nf, wr_hi, wr_lo, br)


RUN = V7X_SUBLANES
BLOCK_PIECES = MOE_ROWS // RUN
TAIL_SIZES = (32, 16, 8, 4, 2, 1)
assert MOE_TILE * TOP_K % V7X_LANES == 0 and N_EXPERTS * (RUN - 1) < 2 * TAIL_SIZES[0] * RUN


def _binary_pieces(n, body):
    for size in TAIL_SIZES:
        @pl.when((n & size) != 0)
        def _(size=size):
            body(n & ~(2 * size - 1), size)


def _dispatch_kernel(h_ref, route_ref, xs_ref, *, tm, ns):
    pos_t = route_ref[...].T
    q = lax.broadcasted_iota(jnp.int32, (ns, tm), 0).astype(F32)
    perm = jnp.zeros((ns, tm), F32)
    for k in range(TOP_K):
        perm = perm + jnp.where(q == pos_t[k:k + 1, :], 1.0, 0.0)
    xs_ref[...] = jnp.dot(perm.astype(BF16), h_ref[...], preferred_element_type=F32)


def _dispatch(h2, route, tm, ns):
    t, d = h2.shape
    return pl.pallas_call(
        functools.partial(_dispatch_kernel, tm=tm, ns=ns),
        grid=(t // tm,),
        in_specs=[
            pl.BlockSpec((tm, d), lambda i: (i, 0)),
            pl.BlockSpec((tm, V7X_LANES), lambda i: (i, 0)),
        ],
        out_specs=pl.BlockSpec((ns, d), lambda i: (i, 0)),
        out_shape=jax.ShapeDtypeStruct((t // tm * ns, d), F32),
        compiler_params=_cparams("parallel"),
        name="moe_dispatch",
    )(h2, route)


def _experts_kernel(bexp_ref, nused_ref, tail_ref, tab_ref, next_tab_ref, xs_ref, wg_ref, bg_ref,
                    wu_ref, bu_ref, wd_ref, bd_ref, yb_ref,
                    wgb_ref, wub_ref, wdb_ref, xbuf_ref, ybuf_ref, zero_ref, gsem, ssem, zsem, *,
                    ns, n_tiles):
    b = pl.program_id(0)
    n_used = nused_ref[0]
    slot = lax.rem(b, 2)

    def gather(t_ref, sl, j):
        src = pl.ds(pl.multiple_of(t_ref[0, j] * RUN, RUN), RUN)
        return pltpu.make_async_copy(xs_ref.at[src], xbuf_ref.at[sl, pl.ds(j * RUN, RUN)], gsem.at[sl])

    def scatter(t_ref, sl, j):
        dst = pl.ds(pl.multiple_of(t_ref[0, BLOCK_PIECES + j] * RUN, RUN), RUN)
        return pltpu.make_async_copy(ybuf_ref.at[sl, pl.ds(j * RUN, RUN)], yb_ref.at[dst], ssem.at[sl])

    def zero_copy(rows):
        return pltpu.make_async_copy(zero_ref.at[pl.ds(0, rows.size)], yb_ref.at[rows], zsem)

    def zero_unwritten(fn):
        for half in range(2):
            fn(zero_copy(pl.ds(n_tiles * ns + half * MOE_ROWS, MOE_ROWS)))

        def per_tile(i, carry):
            n = tail_ref[i]
            first = (i + 1) * ns - n * RUN
            _binary_pieces(n, lambda off, size: fn(zero_copy(
                pl.ds(pl.multiple_of(first + off * RUN, RUN), size * RUN))))
            return carry
        lax.fori_loop(0, n_tiles, per_tile, 0)

    @pl.when(b == 0)
    def _():
        zero_ref[...] = jnp.zeros_like(zero_ref)
        zero_unwritten(lambda cp: cp.start())
        zero_unwritten(lambda cp: cp.wait())
        for j in range(BLOCK_PIECES):
            gather(tab_ref, 0, j).start(priority=j % 2)

    @pl.when(b + 1 < n_used)
    def _():
        for j in range(BLOCK_PIECES):
            gather(next_tab_ref, 1 - slot, j).start(priority=j % 2)

    new_expert = jnp.logical_or(b == 0, bexp_ref[b] != bexp_ref[jnp.maximum(b - 1, 0)])

    @pl.when(new_expert)
    def _():
        wgb_ref[...] = wg_ref[...].astype(BF16)
        wub_ref[...] = wu_ref[...].astype(BF16)
        wdb_ref[...] = wd_ref[...].astype(BF16)

    @pl.when(b < n_used)
    def _():
        for j in range(BLOCK_PIECES):
            gather(tab_ref, slot, j).wait()

        @pl.when(b >= 2)
        def _():
            for j in range(BLOCK_PIECES):
                scatter(tab_ref, slot, j).wait()

        xb = xbuf_ref[slot].astype(BF16)
        g = jnp.dot(xb, wgb_ref[...], preferred_element_type=F32) + bg_ref[...]
        u = jnp.dot(xb, wub_ref[...], preferred_element_type=F32) + bu_ref[...]
        g = jnp.minimum(g, SWIGLU_LIMIT)
        u = jnp.clip(u, -SWIGLU_LIMIT, SWIGLU_LIMIT)
        act = (u + 1.0) * g * jax.nn.sigmoid(SWIGLU_ALPHA * g)
        ybuf_ref[slot] = (jnp.dot(act.astype(BF16), wdb_ref[...], preferred_element_type=F32)
                          + bd_ref[...])
        for j in range(BLOCK_PIECES):
            scatter(tab_ref, slot, j).start(priority=j % 2)

    @pl.when(b == n_used - 1)
    def _():
        for j in range(BLOCK_PIECES):
            scatter(tab_ref, slot, j).wait()

        @pl.when(b >= 1)
        def _():
            for j in range(BLOCK_PIECES):
                scatter(tab_ref, 1 - slot, j).wait()


def _experts(block_exp, n_used, tail, tab, xs, wg, bg, wu, bu, wd, bd, ns):
    d, de = wg.shape[1], wg.shape[2]
    n_blocks = tab.shape[0]
    n_tiles = xs.shape[0] // ns
    assert tab.shape[2] == 2 * BLOCK_PIECES == V7X_LANES
    wmap = lambda b, be, nu, tl: (be[b], 0, 0)
    smem_tab = lambda f: pl.BlockSpec((None, 1, V7X_LANES), f, memory_space=pltpu.SMEM)
    any_spec = pl.BlockSpec(memory_space=pl.ANY)
    return pl.pallas_call(
        functools.partial(_experts_kernel, ns=ns, n_tiles=n_tiles),
        grid_spec=pltpu.PrefetchScalarGridSpec(
            num_scalar_prefetch=3,
            grid=(n_blocks,),
            in_specs=[
                smem_tab(lambda b, be, nu, tl: (b, 0, 0)),
                smem_tab(lambda b, be, nu, tl: (jnp.minimum(b + 1, n_blocks - 1), 0, 0)),
                any_spec,
                pl.BlockSpec((None, d, de), wmap), pl.BlockSpec((None, 1, de), wmap),
                pl.BlockSpec((None, d, de), wmap), pl.BlockSpec((None, 1, de), wmap),
                pl.BlockSpec((None, de, d), wmap), pl.BlockSpec((None, 1, d), wmap),
            ],
            out_specs=any_spec,
            scratch_shapes=[
                pltpu.VMEM((d, de), BF16), pltpu.VMEM((d, de), BF16), pltpu.VMEM((de, d), BF16),
                pltpu.VMEM((2, MOE_ROWS, d), F32), pltpu.VMEM((2, MOE_ROWS, d), F32),
                pltpu.VMEM((MOE_ROWS, d), F32),
                pltpu.SemaphoreType.DMA((2,)), pltpu.SemaphoreType.DMA((2,)), pltpu.SemaphoreType.DMA,
            ],
        ),
        out_shape=jax.ShapeDtypeStruct((n_tiles * ns + 2 * MOE_ROWS, d), F32),
        compiler_params=_cparams("arbitrary"),
        name="moe_experts",
    )(block_exp, n_used, tail, tab, tab, xs, wg, bg, wu, bu, wd, bd)


def _combine_kernel(extra_ref, route_ref, x1_ref, yb_ref, o_ref, sorted_ref, sems, *, tm, ns):
    i = pl.program_id(0)
    slot = lax.rem(i, 2)
    base_rows = tm * TOP_K

    def copies(tile, sl, fn):
        def rows_copy(first, n_rows):
            src = pl.ds(pl.multiple_of(tile * ns + first, RUN), n_rows)
            dst = pl.ds(pl.multiple_of(first, RUN), n_rows)
            return pltpu.make_async_copy(yb_ref.at[src], sorted_ref.at[sl, dst], sems.at[sl])

        fn(rows_copy(0, base_rows))
        _binary_pieces(extra_ref[tile],
                       lambda off, size: fn(rows_copy(base_rows + off * RUN, size * RUN)))

    @pl.when(i == 0)
    def _():
        sorted_ref[...] = jnp.zeros_like(sorted_ref)
        copies(i, slot, lambda cp: cp.start())

    @pl.when(i + 1 < pl.num_programs(0))
    def _():
        copies(i + 1, 1 - slot, lambda cp: cp.start())

    copies(i, slot, lambda cp: cp.wait())

    ys = sorted_ref[slot].astype(BF16)
    route = route_ref[...]
    q = lax.broadcasted_iota(jnp.int32, (tm, ns), 1).astype(F32)
    wmat = jnp.zeros((tm, ns), F32)
    for k in range(TOP_K):
        wmat = wmat + jnp.where(q == route[:, k:k + 1], route[:, TOP_K + k:TOP_K + k + 1], 0.0)
    o_ref[...] = x1_ref[...] + jnp.dot(wmat.astype(BF16), ys, preferred_element_type=F32)


def _combine(extra, route, x1, yb, tm, ns):
    t, d = x1.shape
    return pl.pallas_call(
        functools.partial(_combine_kernel, tm=tm, ns=ns),
        grid=(t // tm,),
        in_specs=[
            pl.BlockSpec(memory_space=pltpu.SMEM),
            pl.BlockSpec((tm, V7X_LANES), lambda i: (i, 0)),
            pl.BlockSpec((tm, d), lambda i: (i, 0)),
            pl.BlockSpec(memory_space=pl.ANY),
        ],
        out_specs=pl.BlockSpec((tm, d), lambda i: (i, 0)),
        out_shape=jax.ShapeDtypeStruct((t, d), F32),
        scratch_shapes=[pltpu.VMEM((2, ns, d), F32), pltpu.SemaphoreType.DMA((2,))],
        compiler_params=_cparams("arbitrary"),
        name="moe_combine",
    )(extra, route, x1, yb)


def _head_indicator(width, head_dim):
    lane_head = jnp.arange(width) // head_dim
    return (lane_head[:, None] == jnp.arange(V7X_LANES)[None, :]).astype(BF16)


def _layer(x, mem, norm_mix, w_in, a_q_gain, a_k_gain, a_rel_bias, conv_w, conv_b, dt_bias, a_log,
           d_skip, ssm_norm, norm_mem, w_mem_kv, x_q_gain, x_k_gain, w_br_a, w_br_b, w_br_c, w_out,
           norm_ffn, w_router, b_router, w_gate, b_gate, w_up, b_up, w_down, b_down):
    b, s, d = x.shape
    t = b * s
    a_width = A_HEADS * A_HEAD_DIM
    inner = SSM_HEADS * SSM_HEAD_DIM
    gs = SSM_GROUPS * SSM_STATE
    x_width = X_HEADS * X_HEAD_DIM
    assert d == a_width == x_width and inner == 2 * d and 2 * gs == d

    o_dt = 3 * a_width + inner + inner + 2 * gs
    assert o_dt % d == 0
    w_main = jnp.concatenate([w_in[:, :o_dt], w_in[:, o_dt + SSM_HEADS:]], axis=1).astype(BF16)
    roles = ("qa", "ka", "plain", "plain", "plain", "plain", "plain", "plain", "qx", "sig", "sig", "sig")
    w_dt = jnp.pad(w_in[:, o_dt:o_dt + SSM_HEADS], ((0, 0), (0, V7X_LANES - SSM_HEADS))).astype(BF16)
    gains = jnp.zeros((V7X_SUBLANES, d), F32)
    gains = gains.at[0].set(jnp.tile(a_q_gain, A_HEADS) * (A_HEAD_DIM ** -0.5 * LOG2_E))
    gains = gains.at[1].set(jnp.tile(a_k_gain, A_HEADS))
    gains = gains.at[2].set(jnp.tile(x_q_gain, X_HEADS) * X_HEAD_DIM ** -0.5)

    x2d = x.reshape(t, d)
    proj, dt_raw = _in_proj(x2d, norm_mix.reshape(1, d), w_main, w_dt, gains, roles,
                            tm=min(1024, t), tn=d)
    proj3 = proj.reshape(b, s, proj.shape[1])

    y_a = _attention(proj3, _attn_bias(a_rel_bias), q_tile=0, k_tile=1, v_tile=2)

    pad_h = lambda v: jnp.pad(v.astype(F32), (0, V7X_LANES - SSM_HEADS)).reshape(1, V7X_LANES)
    e_mat = _head_indicator(inner, SSM_HEAD_DIM).T
    y_b = _ssd(proj3, dt_raw.reshape(b, s, V7X_LANES),
               conv_w[:, :inner], conv_b[:inner].reshape(1, inner),
               conv_w[:, inner:], conv_b[inner:].reshape(1, 2 * gs),
               pad_h(dt_bias), pad_h(-jnp.exp(a_log.astype(F32))),
               jnp.repeat(d_skip.astype(F32), SSM_HEAD_DIM).reshape(1, inner),
               ssm_norm.reshape(1, inner), e_mat, z_tile=3, x_tile=5, bc_tile=7)

    k_mem, v_mem = _mem_kv(mem, norm_mem.reshape(1, d), w_mem_kv.astype(BF16),
                           x_k_gain.reshape(1, X_HEAD_DIM))
    y_c = _mem_attn(proj3, k_mem, v_mem, q_blk=8, tq=min(512, s))

    w_r = jnp.pad(w_router, ((0, 0), (0, V7X_LANES - N_EXPERTS)))
    b_r = jnp.pad(b_router, (0, V7X_LANES - N_EXPERTS)).reshape(1, V7X_LANES)
    tm_moe = min(MOE_TILE, t)
    n_tiles = t // tm_moe
    x1, h2, route, tile_cnt = _merge(
        x2d, y_a.reshape(t, a_width), y_b.reshape(t, inner), y_c.reshape(t, x_width), proj,
        w_br_a.astype(BF16), w_br_b.astype(BF16), w_br_c.astype(BF16), w_out.astype(BF16),
        norm_ffn.reshape(1, d), w_r, b_r, gate_blk=9, tm=min(MERGE_TM, t), rt=tm_moe)

    ns = -(-(tm_moe * TOP_K + N_EXPERTS * (RUN - 1)) // V7X_LANES) * V7X_LANES
    units = ns // RUN
    n_blocks = -(-(t * TOP_K + n_tiles * N_EXPERTS * (RUN - 1) + N_EXPERTS * (MOE_ROWS - 1))
                 // MOE_ROWS)
    n8 = tile_cnt[:, 0, :N_EXPERTS].astype(jnp.int32) // RUN
    total = jnp.sum(n8, axis=0)
    padded = (total + BLOCK_PIECES - 1) // BLOCK_PIECES * BLOCK_PIECES
    pad_ends = jnp.cumsum(padded)
    pad_starts = pad_ends - padded
    n_used = (pad_ends[-1] // BLOCK_PIECES).reshape(1).astype(jnp.int32)
    blk = jnp.minimum(jnp.arange(n_blocks, dtype=jnp.int32), n_used[0] - 1)
    block_exp = jnp.minimum(jnp.sum(pad_ends[None, :] <= (blk * BLOCK_PIECES)[:, None], axis=1),
                            N_EXPERTS - 1).astype(jnp.int32)
    slot_j = jnp.arange(BLOCK_PIECES, dtype=jnp.int32)[None, :]
    onehot_e = (block_exp[:, None] == jnp.arange(N_EXPERTS)[None, :]).astype(jnp.int32)
    q = blk[:, None] * BLOCK_PIECES + slot_j - (onehot_e @ pad_starts)[:, None]
    real = q < (onehot_e @ total)[:, None]
    ends_b = onehot_e @ jnp.cumsum(n8, axis=0).T
    tile_of = jnp.minimum(jnp.sum(ends_b[:, None, :] <= q[:, :, None], axis=-1), n_tiles - 1)
    tile_1h = (tile_of[:, :, None] == jnp.arange(n_tiles)[None, None, :]).astype(jnp.int32)
    starts_b = ends_b - onehot_e @ n8.T
    in_tile_b = onehot_e @ (jnp.cumsum(n8, axis=1) - n8).T
    piece = (tile_of * units + jnp.sum(tile_1h * (in_tile_b - starts_b)[:, None, :], axis=-1) + q)
    zero_piece = (tm_moe * TOP_K + N_EXPERTS * (RUN - 1)) // RUN
    spare = n_tiles * units + (blk % 2)[:, None] * BLOCK_PIECES + slot_j
    tab = jnp.concatenate([jnp.where(real, piece, zero_piece), jnp.where(real, piece, spare)], axis=1)
    tab = tab.reshape(n_blocks, 1, 2 * BLOCK_PIECES).astype(jnp.int32)
    used = jnp.sum(n8, axis=1)
    tail = (units - used).astype(jnp.int32)
    extra = (used - tm_moe * TOP_K // RUN).astype(jnp.int32)

    xs = _dispatch(h2, route, tm_moe, ns)
    yb = _experts(block_exp, n_used, tail, tab, xs,
                  w_gate, b_gate.reshape(N_EXPERTS, 1, -1),
                  w_up, b_up.reshape(N_EXPERTS, 1, -1),
                  w_down, b_down.reshape(N_EXPERTS, 1, -1), ns)
    out = _combine(extra, route, x1, yb, tm_moe, ns)
    return out.reshape(b, s, d)


def kernel(x, mem, norm_mix, w_in, a_q_gain, a_k_gain, a_rel_bias, conv_w, conv_b, dt_bias, a_log, d_skip, ssm_norm, norm_mem, w_mem_kv, x_q_gain, x_k_gain, w_br_a, w_br_b, w_br_c, w_out, norm_ffn, w_router, b_router, w_gate, b_gate, w_up, b_up, w_down, b_down):
    for l in range(norm_mix.shape[0]):
        x = _layer(x, mem, norm_mix[l], w_in[l], a_q_gain[l], a_k_gain[l], a_rel_bias[l], conv_w[l],
                   conv_b[l], dt_bias[l], a_log[l], d_skip[l], ssm_norm[l], norm_mem[l], w_mem_kv[l],
                   x_q_gain[l], x_k_gain[l], w_br_a[l], w_br_b[l], w_br_c[l], w_out[l], norm_ffn[l],
                   w_router[l], b_router[l], w_gate[l], b_gate[l], w_up[l], b_up[l], w_down[l],
                   b_down[l])
    return x
```

```python
import functools

import jax
import jax.numpy as jnp
from jax import lax
from jax.experimental import pallas as pl
from jax.experimental.pallas import tpu as pltpu

F32 = jnp.float32
BF16 = jnp.bfloat16
HIGHEST = lax.Precision.HIGHEST

V7X_LANES = 128
V7X_SUBLANES = 8
V7X_VMEM_LIMIT_BYTES = 56 * 1024 * 1024

EPS = 1e-6
LOG2_E = 1.4426950408889634
NEG = -1e30

CHUNK = 64
A_HEADS = 16
A_HEAD_DIM = 64
LEFT_CHUNKS = 8
REL_CLIP = 128
SSM_HEADS = 32
SSM_HEAD_DIM = 64
SSM_GROUPS = 4
SSM_STATE = 128
CONV_WIDTH = 4
X_HEADS = 4
X_HEAD_DIM = 256
N_EXPERTS = 32
TOP_K = 4
SWIGLU_LIMIT = 7.0
SWIGLU_ALPHA = 1.702

ATTN_TQ = 256
SSD_L = 256
MOE_ROWS = 512
MOE_TILE = 256
MERGE_TM = 512


def _cparams(*sem):
    return pltpu.CompilerParams(dimension_semantics=sem, vmem_limit_bytes=V7X_VMEM_LIMIT_BYTES)


def _split_bf16(v):
    hi = v.astype(BF16)
    lo = (v - hi.astype(F32)).astype(BF16)
    return hi, lo


def _in_proj_kernel(x_ref, nw_ref, w_ref, wdt_ref, gains_ref, o_ref, dt_ref, h_ref, *, roles):
    j = pl.program_id(1)

    @pl.when(j == 0)
    def _():
        x = x_ref[...]
        ms = jnp.mean(x * x, axis=-1, keepdims=True)
        hb = (x * lax.rsqrt(ms + EPS) * nw_ref[...]).astype(BF16)
        h_ref[...] = hb
        dt_ref[...] = jnp.dot(hb, wdt_ref[...], preferred_element_type=F32)

    def head_norm(acc, head_dim, gain_row):
        outs = []
        if head_dim >= V7X_LANES:
            for c0 in range(0, acc.shape[1], head_dim):
                blk = acc[:, c0:c0 + head_dim]
                s = jnp.sum(blk * blk, axis=-1, keepdims=True)
                scale = lax.rsqrt(s * (1.0 / head_dim) + EPS)
                outs.append(blk * scale * gain_row[:, c0:c0 + head_dim])
        else:
            assert 2 * head_dim == V7X_LANES
            lo = lax.broadcasted_iota(jnp.int32, (1, V7X_LANES), 1) < head_dim
            for c0 in range(0, acc.shape[1], V7X_LANES):
                blk = acc[:, c0:c0 + V7X_LANES]
                sq = blk * blk
                s_lo = jnp.sum(jnp.where(lo, sq, 0.0), axis=-1, keepdims=True)
                s_hi = jnp.sum(jnp.where(lo, 0.0, sq), axis=-1, keepdims=True)
                scale = jnp.where(lo, lax.rsqrt(s_lo * (1.0 / head_dim) + EPS),
                                  lax.rsqrt(s_hi * (1.0 / head_dim) + EPS))
                outs.append(blk * scale * gain_row[:, c0:c0 + V7X_LANES])
        return jnp.concatenate(outs, axis=1)

    def cond_for(role):
        c = None
        for jj, r in enumerate(roles):
            if r == role:
                c = (j == jj) if c is None else jnp.logical_or(c, j == jj)
        return c

    def epilogue(acc, role):
        if role == "qa":
            return head_norm(acc, A_HEAD_DIM, gains_ref[0:1, :])
        if role == "ka":
            return head_norm(acc, A_HEAD_DIM, gains_ref[1:2, :])
        if role == "qx":
            return head_norm(acc, X_HEAD_DIM, gains_ref[2:3, :])
        if role == "sig":
            return 0.5 * jnp.tanh(0.5 * acc) + 0.5
        return acc

    d = x_ref.shape[1]
    for step_roles in sorted(set(roles)):
        @pl.when(cond_for(step_roles))
        def _(step_roles=step_roles):
            acc = jnp.dot(h_ref[...], w_ref[...], preferred_element_type=F32)
            for k, role in enumerate(step_roles):
                cs = slice(k * d, (k + 1) * d)
                o_ref[:, cs] = epilogue(acc[:, cs], role).astype(o_ref.dtype)


def _in_proj(x2d, norm_w, w_main, w_dt, gains, roles, tm, tn):
    t, d = x2d.shape
    n = w_main.shape[1]
    assert t % tm == 0 and n % tn == 0 and len(roles) == n // tn
    assert all(len(r) * d == tn for r in roles)
    const = lambda i, j: (0, 0)
    return pl.pallas_call(
        functools.partial(_in_proj_kernel, roles=roles),
        grid=(t // tm, n // tn),
        in_specs=[
            pl.BlockSpec((tm, d), lambda i, j: (i, 0)),
            pl.BlockSpec((1, d), const),
            pl.BlockSpec((d, tn), lambda i, j: (0, j)),
            pl.BlockSpec((d, V7X_LANES), const),
            pl.BlockSpec(gains.shape, const),
        ],
        out_specs=[
            pl.BlockSpec((tm, tn), lambda i, j: (i, j)),
            pl.BlockSpec((tm, V7X_LANES), lambda i, j: (i, 0)),
        ],
        out_shape=[
            jax.ShapeDtypeStruct((t, n), BF16),
            jax.ShapeDtypeStruct((t, V7X_LANES), F32),
        ],
        scratch_shapes=[pltpu.VMEM((tm, d), BF16)],
        compiler_params=_cparams("parallel", "arbitrary"),
        name="in_proj",
    )(x2d, norm_w, w_main, w_dt, gains)


def _attn_kernel(q_ref, *refs, tq, nprev):
    k_refs = refs[:nprev + 1]
    v_refs = refs[nprev + 1:2 * nprev + 2]
    bias_ref, o_ref = refs[2 * nprev + 2:]
    qb = pl.program_id(1)
    nk = (nprev + 1) * tq
    lane = lax.broadcasted_iota(jnp.int32, (1, V7X_LANES), 1)
    col = lax.broadcasted_iota(jnp.int32, (1, nk), 1)
    before_start = col < (nprev - qb) * tq
    for hp in range(A_HEADS // 2):
        ls = slice(hp * V7X_LANES, (hp + 1) * V7X_LANES)
        q2 = q_ref[:, ls]
        kk = jnp.concatenate([r[:, ls] for r in k_refs], axis=0)
        vv = jnp.concatenate([r[:, ls] for r in v_refs], axis=0)
        outs = []
        for hh in range(2):
            sel = (lane < A_HEAD_DIM) if hh == 0 else (lane >= A_HEAD_DIM)
            qm = jnp.where(sel, q2, jnp.zeros_like(q2))
            s = lax.dot_general(qm, kk, (((1,), (1,)), ((), ())), preferred_element_type=F32)
            s = jnp.where(before_start, NEG, s + bias_ref[2 * hp + hh])
            m = jnp.max(s, axis=-1, keepdims=True)
            p = jnp.exp2(s - m)
            l = jnp.sum(p, axis=-1, keepdims=True)
            o = jnp.dot(p.astype(BF16), vv, preferred_element_type=F32)
            outs.append(o / l)
        o_ref[:, ls] = jnp.where(lane < A_HEAD_DIM, outs[0], outs[1]).astype(o_ref.dtype)


def _attention(proj3, bias, q_tile, k_tile, v_tile):
    b, s, _ = proj3.shape
    tq = ATTN_TQ
    width = A_HEADS * A_HEAD_DIM
    left = LEFT_CHUNKS * CHUNK
    assert left % tq == 0 and s % tq == 0
    nprev = left // tq

    def kv_spec(tile, back):
        return pl.BlockSpec((None, tq, width), lambda bi, qi: (bi, jnp.maximum(qi - back, 0), tile))

    in_specs = [pl.BlockSpec((None, tq, width), lambda bi, qi: (bi, qi, q_tile))]
    in_specs += [kv_spec(k_tile, nprev - i) for i in range(nprev + 1)]
    in_specs += [kv_spec(v_tile, nprev - i) for i in range(nprev + 1)]
    in_specs += [pl.BlockSpec(bias.shape, lambda bi, qi: (0, 0, 0), pipeline_mode=pl.Buffered(1))]
    return pl.pallas_call(
        functools.partial(_attn_kernel, tq=tq, nprev=nprev),
        grid=(b, s // tq),
        in_specs=in_specs,
        out_specs=pl.BlockSpec((None, tq, width), lambda bi, qi: (bi, qi, 0)),
        out_shape=jax.ShapeDtypeStruct((b, s, width), BF16),
        compiler_params=_cparams("parallel", "parallel"),
        name="chunk_attn",
    )(proj3, *([proj3] * (2 * nprev + 2)), bias)


def _attn_bias_kernel(v_ref, o_ref, *, tq, nk):
    x = jnp.broadcast_to(v_ref[...], (tq, v_ref.shape[-1]))
    toeplitz = pltpu.roll(x, 0, 1, stride=1, stride_axis=0)[:, :nk]
    qc = lax.broadcasted_iota(jnp.int32, (tq, nk), 0) // CHUNK
    kc = lax.broadcasted_iota(jnp.int32, (tq, nk), 1) // CHUNK
    in_band = jnp.where(kc >= qc, kc - qc, LEFT_CHUNKS + 1) <= LEFT_CHUNKS
    o_ref[...] = jnp.where(in_band, toeplitz * LOG2_E, NEG)


def _attn_bias(rel_bias):
    tq = ATTN_TQ
    left = LEFT_CHUNKS * CHUNK
    nk = left + tq
    m_len = 1 << (tq + nk - 1).bit_length()
    m = jnp.arange(m_len)
    diff = jnp.where(m < nk, m, m - m_len)
    v = rel_bias[:, jnp.clip(left - diff, -REL_CLIP, REL_CLIP) + REL_CLIP].astype(F32)
    h = v.shape[0]
    return pl.pallas_call(
        functools.partial(_attn_bias_kernel, tq=tq, nk=nk),
        grid=(h,),
        in_specs=[pl.BlockSpec((None, 1, m_len), lambda i: (i, 0, 0))],
        out_specs=pl.BlockSpec((None, tq, nk), lambda i: (i, 0, 0)),
        out_shape=jax.ShapeDtypeStruct((h, tq, nk), F32),
        compiler_params=_cparams("parallel"),
        name="attn_bias",
    )(v.reshape(h, 1, m_len))


def _ssd_kernel(z0_ref, z1_ref, x0_ref, x1_ref, bc_ref, dt_ref, cwx_ref, cbx_ref, cwbc_ref, cbbc_ref,
                dtb_ref, aneg_ref, dskip_ref, gain_ref, e_ref, o_ref,
                xf_ref, bcf_ref, st_ref, xs_ref, y_ref, *, L):
    c = pl.program_id(1)
    inner = SSM_HEADS * SSM_HEAD_DIM
    gw = inner // SSM_GROUPS
    gs = SSM_GROUPS * SSM_STATE
    tail = V7X_SUBLANES
    half = inner // 2

    @pl.when(c == 0)
    def _():
        xf_ref[0:tail, :] = jnp.zeros((tail, inner), F32)
        bcf_ref[0:tail, :] = jnp.zeros((tail, 2 * gs), F32)
        st_ref[...] = jnp.zeros_like(st_ref)

    xf_ref[tail:, 0:half] = x0_ref[...].astype(F32)
    xf_ref[tail:, half:inner] = x1_ref[...].astype(F32)
    bcf_ref[tail:, :] = bc_ref[...].astype(F32)

    def conv_silu(src_ref, w_ref, b_ref, c0, c1):
        acc = b_ref[:, c0:c1] + w_ref[CONV_WIDTH - 1:CONV_WIDTH, c0:c1] * src_ref[tail:tail + L, c0:c1]
        for k in range(1, CONV_WIDTH):
            acc = acc + (w_ref[CONV_WIDTH - 1 - k:CONV_WIDTH - k, c0:c1]
                         * src_ref[tail - k:tail - k + L, c0:c1])
        return acc * jax.nn.sigmoid(acc)

    for g in range(SSM_GROUPS):
        xs_ref[:, g * gw:(g + 1) * gw] = conv_silu(xf_ref, cwx_ref, cbx_ref, g * gw, (g + 1) * gw)
    bmat = conv_silu(bcf_ref, cwbc_ref, cbbc_ref, 0, gs)
    cmat = conv_silu(bcf_ref, cwbc_ref, cbbc_ref, gs, 2 * gs)
    xf_ref[0:tail, :] = xf_ref[L:L + tail, :]
    bcf_ref[0:tail, :] = bcf_ref[L:L + tail, :]

    pre = dt_ref[...] + dtb_ref[...]
    dt = jnp.maximum(pre, 0.0) + jnp.log1p(jnp.exp(-jnp.abs(pre)))
    a = dt * aneg_ref[...]
    row = lax.broadcasted_iota(jnp.int32, (L, L), 0)
    colm = lax.broadcasted_iota(jnp.int32, (L, L), 1)
    lower = colm <= row
    tri = jnp.where(lower, 1.0, 0.0).astype(F32)
    cs = jnp.dot(tri, a, precision=HIGHEST, preferred_element_type=F32)
    cs2 = cs * LOG2_E
    src_t = (cs2 - jnp.log2(dt)).T
    cs_last = cs[L - 1:L, :]
    w_state = dt * jnp.exp(cs_last - cs)
    e_cs = jnp.exp(cs)
    chunk_decay = jnp.broadcast_to(jnp.exp(cs_last), (tail, V7X_LANES))
    stacked = jnp.concatenate([w_state, e_cs, chunk_decay], axis=0)
    s_hi, s_lo = _split_bf16(stacked)
    expanded = (jnp.dot(s_hi, e_ref[...], preferred_element_type=F32)
                + jnp.dot(s_lo, e_ref[...], preferred_element_type=F32))
    w_state_e = expanded[0:L]
    e_cs_e = expanded[L:2 * L]
    decay_e = expanded[2 * L:2 * L + 1]

    lane = lax.broadcasted_iota(jnp.int32, (1, V7X_LANES), 1)
    pairs_per_group = gw // V7X_LANES
    for g in range(SSM_GROUPS):
        bg = bmat[:, g * SSM_STATE:(g + 1) * SSM_STATE]
        cg = cmat[:, g * SSM_STATE:(g + 1) * SSM_STATE].astype(BF16)
        cb = lax.dot_general(cg, bg.astype(BF16), (((1,), (1,)), ((), ())),
                             preferred_element_type=F32)
        state_b = st_ref[g].astype(BF16)
        y_off = jnp.dot(cg, state_b, preferred_element_type=F32) * e_cs_e[:, g * gw:(g + 1) * gw]
        for pr in range(pairs_per_group):
            c0 = g * gw + pr * V7X_LANES
            xp = xs_ref[:, c0:c0 + V7X_LANES]
            xpb = xp.astype(BF16)
            acc = y_off[:, pr * V7X_LANES:(pr + 1) * V7X_LANES] + dskip_ref[:, c0:c0 + V7X_LANES] * xp
            for hh in range(2):
                h = c0 // SSM_HEAD_DIM + hh
                d = cs2[:, h:h + 1] - src_t[h:h + 1, :]
                m = cb * jnp.exp2(jnp.where(lower, d, NEG))
                sel = (lane < SSM_HEAD_DIM) if hh == 0 else (lane >= SSM_HEAD_DIM)
                xm = jnp.where(sel, xpb, jnp.zeros_like(xpb))
                acc = acc + jnp.dot(m.astype(BF16), xm, preferred_element_type=F32)
            y_ref[:, c0:c0 + V7X_LANES] = acc
        xw = (xs_ref[:, g * gw:(g + 1) * gw] * w_state_e[:, g * gw:(g + 1) * gw]).astype(BF16)
        new = jnp.dot(bg.T.astype(BF16), xw, preferred_element_type=F32)
        st_ref[g] = st_ref[g] * decay_e[:, g * gw:(g + 1) * gw] + new

    for g in range(SSM_GROUPS):
        sl = slice(g * gw, (g + 1) * gw)
        z_ref, z0 = (z0_ref, g * gw) if g * gw < half else (z1_ref, g * gw - half)
        zz = z_ref[:, z0:z0 + gw].astype(F32)
        yz = y_ref[:, sl] * (zz * jax.nn.sigmoid(zz))
        ms = jnp.mean(yz * yz, axis=-1, keepdims=True)
        o_ref[:, sl] = (yz * lax.rsqrt(ms + EPS) * gain_ref[:, sl]).astype(o_ref.dtype)


def _ssd(proj3, dt3, cwx, cbx, cwbc, cbbc, dtb, aneg, dskip_e, gain, e_mat, z_tile, x_tile, bc_tile):
    b, s, _ = proj3.shape
    L = SSD_L
    assert s % L == 0
    inner = SSM_HEADS * SSM_HEAD_DIM
    gs2 = 2 * SSM_GROUPS * SSM_STATE
    assert gs2 == inner // 2
    const = lambda bi, ci: (0, 0)
    full = lambda a: pl.BlockSpec(a.shape, const)
    tile = lambda k: pl.BlockSpec((None, L, gs2), lambda bi, ci: (bi, ci, k))
    return pl.pallas_call(
        functools.partial(_ssd_kernel, L=L),
        grid=(b, s // L),
        in_specs=[
            tile(z_tile), tile(z_tile + 1), tile(x_tile), tile(x_tile + 1), tile(bc_tile),
            pl.BlockSpec((None, L, V7X_LANES), lambda bi, ci: (bi, ci, 0)),
            full(cwx), full(cbx), full(cwbc), full(cbbc), full(dtb), full(aneg), full(dskip_e),
            full(gain), full(e_mat),
        ],
        out_specs=pl.BlockSpec((None, L, inner), lambda bi, ci: (bi, ci, 0)),
        out_shape=jax.ShapeDtypeStruct((b, s, inner), BF16),
        scratch_shapes=[
            pltpu.VMEM((L + V7X_SUBLANES, inner), F32),
            pltpu.VMEM((L + V7X_SUBLANES, gs2), F32),
            pltpu.VMEM((SSM_GROUPS, SSM_STATE, inner // SSM_GROUPS), F32),
            pltpu.VMEM((L, inner), F32),
            pltpu.VMEM((L, inner), F32),
        ],
        compiler_params=_cparams("parallel", "arbitrary"),
        name="ssd",
    )(proj3, proj3, proj3, proj3, proj3, dt3, cwx, cbx, cwbc, cbbc, dtb, aneg, dskip_e, gain, e_mat)


def _mem_kv_kernel(mem_ref, g_ref, w_ref, kg_ref, k_ref, v_ref):
    m = mem_ref[...]
    ms = jnp.mean(m * m, axis=-1, keepdims=True)
    mn = (m * lax.rsqrt(ms + EPS) * g_ref[...]).astype(BF16)
    kv = jnp.dot(mn, w_ref[...], preferred_element_type=F32)
    width = X_HEADS * X_HEAD_DIM
    for h in range(X_HEADS):
        sl = slice(h * X_HEAD_DIM, (h + 1) * X_HEAD_DIM)
        kh = kv[:, sl]
        r = lax.rsqrt(jnp.mean(kh * kh, axis=-1, keepdims=True) + EPS)
        k_ref[:, sl] = (kh * r * kg_ref[...]).astype(k_ref.dtype)
    v_ref[...] = kv[:, width:].astype(v_ref.dtype)


def _mem_kv(mem, norm_mem, w_kv, k_gain):
    b, m, d = mem.shape
    width = X_HEADS * X_HEAD_DIM
    const = lambda bi: (0, 0)
    return pl.pallas_call(
        _mem_kv_kernel,
        grid=(b,),
        in_specs=[
            pl.BlockSpec((None, m, d), lambda bi: (bi, 0, 0)),
            pl.BlockSpec((1, d), const),
            pl.BlockSpec((d, 2 * width), const),
            pl.BlockSpec((1, X_HEAD_DIM), const),
        ],
        out_specs=[pl.BlockSpec((None, m, width), lambda bi: (bi, 0, 0))] * 2,
        out_shape=[jax.ShapeDtypeStruct((b, m, width), BF16)] * 2,
        compiler_params=_cparams("parallel"),
        name="mem_kv",
    )(mem, norm_mem, w_kv, k_gain)


def _mem_attn_kernel(q_ref, k_ref, v_ref, o_ref):
    for h in range(X_HEADS):
        sl = slice(h * X_HEAD_DIM, (h + 1) * X_HEAD_DIM)
        s = lax.dot_general(q_ref[:, sl], k_ref[:, sl], (((1,), (1,)), ((), ())),
                            preferred_element_type=F32)
        m = jnp.max(s, axis=-1, keepdims=True)
        p = jnp.exp(s - m)
        l = jnp.sum(p, axis=-1, keepdims=True)
        o = jnp.dot(p.astype(BF16), v_ref[:, sl], preferred_element_type=F32)
        o_ref[:, sl] = (o / l).astype(o_ref.dtype)


def _mem_attn(proj3, k, v, q_blk, tq):
    b, s, _ = proj3.shape
    m = k.shape[1]
    width = X_HEADS * X_HEAD_DIM
    assert s % tq == 0
    return pl.pallas_call(
        _mem_attn_kernel,
        grid=(b, s // tq),
        in_specs=[
            pl.BlockSpec((None, tq, width), lambda bi, qi: (bi, qi, q_blk)),
            pl.BlockSpec((None, m, width), lambda bi, qi: (bi, 0, 0)),
            pl.BlockSpec((None, m, width), lambda bi, qi: (bi, 0, 0)),
        ],
        out_specs=pl.BlockSpec((None, tq, width), lambda bi, qi: (bi, qi, 0)),
        out_shape=jax.ShapeDtypeStruct((b, s, width), BF16),
        compiler_params=_cparams("parallel", "parallel"),
        name="mem_attn",
    )(proj3, k, v)


def _merge_kernel(x_ref, ya_ref, yb_ref, yc_ref, g0_ref, g1_ref, g2_ref, wa_ref, wb_ref, wc_ref,
                  wo_ref, nf_ref, wrh_ref, wrl_ref, br_ref,
                  x1_ref, h2_ref, route_ref, cnt_ref, *, tm, rt):
    merged = (g0_ref[...].astype(F32) * jnp.dot(ya_ref[...], wa_ref[...], preferred_element_type=F32)
              + g1_ref[...].astype(F32) * jnp.dot(yb_ref[...], wb_ref[...], preferred_element_type=F32)
              + g2_ref[...].astype(F32) * jnp.dot(yc_ref[...], wc_ref[...], preferred_element_type=F32))
    x1 = x_ref[...] + jnp.dot(merged.astype(BF16), wo_ref[...], preferred_element_type=F32)
    x1_ref[...] = x1
    ms = jnp.mean(x1 * x1, axis=-1, keepdims=True)
    h2 = x1 * lax.rsqrt(ms + EPS) * nf_ref[...]
    h2_ref[...] = h2.astype(h2_ref.dtype)

    h_hi, h_lo = _split_bf16(h2)
    logits_all = (jnp.dot(h_hi, wrh_ref[...], preferred_element_type=F32)
                  + jnp.dot(h_lo, wrh_ref[...], preferred_element_type=F32)
                  + jnp.dot(h_hi, wrl_ref[...], preferred_element_type=F32)) + br_ref[...]
    for sub in range(tm // rt):
        _route_tile(logits_all[sub * rt:(sub + 1) * rt], route_ref.at[pl.ds(sub * rt, rt)],
                    cnt_ref.at[sub], rt)


def _route_tile(logits, route_ref, cnt_ref, tm):
    lane = lax.broadcasted_iota(jnp.int32, (tm, V7X_LANES), 1)
    lane_f = lane.astype(F32)
    work = jnp.where(lane < N_EXPERTS, logits, NEG)
    sel_val, sel_oh = [], []
    for _ in range(TOP_K):
        mval = jnp.max(work, axis=-1, keepdims=True)
        ik = jnp.min(jnp.where(work == mval, lane_f, float(V7X_LANES)), axis=-1, keepdims=True)
        oh = lane_f == ik
        work = jnp.where(oh, NEG, work)
        sel_val.append(mval)
        sel_oh.append(oh)
    ex = [jnp.exp(v - sel_val[0]) for v in sel_val]
    denom = ex[0] + ex[1] + ex[2] + ex[3]

    oh_all = jnp.zeros((tm, V7X_LANES), F32)
    for oh in sel_oh:
        oh_all = oh_all + jnp.where(oh, 1.0, 0.0)
    row = lax.broadcasted_iota(jnp.int32, (tm, tm), 0)
    colm = lax.broadcasted_iota(jnp.int32, (tm, tm), 1)
    strict = jnp.where(colm < row, 1.0, 0.0).astype(BF16)
    before = jnp.dot(strict, oh_all.astype(BF16), preferred_element_type=F32)
    cnt = jnp.sum(oh_all, axis=0, keepdims=True)
    cnt8 = jnp.floor((cnt + (V7X_SUBLANES - 1.0)) * (1.0 / V7X_SUBLANES)) * V7X_SUBLANES
    cnt8 = jnp.broadcast_to(cnt8, (V7X_SUBLANES, V7X_LANES))
    er = lax.broadcasted_iota(jnp.int32, (V7X_LANES, V7X_LANES), 0)
    ec = lax.broadcasted_iota(jnp.int32, (V7X_LANES, V7X_LANES), 1)
    earlier = jnp.where(er < ec, 1.0, 0.0).astype(BF16)
    run_start = jnp.dot(cnt8.astype(BF16), earlier, preferred_element_type=F32)[0:1, :]
    slot = before + run_start
    route = jnp.zeros((tm, V7X_LANES), F32)
    for k in range(TOP_K):
        pos = jnp.sum(jnp.where(sel_oh[k], slot, 0.0), axis=-1, keepdims=True)
        route = jnp.where(lane == k, pos, route)
        route = jnp.where(lane == TOP_K + k, ex[k] / denom, route)
    route_ref[...] = route
    cnt_ref[...] = cnt8


def _merge(x2d, ya, yb, yc, proj, wa, wb, wc, wo, nf, wr, br, gate_blk, tm, rt):
    t, d = x2d.shape
    assert t % tm == 0 and tm % rt == 0
    wr_hi, wr_lo = _split_bf16(wr)
    const = lambda i: (0, 0)
    full = lambda a: pl.BlockSpec(a.shape, const)
    rows = lambda w: pl.BlockSpec((tm, w), lambda i: (i, 0))
    return pl.pallas_call(
        functools.partial(_merge_kernel, tm=tm, rt=rt),
        grid=(t // tm,),
        in_specs=[
            rows(d), rows(ya.shape[1]), rows(yb.shape[1]), rows(yc.shape[1]),
            pl.BlockSpec((tm, d), lambda i: (i, gate_blk)),
            pl.BlockSpec((tm, d), lambda i: (i, gate_blk + 1)),
            pl.BlockSpec((tm, d), lambda i: (i, gate_blk + 2)),
            full(wa), full(wb), full(wc), full(wo), full(nf), full(wr_hi), full(wr_lo), full(br),
        ],
        out_specs=[rows(d), rows(d), rows(V7X_LANES),
                   pl.BlockSpec((tm // rt, V7X_SUBLANES, V7X_LANES), lambda i: (i, 0, 0))],
        out_shape=[
            jax.ShapeDtypeStruct((t, d), F32),
            jax.ShapeDtypeStruct((t, d), BF16),
            jax.ShapeDtypeStruct((t, V7X_LANES), F32),
            jax.ShapeDtypeStruct((t // rt, V7X_SUBLANES, V7X_LANES), F32),
        ],
        compiler_params=_cparams("parallel"),
        name="merge_route",
    )(x2d, ya, yb, yc, proj, proj, proj, wa, wb, wc, wo, nf, wr_hi, wr_lo, br)


RUN = V7X_SUBLANES
BLOCK_PIECES = MOE_ROWS // RUN
TAIL_SIZES = (32, 16, 8, 4, 2, 1)
assert MOE_TILE * TOP_K % V7X_LANES == 0 and N_EXPERTS * (RUN - 1) < 2 * TAIL_SIZES[0] * RUN


def _binary_pieces(n, body):
    for size in TAIL_SIZES:
        @pl.when((n & size) != 0)
        def _(size=size):
            body(n & ~(2 * size - 1), size)


def _dispatch_kernel(h_ref, route_ref, xs_ref, *, tm, ns):
    pos_t = route_ref[...].T
    q = lax.broadcasted_iota(jnp.int32, (ns, tm), 0).astype(F32)
    perm = jnp.zeros((ns, tm), F32)
    for k in range(TOP_K):
        perm = perm + jnp.where(q == pos_t[k:k + 1, :], 1.0, 0.0)
    xs_ref[...] = jnp.dot(perm.astype(BF16), h_ref[...], preferred_element_type=F32)


def _dispatch(h2, route, tm, ns):
    t, d = h2.shape
    return pl.pallas_call(
        functools.partial(_dispatch_kernel, tm=tm, ns=ns),
        grid=(t // tm,),
        in_specs=[
            pl.BlockSpec((tm, d), lambda i: (i, 0)),
            pl.BlockSpec((tm, V7X_LANES), lambda i: (i, 0)),
        ],
        out_specs=pl.BlockSpec((ns, d), lambda i: (i, 0)),
        out_shape=jax.ShapeDtypeStruct((t // tm * ns, d), F32),
        compiler_params=_cparams("parallel"),
        name="moe_dispatch",
    )(h2, route)


def _experts_kernel(bexp_ref, nused_ref, tail_ref, tab_ref, next_tab_ref, xs_ref, wg_ref, bg_ref,
                    wu_ref, bu_ref, wd_ref, bd_ref, yb_ref,
                    wgb_ref, wub_ref, wdb_ref, xbuf_ref, ybuf_ref, zero_ref, gsem, ssem, zsem, *,
                    ns, n_tiles):
    b = pl.program_id(0)
    n_used = nused_ref[0]
    slot = lax.rem(b, 2)

    def gather(t_ref, sl, j):
        src = pl.ds(pl.multiple_of(t_ref[0, j] * RUN, RUN), RUN)
        return pltpu.make_async_copy(xs_ref.at[src], xbuf_ref.at[sl, pl.ds(j * RUN, RUN)], gsem.at[sl])

    def scatter(t_ref, sl, j):
        dst = pl.ds(pl.multiple_of(t_ref[0, BLOCK_PIECES + j] * RUN, RUN), RUN)
        return pltpu.make_async_copy(ybuf_ref.at[sl, pl.ds(j * RUN, RUN)], yb_ref.at[dst], ssem.at[sl])

    def zero_copy(rows):
        return pltpu.make_async_copy(zero_ref.at[pl.ds(0, rows.size)], yb_ref.at[rows], zsem)

    def zero_unwritten(fn):
        for half in range(2):
            fn(zero_copy(pl.ds(n_tiles * ns + half * MOE_ROWS, MOE_ROWS)))

        def per_tile(i, carry):
            n = tail_ref[i]
            first = (i + 1) * ns - n * RUN
            _binary_pieces(n, lambda off, size: fn(zero_copy(
                pl.ds(pl.multiple_of(first + off * RUN, RUN), size * RUN))))
            return carry
        lax.fori_loop(0, n_tiles, per_tile, 0)

    @pl.when(b == 0)
    def _():
        zero_ref[...] = jnp.zeros_like(zero_ref)
        zero_unwritten(lambda cp: cp.start())
        zero_unwritten(lambda cp: cp.wait())
        for j in range(BLOCK_PIECES):
            gather(tab_ref, 0, j).start(priority=j % 2)

    @pl.when(b + 1 < n_used)
    def _():
        for j in range(BLOCK_PIECES):
            gather(next_tab_ref, 1 - slot, j).start(priority=j % 2)

    new_expert = jnp.logical_or(b == 0, bexp_ref[b] != bexp_ref[jnp.maximum(b - 1, 0)])

    @pl.when(new_expert)
    def _():
        wgb_ref[...] = wg_ref[...].astype(BF16)
        wub_ref[...] = wu_ref[...].astype(BF16)
        wdb_ref[...] = wd_ref[...].astype(BF16)

    @pl.when(b < n_used)
    def _():
        for j in range(BLOCK_PIECES):
            gather(tab_ref, slot, j).wait()

        @pl.when(b >= 2)
        def _():
            for j in range(BLOCK_PIECES):
                scatter(tab_ref, slot, j).wait()

        xb = xbuf_ref[slot].astype(BF16)
        g = jnp.dot(xb, wgb_ref[...], preferred_element_type=F32) + bg_ref[...]
        u = jnp.dot(xb, wub_ref[...], preferred_element_type=F32) + bu_ref[...]
        g = jnp.minimum(g, SWIGLU_LIMIT)
        u = jnp.clip(u, -SWIGLU_LIMIT, SWIGLU_LIMIT)
        act = (u + 1.0) * g * jax.nn.sigmoid(SWIGLU_ALPHA * g)
        ybuf_ref[slot] = (jnp.dot(act.astype(BF16), wdb_ref[...], preferred_element_type=F32)
                          + bd_ref[...])
        for j in range(BLOCK_PIECES):
            scatter(tab_ref, slot, j).start(priority=j % 2)

    @pl.when(b == n_used - 1)
    def _():
        for j in range(BLOCK_PIECES):
            scatter(tab_ref, slot, j).wait()

        @pl.when(b >= 1)
        def _():
            for j in range(BLOCK_PIECES):
                scatter(tab_ref, 1 - slot, j).wait()


def _experts(block_exp, n_used, tail, tab, xs, wg, bg, wu, bu, wd, bd, ns):
    d, de = wg.shape[1], wg.shape[2]
    n_blocks = tab.shape[0]
    n_tiles = xs.shape[0] // ns
    assert tab.shape[2] == 2 * BLOCK_PIECES == V7X_LANES
    wmap = lambda b, be, nu, tl: (be[b], 0, 0)
    smem_tab = lambda f: pl.BlockSpec((None, 1, V7X_LANES), f, memory_space=pltpu.SMEM)
    any_spec = pl.BlockSpec(memory_space=pl.ANY)
    return pl.pallas_call(
        functools.partial(_experts_kernel, ns=ns, n_tiles=n_tiles),
        grid_spec=pltpu.PrefetchScalarGridSpec(
            num_scalar_prefetch=3,
            grid=(n_blocks,),
            in_specs=[
                smem_tab(lambda b, be, nu, tl: (b, 0, 0)),
                smem_tab(lambda b, be, nu, tl: (jnp.minimum(b + 1, n_blocks - 1), 0, 0)),
                any_spec,
                pl.BlockSpec((None, d, de), wmap), pl.BlockSpec((None, 1, de), wmap),
                pl.BlockSpec((None, d, de), wmap), pl.BlockSpec((None, 1, de), wmap),
                pl.BlockSpec((None, de, d), wmap), pl.BlockSpec((None, 1, d), wmap),
            ],
            out_specs=any_spec,
            scratch_shapes=[
                pltpu.VMEM((d, de), BF16), pltpu.VMEM((d, de), BF16), pltpu.VMEM((de, d), BF16),
                pltpu.VMEM((2, MOE_ROWS, d), F32), pltpu.VMEM((2, MOE_ROWS, d), F32),
                pltpu.VMEM((MOE_ROWS, d), F32),
                pltpu.SemaphoreType.DMA((2,)), pltpu.SemaphoreType.DMA((2,)), pltpu.SemaphoreType.DMA,
            ],
        ),
        out_shape=jax.ShapeDtypeStruct((n_tiles * ns + 2 * MOE_ROWS, d), F32),
        compiler_params=_cparams("arbitrary"),
        name="moe_experts",
    )(block_exp, n_used, tail, tab, tab, xs, wg, bg, wu, bu, wd, bd)


def _combine_kernel(extra_ref, route_ref, x1_ref, yb_ref, o_ref, sorted_ref, sems, *, tm, ns):
    i = pl.program_id(0)
    slot = lax.rem(i, 2)
    base_rows = tm * TOP_K

    def copies(tile, sl, fn):
        def rows_copy(first, n_rows):
            src = pl.ds(pl.multiple_of(tile * ns + first, RUN), n_rows)
            dst = pl.ds(pl.multiple_of(first, RUN), n_rows)
            return pltpu.make_async_copy(yb_ref.at[src], sorted_ref.at[sl, dst], sems.at[sl])

        fn(rows_copy(0, base_rows))
        _binary_pieces(extra_ref[tile],
                       lambda off, size: fn(rows_copy(base_rows + off * RUN, size * RUN)))

    @pl.when(i == 0)
    def _():
        sorted_ref[...] = jnp.zeros_like(sorted_ref)
        copies(i, slot, lambda cp: cp.start())

    @pl.when(i + 1 < pl.num_programs(0))
    def _():
        copies(i + 1, 1 - slot, lambda cp: cp.start())

    copies(i, slot, lambda cp: cp.wait())

    ys = sorted_ref[slot].astype(BF16)
    route = route_ref[...]
    q = lax.broadcasted_iota(jnp.int32, (tm, ns), 1).astype(F32)
    wmat = jnp.zeros((tm, ns), F32)
    for k in range(TOP_K):
        wmat = wmat + jnp.where(q == route[:, k:k + 1], route[:, TOP_K + k:TOP_K + k + 1], 0.0)
    o_ref[...] = x1_ref[...] + jnp.dot(wmat.astype(BF16), ys, preferred_element_type=F32)


def _combine(extra, route, x1, yb, tm, ns):
    t, d = x1.shape
    return pl.pallas_call(
        functools.partial(_combine_kernel, tm=tm, ns=ns),
        grid=(t // tm,),
        in_specs=[
            pl.BlockSpec(memory_space=pltpu.SMEM),
            pl.BlockSpec((tm, V7X_LANES), lambda i: (i, 0)),
            pl.BlockSpec((tm, d), lambda i: (i, 0)),
            pl.BlockSpec(memory_space=pl.ANY),
        ],
        out_specs=pl.BlockSpec((tm, d), lambda i: (i, 0)),
        out_shape=jax.ShapeDtypeStruct((t, d), F32),
        scratch_shapes=[pltpu.VMEM((2, ns, d), F32), pltpu.SemaphoreType.DMA((2,))],
        compiler_params=_cparams("arbitrary"),
        name="moe_combine",
    )(extra, route, x1, yb)


def _head_indicator(width, head_dim):
    lane_head = jnp.arange(width) // head_dim
    return (lane_head[:, None] == jnp.arange(V7X_LANES)[None, :]).astype(BF16)


def _layer(x, mem, norm_mix, w_in, a_q_gain, a_k_gain, a_rel_bias, conv_w, conv_b, dt_bias, a_log,
           d_skip, ssm_norm, norm_mem, w_mem_kv, x_q_gain, x_k_gain, w_br_a, w_br_b, w_br_c, w_out,
           norm_ffn, w_router, b_router, w_gate, b_gate, w_up, b_up, w_down, b_down):
    b, s, d = x.shape
    t = b * s
    a_width = A_HEADS * A_HEAD_DIM
    inner = SSM_HEADS * SSM_HEAD_DIM
    gs = SSM_GROUPS * SSM_STATE
    x_width = X_HEADS * X_HEAD_DIM
    assert d == a_width == x_width and inner == 2 * d and 2 * gs == d

    o_dt = 3 * a_width + inner + inner + 2 * gs
    assert o_dt % d == 0
    w_main = jnp.concatenate([w_in[:, :o_dt], w_in[:, o_dt + SSM_HEADS:]], axis=1).astype(BF16)
    plain2 = ("plain", "plain")
    roles = (("qa", "ka"), plain2, plain2, plain2, ("qx", "sig"), ("sig", "sig"))
    w_dt = jnp.pad(w_in[:, o_dt:o_dt + SSM_HEADS], ((0, 0), (0, V7X_LANES - SSM_HEADS))).astype(BF16)
    gains = jnp.zeros((V7X_SUBLANES, d), F32)
    gains = gains.at[0].set(jnp.tile(a_q_gain, A_HEADS) * (A_HEAD_DIM ** -0.5 * LOG2_E))
    gains = gains.at[1].set(jnp.tile(a_k_gain, A_HEADS))
    gains = gains.at[2].set(jnp.tile(x_q_gain, X_HEADS) * X_HEAD_DIM ** -0.5)

    x2d = x.reshape(t, d)
    proj, dt_raw = _in_proj(x2d, norm_mix.reshape(1, d), w_main, w_dt, gains, roles,
                            tm=min(1024, t), tn=2 * d)
    proj3 = proj.reshape(b, s, proj.shape[1])

    y_a = _attention(proj3, _attn_bias(a_rel_bias), q_tile=0, k_tile=1, v_tile=2)

    pad_h = lambda v: jnp.pad(v.astype(F32), (0, V7X_LANES - SSM_HEADS)).reshape(1, V7X_LANES)
    e_mat = _head_indicator(inner, SSM_HEAD_DIM).T
    y_b = _ssd(proj3, dt_raw.reshape(b, s, V7X_LANES),
               conv_w[:, :inner], conv_b[:inner].reshape(1, inner),
               conv_w[:, inner:], conv_b[inner:].reshape(1, 2 * gs),
               pad_h(dt_bias), pad_h(-jnp.exp(a_log.astype(F32))),
               jnp.repeat(d_skip.astype(F32), SSM_HEAD_DIM).reshape(1, inner),
               ssm_norm.reshape(1, inner), e_mat, z_tile=3, x_tile=5, bc_tile=7)

    k_mem, v_mem = _mem_kv(mem, norm_mem.reshape(1, d), w_mem_kv.astype(BF16),
                           x_k_gain.reshape(1, X_HEAD_DIM))
    y_c = _mem_attn(proj3, k_mem, v_mem, q_blk=8, tq=min(512, s))

    w_r = jnp.pad(w_router, ((0, 0), (0, V7X_LANES - N_EXPERTS)))
    b_r = jnp.pad(b_router, (0, V7X_LANES - N_EXPERTS)).reshape(1, V7X_LANES)
    tm_moe = min(MOE_TILE, t)
    n_tiles = t // tm_moe
    x1, h2, route, tile_cnt = _merge(
        x2d, y_a.reshape(t, a_width), y_b.reshape(t, inner), y_c.reshape(t, x_width), proj,
        w_br_a.astype(BF16), w_br_b.astype(BF16), w_br_c.astype(BF16), w_out.astype(BF16),
        norm_ffn.reshape(1, d), w_r, b_r, gate_blk=9, tm=min(MERGE_TM, t), rt=tm_moe)

    ns = -(-(tm_moe * TOP_K + N_EXPERTS * (RUN - 1)) // V7X_LANES) * V7X_LANES
    units = ns // RUN
    n_blocks = -(-(t * TOP_K + n_tiles * N_EXPERTS * (RUN - 1) + N_EXPERTS * (MOE_ROWS - 1))
                 // MOE_ROWS)
    n8 = tile_cnt[:, 0, :N_EXPERTS].astype(jnp.int32) // RUN
    total = jnp.sum(n8, axis=0)
    padded = (total + BLOCK_PIECES - 1) // BLOCK_PIECES * BLOCK_PIECES
    pad_ends = jnp.cumsum(padded)
    pad_starts = pad_ends - padded
    n_used = (pad_ends[-1] // BLOCK_PIECES).reshape(1).astype(jnp.int32)
    blk = jnp.minimum(jnp.arange(n_blocks, dtype=jnp.int32), n_used[0] - 1)
    block_exp = jnp.minimum(jnp.sum(pad_ends[None, :] <= (blk * BLOCK_PIECES)[:, None], axis=1),
                            N_EXPERTS - 1).astype(jnp.int32)
    slot_j = jnp.arange(BLOCK_PIECES, dtype=jnp.int32)[None, :]
    onehot_e = (block_exp[:, None] == jnp.arange(N_EXPERTS)[None, :]).astype(jnp.int32)
    q = blk[:, None] * BLOCK_PIECES + slot_j - (onehot_e @ pad_starts)[:, None]
    real = q < (onehot_e @ total)[:, None]
    ends_b = onehot_e @ jnp.cumsum(n8, axis=0).T
    tile_of = jnp.minimum(jnp.sum(ends_b[:, None, :] <= q[:, :, None], axis=-1), n_tiles - 1)
    tile_1h = (tile_of[:, :, None] == jnp.arange(n_tiles)[None, None, :]).astype(jnp.int32)
    starts_b = ends_b - onehot_e @ n8.T
    in_tile_b = onehot_e @ (jnp.cumsum(n8, axis=1) - n8).T
    piece = (tile_of * units + jnp.sum(tile_1h * (in_tile_b - starts_b)[:, None, :], axis=-1) + q)
    zero_piece = (tm_moe * TOP_K + N_EXPERTS * (RUN - 1)) // RUN
    spare = n_tiles * units + (blk % 2)[:, None] * BLOCK_PIECES + slot_j
    tab = jnp.concatenate([jnp.where(real, piece, zero_piece), jnp.where(real, piece, spare)], axis=1)
    tab = tab.reshape(n_blocks, 1, 2 * BLOCK_PIECES).astype(jnp.int32)
    used = jnp.sum(n8, axis=1)
    tail = (units - used).astype(jnp.int32)
    extra = (used - tm_moe * TOP_K // RUN).astype(jnp.int32)

    xs = _dispatch(h2, route, tm_moe, ns)
    yb = _experts(block_exp, n_used, tail, tab, xs,
                  w_gate, b_gate.reshape(N_EXPERTS, 1, -1),
                  w_up, b_up.reshape(N_EXPERTS, 1, -1),
                  w_down, b_down.reshape(N_EXPERTS, 1, -1), ns)
    out = _combine(extra, route, x1, yb, tm_moe, ns)
    return out.reshape(b, s, d)


def kernel(x, mem, norm_mix, w_in, a_q_gain, a_k_gain, a_rel_bias, conv_w, conv_b, dt_bias, a_log, d_skip, ssm_norm, norm_mem, w_mem_kv, x_q_gain, x_k_gain, w_br_a, w_br_b, w_br_c, w_out, norm_ffn, w_router, b_router, w_gate, b_gate, w_up, b_up, w_down, b_down):
    for l in range(norm_mix.shape[0]):
        x = _layer(x, mem, norm_mix[l], w_in[l], a_q_gain[l], a_k_gain[l], a_rel_bias[l], conv_w[l],
                   conv_b[l], dt_bias[l], a_log[l], d_skip[l], ssm_norm[l], norm_mem[l], w_mem_kv[l],
                   x_q_gain[l], x_k_gain[l], w_br_a[l], w_br_b[l], w_br_c[l], w_out[l], norm_ffn[l],
                   w_router[l], b_router[l], w_gate[l], b_gate[l], w_up[l], b_up[l], w_down[l],
                   b_down[l])
    return x
```

```python
import functools

import jax
import jax.numpy as jnp
from jax import lax
from jax.experimental import pallas as pl
from jax.experimental.pallas import tpu as pltpu

F32 = jnp.float32
BF16 = jnp.bfloat16
HIGHEST = lax.Precision.HIGHEST

V7X_LANES = 128
V7X_SUBLANES = 8
V7X_VMEM_LIMIT_BYTES = 56 * 1024 * 1024

EPS = 1e-6
LOG2_E = 1.4426950408889634
NEG = -1e30

CHUNK = 64
A_HEADS = 16
A_HEAD_DIM = 64
LEFT_CHUNKS = 8
REL_CLIP = 128
SSM_HEADS = 32
SSM_HEAD_DIM = 64
SSM_GROUPS = 4
SSM_STATE = 128
CONV_WIDTH = 4
X_HEADS = 4
X_HEAD_DIM = 256
N_EXPERTS = 32
TOP_K = 4
SWIGLU_LIMIT = 7.0
SWIGLU_ALPHA = 1.702

ATTN_TQ = 256
SSD_L = 256
MOE_ROWS = 512
MOE_TILE = 512
MERGE_TM = 512


def _cparams(*sem):
    return pltpu.CompilerParams(dimension_semantics=sem, vmem_limit_bytes=V7X_VMEM_LIMIT_BYTES)


def _split_bf16(v):
    hi = v.astype(BF16)
    lo = (v - hi.astype(F32)).astype(BF16)
    return hi, lo


def _in_proj_kernel(x_ref, nw_ref, wa_ref, wb_ref, wdt_ref, gains_ref, o_ref, dt_ref, h_ref, *,
                    roles, steps_a):
    j = pl.program_id(1)

    @pl.when(j == 0)
    def _():
        x = x_ref[...]
        ms = jnp.mean(x * x, axis=-1, keepdims=True)
        hb = (x * lax.rsqrt(ms + EPS) * nw_ref[...]).astype(BF16)
        h_ref[...] = hb
        dt_ref[...] = jnp.dot(hb, wdt_ref[...], preferred_element_type=F32)

    def head_norm(acc, head_dim, gain_row):
        outs = []
        if head_dim >= V7X_LANES:
            for c0 in range(0, acc.shape[1], head_dim):
                blk = acc[:, c0:c0 + head_dim]
                s = jnp.sum(blk * blk, axis=-1, keepdims=True)
                scale = lax.rsqrt(s * (1.0 / head_dim) + EPS)
                outs.append(blk * scale * gain_row[:, c0:c0 + head_dim])
        else:
            assert 2 * head_dim == V7X_LANES
            lo = lax.broadcasted_iota(jnp.int32, (1, V7X_LANES), 1) < head_dim
            for c0 in range(0, acc.shape[1], V7X_LANES):
                blk = acc[:, c0:c0 + V7X_LANES]
                sq = blk * blk
                s_lo = jnp.sum(jnp.where(lo, sq, 0.0), axis=-1, keepdims=True)
                s_hi = jnp.sum(jnp.where(lo, 0.0, sq), axis=-1, keepdims=True)
                scale = jnp.where(lo, lax.rsqrt(s_lo * (1.0 / head_dim) + EPS),
                                  lax.rsqrt(s_hi * (1.0 / head_dim) + EPS))
                outs.append(blk * scale * gain_row[:, c0:c0 + V7X_LANES])
        return jnp.concatenate(outs, axis=1)

    def cond_for(role):
        c = None
        for jj, r in enumerate(roles):
            if r == role:
                c = (j == jj) if c is None else jnp.logical_or(c, j == jj)
        return c

    def epilogue(acc, role):
        if role == "qa":
            return head_norm(acc, A_HEAD_DIM, gains_ref[0:1, :])
        if role == "ka":
            return head_norm(acc, A_HEAD_DIM, gains_ref[1:2, :])
        if role == "qx":
            return head_norm(acc, X_HEAD_DIM, gains_ref[2:3, :])
        if role == "sig":
            return 0.5 * jnp.tanh(0.5 * acc) + 0.5
        return acc

    d = x_ref.shape[1]
    for step_roles in sorted(set(roles)):
        in_a = {jj < steps_a for jj, r in enumerate(roles) if r == step_roles}
        assert len(in_a) == 1
        w_ref = wa_ref if in_a.pop() else wb_ref

        @pl.when(cond_for(step_roles))
        def _(step_roles=step_roles, w_ref=w_ref):
            acc = jnp.dot(h_ref[...], w_ref[...], preferred_element_type=F32)
            for k, role in enumerate(step_roles):
                cs = slice(k * d, (k + 1) * d)
                o_ref[:, cs] = epilogue(acc[:, cs], role).astype(o_ref.dtype)


def _in_proj(x2d, norm_w, w_a, w_b, w_dt, gains, roles, tm, tn):
    t, d = x2d.shape
    n = w_a.shape[1] + w_b.shape[1]
    assert t % tm == 0 and w_a.shape[1] % tn == 0 and w_b.shape[1] % tn == 0
    assert len(roles) == n // tn
    assert all(len(r) * d == tn for r in roles)
    steps_a = w_a.shape[1] // tn
    const = lambda i, j: (0, 0)
    return pl.pallas_call(
        functools.partial(_in_proj_kernel, roles=roles, steps_a=steps_a),
        grid=(t // tm, n // tn),
        in_specs=[
            pl.BlockSpec((tm, d), lambda i, j: (i, 0)),
            pl.BlockSpec((1, d), const),
            pl.BlockSpec((d, tn), lambda i, j: (0, jnp.minimum(j, steps_a - 1))),
            pl.BlockSpec((d, tn), lambda i, j: (0, jnp.maximum(j - steps_a, 0))),
            pl.BlockSpec((d, V7X_LANES), const),
            pl.BlockSpec(gains.shape, const),
        ],
        out_specs=[
            pl.BlockSpec((tm, tn), lambda i, j: (i, j)),
            pl.BlockSpec((tm, V7X_LANES), lambda i, j: (i, 0)),
        ],
        out_shape=[
            jax.ShapeDtypeStruct((t, n), BF16),
            jax.ShapeDtypeStruct((t, V7X_LANES), F32),
        ],
        scratch_shapes=[pltpu.VMEM((tm, d), BF16)],
        compiler_params=_cparams("parallel", "arbitrary"),
        name="in_proj",
    )(x2d, norm_w, w_a, w_b, w_dt, gains)


def _attn_kernel(q_ref, *refs, tq, nprev):
    k_refs = refs[:nprev + 1]
    v_refs = refs[nprev + 1:2 * nprev + 2]
    bias_ref, o_ref = refs[2 * nprev + 2:]
    qb = pl.program_id(1)
    nk = (nprev + 1) * tq
    lane = lax.broadcasted_iota(jnp.int32, (1, V7X_LANES), 1)
    col = lax.broadcasted_iota(jnp.int32, (1, nk), 1)
    before_start = col < (nprev - qb) * tq
    for hp in range(A_HEADS // 2):
        ls = slice(hp * V7X_LANES, (hp + 1) * V7X_LANES)
        q2 = q_ref[:, ls]
        kk = jnp.concatenate([r[:, ls] for r in k_refs], axis=0)
        vv = jnp.concatenate([r[:, ls] for r in v_refs], axis=0)
        outs = []
        for hh in range(2):
            sel = (lane < A_HEAD_DIM) if hh == 0 else (lane >= A_HEAD_DIM)
            qm = jnp.where(sel, q2, jnp.zeros_like(q2))
            s = lax.dot_general(qm, kk, (((1,), (1,)), ((), ())), preferred_element_type=F32)
            s = jnp.where(before_start, NEG, s + bias_ref[2 * hp + hh])
            m = jnp.max(s, axis=-1, keepdims=True)
            p = jnp.exp2(s - m)
            l = jnp.sum(p, axis=-1, keepdims=True)
            o = jnp.dot(p.astype(BF16), vv, preferred_element_type=F32)
            outs.append(o / l)
        o_ref[:, ls] = jnp.where(lane < A_HEAD_DIM, outs[0], outs[1]).astype(o_ref.dtype)


def _attention(proj3, bias, q_tile, k_tile, v_tile):
    b, s, _ = proj3.shape
    tq = ATTN_TQ
    width = A_HEADS * A_HEAD_DIM
    left = LEFT_CHUNKS * CHUNK
    assert left % tq == 0 and s % tq == 0
    nprev = left // tq

    def kv_spec(tile, back):
        return pl.BlockSpec((None, tq, width), lambda bi, qi: (bi, jnp.maximum(qi - back, 0), tile))

    in_specs = [pl.BlockSpec((None, tq, width), lambda bi, qi: (bi, qi, q_tile))]
    in_specs += [kv_spec(k_tile, nprev - i) for i in range(nprev + 1)]
    in_specs += [kv_spec(v_tile, nprev - i) for i in range(nprev + 1)]
    in_specs += [pl.BlockSpec(bias.shape, lambda bi, qi: (0, 0, 0), pipeline_mode=pl.Buffered(1))]
    return pl.pallas_call(
        functools.partial(_attn_kernel, tq=tq, nprev=nprev),
        grid=(b, s // tq),
        in_specs=in_specs,
        out_specs=pl.BlockSpec((None, tq, width), lambda bi, qi: (bi, qi, 0)),
        out_shape=jax.ShapeDtypeStruct((b, s, width), BF16),
        compiler_params=_cparams("parallel", "parallel"),
        name="chunk_attn",
    )(proj3, *([proj3] * (2 * nprev + 2)), bias)


def _attn_bias_kernel(v_ref, o_ref, *, tq, nk):
    x = jnp.broadcast_to(v_ref[...], (tq, v_ref.shape[-1]))
    toeplitz = pltpu.roll(x, 0, 1, stride=1, stride_axis=0)[:, :nk]
    qc = lax.broadcasted_iota(jnp.int32, (tq, nk), 0) // CHUNK
    kc = lax.broadcasted_iota(jnp.int32, (tq, nk), 1) // CHUNK
    in_band = jnp.where(kc >= qc, kc - qc, LEFT_CHUNKS + 1) <= LEFT_CHUNKS
    o_ref[...] = jnp.where(in_band, toeplitz * LOG2_E, NEG)


def _attn_bias(rel_bias):
    tq = ATTN_TQ
    left = LEFT_CHUNKS * CHUNK
    nk = left + tq
    m_len = 1 << (tq + nk - 1).bit_length()
    m = jnp.arange(m_len)
    diff = jnp.where(m < nk, m, m - m_len)
    v = rel_bias[:, jnp.clip(left - diff, -REL_CLIP, REL_CLIP) + REL_CLIP].astype(F32)
    h = v.shape[0]
    return pl.pallas_call(
        functools.partial(_attn_bias_kernel, tq=tq, nk=nk),
        grid=(h,),
        in_specs=[pl.BlockSpec((None, 1, m_len), lambda i: (i, 0, 0))],
        out_specs=pl.BlockSpec((None, tq, nk), lambda i: (i, 0, 0)),
        out_shape=jax.ShapeDtypeStruct((h, tq, nk), F32),
        compiler_params=_cparams("parallel"),
        name="attn_bias",
    )(v.reshape(h, 1, m_len))


def _ssd_kernel(z0_ref, z1_ref, x0_ref, x1_ref, bc_ref, dt_ref, cwx_ref, cbx_ref, cwbc_ref, cbbc_ref,
                dtb_ref, aneg_ref, dskip_ref, gain_ref, e_ref, o_ref,
                xf_ref, bcf_ref, st_ref, xs_ref, y_ref, *, L):
    c = pl.program_id(1)
    inner = SSM_HEADS * SSM_HEAD_DIM
    gw = inner // SSM_GROUPS
    gs = SSM_GROUPS * SSM_STATE
    tail = V7X_SUBLANES
    half = inner // 2

    @pl.when(c == 0)
    def _():
        xf_ref[0:tail, :] = jnp.zeros((tail, inner), F32)
        bcf_ref[0:tail, :] = jnp.zeros((tail, 2 * gs), F32)
        st_ref[...] = jnp.zeros_like(st_ref)

    xf_ref[tail:, 0:half] = x0_ref[...].astype(F32)
    xf_ref[tail:, half:inner] = x1_ref[...].astype(F32)
    bcf_ref[tail:, :] = bc_ref[...].astype(F32)

    def conv_silu(src_ref, w_ref, b_ref, c0, c1):
        acc = b_ref[:, c0:c1] + w_ref[CONV_WIDTH - 1:CONV_WIDTH, c0:c1] * src_ref[tail:tail + L, c0:c1]
        for k in range(1, CONV_WIDTH):
            acc = acc + (w_ref[CONV_WIDTH - 1 - k:CONV_WIDTH - k, c0:c1]
                         * src_ref[tail - k:tail - k + L, c0:c1])
        return acc * jax.nn.sigmoid(acc)

    for g in range(SSM_GROUPS):
        xs_ref[:, g * gw:(g + 1) * gw] = conv_silu(xf_ref, cwx_ref, cbx_ref, g * gw, (g + 1) * gw)
    bmat = conv_silu(bcf_ref, cwbc_ref, cbbc_ref, 0, gs)
    cmat = conv_silu(bcf_ref, cwbc_ref, cbbc_ref, gs, 2 * gs)
    xf_ref[0:tail, :] = xf_ref[L:L + tail, :]
    bcf_ref[0:tail, :] = bcf_ref[L:L + tail, :]

    pre = dt_ref[...] + dtb_ref[...]
    dt = jnp.maximum(pre, 0.0) + jnp.log1p(jnp.exp(-jnp.abs(pre)))
    a = dt * aneg_ref[...]
    row = lax.broadcasted_iota(jnp.int32, (L, L), 0)
    colm = lax.broadcasted_iota(jnp.int32, (L, L), 1)
    lower = colm <= row
    tri = jnp.where(lower, 1.0, 0.0).astype(F32)
    cs = jnp.dot(tri, a, precision=HIGHEST, preferred_element_type=F32)
    cs2 = cs * LOG2_E
    src_t = (cs2 - jnp.log2(dt)).T
    cs_last = cs[L - 1:L, :]
    w_state = dt * jnp.exp(cs_last - cs)
    e_cs = jnp.exp(cs)
    chunk_decay = jnp.broadcast_to(jnp.exp(cs_last), (tail, V7X_LANES))
    stacked = jnp.concatenate([w_state, e_cs, chunk_decay], axis=0)
    s_hi, s_lo = _split_bf16(stacked)
    expanded = (jnp.dot(s_hi, e_ref[...], preferred_element_type=F32)
                + jnp.dot(s_lo, e_ref[...], preferred_element_type=F32))
    w_state_e = expanded[0:L]
    e_cs_e = expanded[L:2 * L]
    decay_e = expanded[2 * L:2 * L + 1]

    lane = lax.broadcasted_iota(jnp.int32, (1, V7X_LANES), 1)
    pairs_per_group = gw // V7X_LANES
    for g in range(SSM_GROUPS):
        bg = bmat[:, g * SSM_STATE:(g + 1) * SSM_STATE]
        cg = cmat[:, g * SSM_STATE:(g + 1) * SSM_STATE].astype(BF16)
        cb = lax.dot_general(cg, bg.astype(BF16), (((1,), (1,)), ((), ())),
                             preferred_element_type=F32)
        state_b = st_ref[g].astype(BF16)
        y_off = jnp.dot(cg, state_b, preferred_element_type=F32) * e_cs_e[:, g * gw:(g + 1) * gw]
        for pr in range(pairs_per_group):
            c0 = g * gw + pr * V7X_LANES
            xp = xs_ref[:, c0:c0 + V7X_LANES]
            xpb = xp.astype(BF16)
            acc = y_off[:, pr * V7X_LANES:(pr + 1) * V7X_LANES] + dskip_ref[:, c0:c0 + V7X_LANES] * xp
            for hh in range(2):
                h = c0 // SSM_HEAD_DIM + hh
                d = cs2[:, h:h + 1] - src_t[h:h + 1, :]
                m = cb * jnp.exp2(jnp.where(lower, d, NEG))
                sel = (lane < SSM_HEAD_DIM) if hh == 0 else (lane >= SSM_HEAD_DIM)
                xm = jnp.where(sel, xpb, jnp.zeros_like(xpb))
                acc = acc + jnp.dot(m.astype(BF16), xm, preferred_element_type=F32)
            y_ref[:, c0:c0 + V7X_LANES] = acc
        xw = (xs_ref[:, g * gw:(g + 1) * gw] * w_state_e[:, g * gw:(g + 1) * gw]).astype(BF16)
        new = jnp.dot(bg.T.astype(BF16), xw, preferred_element_type=F32)
        st_ref[g] = st_ref[g] * decay_e[:, g * gw:(g + 1) * gw] + new

    for g in range(SSM_GROUPS):
        sl = slice(g * gw, (g + 1) * gw)
        z_ref, z0 = (z0_ref, g * gw) if g * gw < half else (z1_ref, g * gw - half)
        zz = z_ref[:, z0:z0 + gw].astype(F32)
        yz = y_ref[:, sl] * (zz * jax.nn.sigmoid(zz))
        ms = jnp.mean(yz * yz, axis=-1, keepdims=True)
        o_ref[:, sl] = (yz * lax.rsqrt(ms + EPS) * gain_ref[:, sl]).astype(o_ref.dtype)


def _ssd(proj3, dt3, cwx, cbx, cwbc, cbbc, dtb, aneg, dskip_e, gain, e_mat, z_tile, x_tile, bc_tile):
    b, s, _ = proj3.shape
    L = SSD_L
    assert s % L == 0
    inner = SSM_HEADS * SSM_HEAD_DIM
    gs2 = 2 * SSM_GROUPS * SSM_STATE
    assert gs2 == inner // 2
    const = lambda bi, ci: (0, 0)
    full = lambda a: pl.BlockSpec(a.shape, const)
    tile = lambda k: pl.BlockSpec((None, L, gs2), lambda bi, ci: (bi, ci, k))
    return pl.pallas_call(
        functools.partial(_ssd_kernel, L=L),
        grid=(b, s // L),
        in_specs=[
            tile(z_tile), tile(z_tile + 1), tile(x_tile), tile(x_tile + 1), tile(bc_tile),
            pl.BlockSpec((None, L, V7X_LANES), lambda bi, ci: (bi, ci, 0)),
            full(cwx), full(cbx), full(cwbc), full(cbbc), full(dtb), full(aneg), full(dskip_e),
            full(gain), full(e_mat),
        ],
        out_specs=pl.BlockSpec((None, L, inner), lambda bi, ci: (bi, ci, 0)),
        out_shape=jax.ShapeDtypeStruct((b, s, inner), BF16),
        scratch_shapes=[
            pltpu.VMEM((L + V7X_SUBLANES, inner), F32),
            pltpu.VMEM((L + V7X_SUBLANES, gs2), F32),
            pltpu.VMEM((SSM_GROUPS, SSM_STATE, inner // SSM_GROUPS), F32),
            pltpu.VMEM((L, inner), F32),
            pltpu.VMEM((L, inner), F32),
        ],
        compiler_params=_cparams("parallel", "arbitrary"),
        name="ssd",
    )(proj3, proj3, proj3, proj3, proj3, dt3, cwx, cbx, cwbc, cbbc, dtb, aneg, dskip_e, gain, e_mat)


def _mem_kv_kernel(mem_ref, g_ref, w_ref, kg_ref, k_ref, v_ref):
    m = mem_ref[...]
    ms = jnp.mean(m * m, axis=-1, keepdims=True)
    mn = (m * lax.rsqrt(ms + EPS) * g_ref[...]).astype(BF16)
    kv = jnp.dot(mn, w_ref[...], preferred_element_type=F32)
    width = X_HEADS * X_HEAD_DIM
    for h in range(X_HEADS):
        sl = slice(h * X_HEAD_DIM, (h + 1) * X_HEAD_DIM)
        kh = kv[:, sl]
        r = lax.rsqrt(jnp.mean(kh * kh, axis=-1, keepdims=True) + EPS)
        k_ref[:, sl] = (kh * r * kg_ref[...]).astype(k_ref.dtype)
    v_ref[...] = kv[:, width:].astype(v_ref.dtype)


def _mem_kv(mem, norm_mem, w_kv, k_gain):
    b, m, d = mem.shape
    width = X_HEADS * X_HEAD_DIM
    const = lambda bi: (0, 0)
    return pl.pallas_call(
        _mem_kv_kernel,
        grid=(b,),
        in_specs=[
            pl.BlockSpec((None, m, d), lambda bi: (bi, 0, 0)),
            pl.BlockSpec((1, d), const),
            pl.BlockSpec((d, 2 * width), const),
            pl.BlockSpec((1, X_HEAD_DIM), const),
        ],
        out_specs=[pl.BlockSpec((None, m, width), lambda bi: (bi, 0, 0))] * 2,
        out_shape=[jax.ShapeDtypeStruct((b, m, width), BF16)] * 2,
        compiler_params=_cparams("parallel"),
        name="mem_kv",
    )(mem, norm_mem, w_kv, k_gain)


def _mem_attn_kernel(q_ref, k_ref, v_ref, o_ref):
    for h in range(X_HEADS):
        sl = slice(h * X_HEAD_DIM, (h + 1) * X_HEAD_DIM)
        s = lax.dot_general(q_ref[:, sl], k_ref[:, sl], (((1,), (1,)), ((), ())),
                            preferred_element_type=F32)
        m = jnp.max(s, axis=-1, keepdims=True)
        p = jnp.exp(s - m)
        l = jnp.sum(p, axis=-1, keepdims=True)
        o = jnp.dot(p.astype(BF16), v_ref[:, sl], preferred_element_type=F32)
        o_ref[:, sl] = (o / l).astype(o_ref.dtype)


def _mem_attn(proj3, k, v, q_blk, tq):
    b, s, _ = proj3.shape
    m = k.shape[1]
    width = X_HEADS * X_HEAD_DIM
    assert s % tq == 0
    return pl.pallas_call(
        _mem_attn_kernel,
        grid=(b, s // tq),
        in_specs=[
            pl.BlockSpec((None, tq, width), lambda bi, qi: (bi, qi, q_blk)),
            pl.BlockSpec((None, m, width), lambda bi, qi: (bi, 0, 0)),
            pl.BlockSpec((None, m, width), lambda bi, qi: (bi, 0, 0)),
        ],
        out_specs=pl.BlockSpec((None, tq, width), lambda bi, qi: (bi, qi, 0)),
        out_shape=jax.ShapeDtypeStruct((b, s, width), BF16),
        compiler_params=_cparams("parallel", "parallel"),
        name="mem_attn",
    )(proj3, k, v)


def _merge_kernel(x_ref, ya_ref, yb_ref, yc_ref, g0_ref, g1_ref, g2_ref, wa_ref, wb_ref, wc_ref,
                  wo_ref, nf_ref, wrh_ref, wrl_ref, br_ref,
                  x1_ref, h2_ref, route_ref, cnt_ref, *, tm, rt):
    merged = (g0_ref[...].astype(F32) * jnp.dot(ya_ref[...], wa_ref[...], preferred_element_type=F32)
              + g1_ref[...].astype(F32) * jnp.dot(yb_ref[...], wb_ref[...], preferred_element_type=F32)
              + g2_ref[...].astype(F32) * jnp.dot(yc_ref[...], wc_ref[...], preferred_element_type=F32))
    x1 = x_ref[...] + jnp.dot(merged.astype(BF16), wo_ref[...], preferred_element_type=F32)
    x1_ref[...] = x1
    ms = jnp.mean(x1 * x1, axis=-1, keepdims=True)
    h2 = x1 * lax.rsqrt(ms + EPS) * nf_ref[...]
    h2_ref[...] = h2.astype(h2_ref.dtype)

    h_hi, h_lo = _split_bf16(h2)
    logits_all = (jnp.dot(h_hi, wrh_ref[...], preferred_element_type=F32)
                  + jnp.dot(h_lo, wrh_ref[...], preferred_element_type=F32)
                  + jnp.dot(h_hi, wrl_ref[...], preferred_element_type=F32)) + br_ref[...]
    for sub in range(tm // rt):
        _route_tile(logits_all[sub * rt:(sub + 1) * rt], route_ref.at[pl.ds(sub * rt, rt)],
                    cnt_ref.at[sub], rt)


def _route_tile(logits, route_ref, cnt_ref, tm):
    lane = lax.broadcasted_iota(jnp.int32, (tm, V7X_LANES), 1)
    lane_f = lane.astype(F32)
    work = jnp.where(lane < N_EXPERTS, logits, NEG)
    sel_val, sel_oh = [], []
    for _ in range(TOP_K):
        mval = jnp.max(work, axis=-1, keepdims=True)
        ik = jnp.min(jnp.where(work == mval, lane_f, float(V7X_LANES)), axis=-1, keepdims=True)
        oh = lane_f == ik
        work = jnp.where(oh, NEG, work)
        sel_val.append(mval)
        sel_oh.append(oh)
    ex = [jnp.exp(v - sel_val[0]) for v in sel_val]
    denom = ex[0] + ex[1] + ex[2] + ex[3]

    oh_all = jnp.zeros((tm, V7X_LANES), F32)
    for oh in sel_oh:
        oh_all = oh_all + jnp.where(oh, 1.0, 0.0)
    row = lax.broadcasted_iota(jnp.int32, (tm, tm), 0)
    colm = lax.broadcasted_iota(jnp.int32, (tm, tm), 1)
    strict = jnp.where(colm < row, 1.0, 0.0).astype(BF16)
    before = jnp.dot(strict, oh_all.astype(BF16), preferred_element_type=F32)
    cnt = jnp.sum(oh_all, axis=0, keepdims=True)
    cnt8 = jnp.floor((cnt + (V7X_SUBLANES - 1.0)) * (1.0 / V7X_SUBLANES)) * V7X_SUBLANES
    cnt8 = jnp.broadcast_to(cnt8, (V7X_SUBLANES, V7X_LANES))
    er = lax.broadcasted_iota(jnp.int32, (V7X_LANES, V7X_LANES), 0)
    ec = lax.broadcasted_iota(jnp.int32, (V7X_LANES, V7X_LANES), 1)
    earlier = jnp.where(er < ec, 1.0, 0.0).astype(BF16)
    run_start = jnp.dot(cnt8.astype(BF16), earlier, preferred_element_type=F32)[0:1, :]
    slot = before + run_start
    route = jnp.zeros((tm, V7X_LANES), F32)
    for k in range(TOP_K):
        pos = jnp.sum(jnp.where(sel_oh[k], slot, 0.0), axis=-1, keepdims=True)
        route = jnp.where(lane == k, pos, route)
        route = jnp.where(lane == TOP_K + k, ex[k] / denom, route)
    route_ref[...] = route
    cnt_ref[...] = cnt8


def _merge(x2d, ya, yb, yc, proj, wa, wb, wc, wo, nf, wr, br, gate_blk, tm, rt):
    t, d = x2d.shape
    assert t % tm == 0 and tm % rt == 0
    wr_hi, wr_lo = _split_bf16(wr)
    const = lambda i: (0, 0)
    full = lambda a: pl.BlockSpec(a.shape, const)
    rows = lambda w: pl.BlockSpec((tm, w), lambda i: (i, 0))
    return pl.pallas_call(
        functools.partial(_merge_kernel, tm=tm, rt=rt),
        grid=(t // tm,),
        in_specs=[
            rows(d), rows(ya.shape[1]), rows(yb.shape[1]), rows(yc.shape[1]),
            pl.BlockSpec((tm, d), lambda i: (i, gate_blk)),
            pl.BlockSpec((tm, d), lambda i: (i, gate_blk + 1)),
            pl.BlockSpec((tm, d), lambda i: (i, gate_blk + 2)),
            full(wa), full(wb), full(wc), full(wo), full(nf), full(wr_hi), full(wr_lo), full(br),
        ],
        out_specs=[rows(d), rows(d), rows(V7X_LANES),
                   pl.BlockSpec((tm // rt, V7X_SUBLANES, V7X_LANES), lambda i: (i, 0, 0))],
        out_shape=[
            jax.ShapeDtypeStruct((t, d), F32),
            jax.ShapeDtypeStruct((t, d), BF16),
            jax.ShapeDtypeStruct((t, V7X_LANES), F32),
            jax.ShapeDtypeStruct((t // rt, V7X_SUBLANES, V7X_LANES), F32),
        ],
        compiler_params=_cparams("parallel"),
        name="merge_route",
    )(x2d, ya, yb, yc, proj, proj, proj, wa, wb, wc, wo, nf, wr_hi, wr_lo, br)


RUN = V7X_SUBLANES
BLOCK_PIECES = MOE_ROWS // RUN
TAIL_SIZES = (32, 16, 8, 4, 2, 1)
assert MOE_TILE * TOP_K % V7X_LANES == 0 and N_EXPERTS * (RUN - 1) < 2 * TAIL_SIZES[0] * RUN


def _binary_pieces(n, body):
    for size in TAIL_SIZES:
        @pl.when((n & size) != 0)
        def _(size=size):
            body(n & ~(2 * size - 1), size)


def _dispatch_kernel(h_ref, route_ref, xs_ref, *, tm, ns):
    pos_t = route_ref[...].T
    q = lax.broadcasted_iota(jnp.int32, (ns, tm), 0).astype(F32)
    perm = jnp.zeros((ns, tm), F32)
    for k in range(TOP_K):
        perm = perm + jnp.where(q == pos_t[k:k + 1, :], 1.0, 0.0)
    xs_ref[...] = jnp.dot(perm.astype(BF16), h_ref[...], preferred_element_type=F32)


def _dispatch(h2, route, tm, ns):
    t, d = h2.shape
    return pl.pallas_call(
        functools.partial(_dispatch_kernel, tm=tm, ns=ns),
        grid=(t // tm,),
        in_specs=[
            pl.BlockSpec((tm, d), lambda i: (i, 0)),
            pl.BlockSpec((tm, V7X_LANES), lambda i: (i, 0)),
        ],
        out_specs=pl.BlockSpec((ns, d), lambda i: (i, 0)),
        out_shape=jax.ShapeDtypeStruct((t // tm * ns, d), F32),
        compiler_params=_cparams("parallel"),
        name="moe_dispatch",
    )(h2, route)


def _experts_kernel(bexp_ref, nused_ref, tail_ref, tab_ref, next_tab_ref, xs_ref, wg_ref, bg_ref,
                    wu_ref, bu_ref, wd_ref, bd_ref, yb_ref,
                    wgb_ref, wub_ref, wdb_ref, xbuf_ref, ybuf_ref, zero_ref, gsem, ssem, zsem, *,
                    ns, n_tiles):
    b = pl.program_id(0)
    n_used = nused_ref[0]
    slot = lax.rem(b, 2)

    def gather(t_ref, sl, j):
        src = pl.ds(pl.multiple_of(t_ref[0, j] * RUN, RUN), RUN)
        return pltpu.make_async_copy(xs_ref.at[src], xbuf_ref.at[sl, pl.ds(j * RUN, RUN)], gsem.at[sl])

    def scatter(t_ref, sl, j):
        dst = pl.ds(pl.multiple_of(t_ref[0, BLOCK_PIECES + j] * RUN, RUN), RUN)
        return pltpu.make_async_copy(ybuf_ref.at[sl, pl.ds(j * RUN, RUN)], yb_ref.at[dst], ssem.at[sl])

    def zero_copy(rows):
        return pltpu.make_async_copy(zero_ref.at[pl.ds(0, rows.size)], yb_ref.at[rows], zsem)

    def zero_unwritten(fn):
        for half in range(2):
            fn(zero_copy(pl.ds(n_tiles * ns + half * MOE_ROWS, MOE_ROWS)))

        def per_tile(i, carry):
            n = tail_ref[i]
            first = (i + 1) * ns - n * RUN
            _binary_pieces(n, lambda off, size: fn(zero_copy(
                pl.ds(pl.multiple_of(first + off * RUN, RUN), size * RUN))))
            return carry
        lax.fori_loop(0, n_tiles, per_tile, 0)

    @pl.when(b == 0)
    def _():
        zero_ref[...] = jnp.zeros_like(zero_ref)
        zero_unwritten(lambda cp: cp.start())
        zero_unwritten(lambda cp: cp.wait())
        for j in range(BLOCK_PIECES):
            gather(tab_ref, 0, j).start(priority=j % 2)

    @pl.when(b + 1 < n_used)
    def _():
        for j in range(BLOCK_PIECES):
            gather(next_tab_ref, 1 - slot, j).start(priority=j % 2)

    new_expert = jnp.logical_or(b == 0, bexp_ref[b] != bexp_ref[jnp.maximum(b - 1, 0)])

    @pl.when(new_expert)
    def _():
        wgb_ref[...] = wg_ref[...].astype(BF16)
        wub_ref[...] = wu_ref[...].astype(BF16)
        wdb_ref[...] = wd_ref[...].astype(BF16)

    @pl.when(b < n_used)
    def _():
        for j in range(BLOCK_PIECES):
            gather(tab_ref, slot, j).wait()

        @pl.when(b >= 2)
        def _():
            for j in range(BLOCK_PIECES):
                scatter(tab_ref, slot, j).wait()

        xb = xbuf_ref[slot].astype(BF16)
        g = jnp.dot(xb, wgb_ref[...], preferred_element_type=F32) + bg_ref[...]
        u = jnp.dot(xb, wub_ref[...], preferred_element_type=F32) + bu_ref[...]
        g = jnp.minimum(g, SWIGLU_LIMIT)
        u = jnp.clip(u, -SWIGLU_LIMIT, SWIGLU_LIMIT)
        act = (u + 1.0) * g * jax.nn.sigmoid(SWIGLU_ALPHA * g)
        ybuf_ref[slot] = (jnp.dot(act.astype(BF16), wdb_ref[...], preferred_element_type=F32)
                          + bd_ref[...])
        for j in range(BLOCK_PIECES):
            scatter(tab_ref, slot, j).start(priority=j % 2)

    @pl.when(b == n_used - 1)
    def _():
        for j in range(BLOCK_PIECES):
            scatter(tab_ref, slot, j).wait()

        @pl.when(b >= 1)
        def _():
            for j in range(BLOCK_PIECES):
                scatter(tab_ref, 1 - slot, j).wait()


def _experts(block_exp, n_used, tail, tab, xs, wg, bg, wu, bu, wd, bd, ns):
    d, de = wg.shape[1], wg.shape[2]
    n_blocks = tab.shape[0]
    n_tiles = xs.shape[0] // ns
    assert tab.shape[2] == 2 * BLOCK_PIECES == V7X_LANES
    wmap = lambda b, be, nu, tl: (be[b], 0, 0)
    smem_tab = lambda f: pl.BlockSpec((None, 1, V7X_LANES), f, memory_space=pltpu.SMEM)
    any_spec = pl.BlockSpec(memory_space=pl.ANY)
    return pl.pallas_call(
        functools.partial(_experts_kernel, ns=ns, n_tiles=n_tiles),
        grid_spec=pltpu.PrefetchScalarGridSpec(
            num_scalar_prefetch=3,
            grid=(n_blocks,),
            in_specs=[
                smem_tab(lambda b, be, nu, tl: (b, 0, 0)),
                smem_tab(lambda b, be, nu, tl: (jnp.minimum(b + 1, n_blocks - 1), 0, 0)),
                any_spec,
                pl.BlockSpec((None, d, de), wmap), pl.BlockSpec((None, 1, de), wmap),
                pl.BlockSpec((None, d, de), wmap), pl.BlockSpec((None, 1, de), wmap),
                pl.BlockSpec((None, de, d), wmap), pl.BlockSpec((None, 1, d), wmap),
            ],
            out_specs=any_spec,
            scratch_shapes=[
                pltpu.VMEM((d, de), BF16), pltpu.VMEM((d, de), BF16), pltpu.VMEM((de, d), BF16),
                pltpu.VMEM((2, MOE_ROWS, d), F32), pltpu.VMEM((2, MOE_ROWS, d), F32),
                pltpu.VMEM((MOE_ROWS, d), F32),
                pltpu.SemaphoreType.DMA((2,)), pltpu.SemaphoreType.DMA((2,)), pltpu.SemaphoreType.DMA,
            ],
        ),
        out_shape=jax.ShapeDtypeStruct((n_tiles * ns + 2 * MOE_ROWS, d), F32),
        compiler_params=_cparams("arbitrary"),
        name="moe_experts",
    )(block_exp, n_used, tail, tab, tab, xs, wg, bg, wu, bu, wd, bd)


def _combine_kernel(extra_ref, route_ref, x1_ref, yb_ref, o_ref, sorted_ref, sems, *, tm, ns):
    i = pl.program_id(0)
    slot = lax.rem(i, 2)
    base_rows = tm * TOP_K

    def copies(tile, sl, fn):
        def rows_copy(first, n_rows):
            src = pl.ds(pl.multiple_of(tile * ns + first, RUN), n_rows)
            dst = pl.ds(pl.multiple_of(first, RUN), n_rows)
            return pltpu.make_async_copy(yb_ref.at[src], sorted_ref.at[sl, dst], sems.at[sl])

        fn(rows_copy(0, base_rows))
        _binary_pieces(extra_ref[tile],
                       lambda off, size: fn(rows_copy(base_rows + off * RUN, size * RUN)))

    @pl.when(i == 0)
    def _():
        sorted_ref[...] = jnp.zeros_like(sorted_ref)
        copies(i, slot, lambda cp: cp.start())

    @pl.when(i + 1 < pl.num_programs(0))
    def _():
        copies(i + 1, 1 - slot, lambda cp: cp.start())

    copies(i, slot, lambda cp: cp.wait())

    ys = sorted_ref[slot].astype(BF16)
    route = route_ref[...]
    q = lax.broadcasted_iota(jnp.int32, (tm, ns), 1).astype(F32)
    wmat = jnp.zeros((tm, ns), F32)
    for k in range(TOP_K):
        wmat = wmat + jnp.where(q == route[:, k:k + 1], route[:, TOP_K + k:TOP_K + k + 1], 0.0)
    o_ref[...] = x1_ref[...] + jnp.dot(wmat.astype(BF16), ys, preferred_element_type=F32)


def _combine(extra, route, x1, yb, tm, ns):
    t, d = x1.shape
    return pl.pallas_call(
        functools.partial(_combine_kernel, tm=tm, ns=ns),
        grid=(t // tm,),
        in_specs=[
            pl.BlockSpec(memory_space=pltpu.SMEM),
            pl.BlockSpec((tm, V7X_LANES), lambda i: (i, 0)),
            pl.BlockSpec((tm, d), lambda i: (i, 0)),
            pl.BlockSpec(memory_space=pl.ANY),
        ],
        out_specs=pl.BlockSpec((tm, d), lambda i: (i, 0)),
        out_shape=jax.ShapeDtypeStruct((t, d), F32),
        scratch_shapes=[pltpu.VMEM((2, ns, d), F32), pltpu.SemaphoreType.DMA((2,))],
        compiler_params=_cparams("arbitrary"),
        name="moe_combine",
    )(extra, route, x1, yb)


def _head_indicator(width, head_dim):
    lane_head = jnp.arange(width) // head_dim
    return (lane_head[:, None] == jnp.arange(V7X_LANES)[None, :]).astype(BF16)


def _layer(x, mem, norm_mix, w_in, a_q_gain, a_k_gain, a_rel_bias, conv_w, conv_b, dt_bias, a_log,
           d_skip, ssm_norm, norm_mem, w_mem_kv, x_q_gain, x_k_gain, w_br_a, w_br_b, w_br_c, w_out,
           norm_ffn, w_router, b_router, w_gate, b_gate, w_up, b_up, w_down, b_down):
    b, s, d = x.shape
    t = b * s
    a_width = A_HEADS * A_HEAD_DIM
    inner = SSM_HEADS * SSM_HEAD_DIM
    gs = SSM_GROUPS * SSM_STATE
    x_width = X_HEADS * X_HEAD_DIM
    assert d == a_width == x_width and inner == 2 * d and 2 * gs == d

    o_dt = 3 * a_width + inner + inner + 2 * gs
    assert o_dt % d == 0
    w_a = w_in[:, :o_dt].astype(BF16)
    w_b = w_in[:, o_dt + SSM_HEADS:].astype(BF16)
    plain2 = ("plain", "plain")
    roles = (("qa", "ka"), plain2, plain2, plain2, ("qx", "sig"), ("sig", "sig"))
    w_dt = jnp.pad(w_in[:, o_dt:o_dt + SSM_HEADS], ((0, 0), (0, V7X_LANES - SSM_HEADS))).astype(BF16)
    gains = jnp.zeros((V7X_SUBLANES, d), F32)
    gains = gains.at[0].set(jnp.tile(a_q_gain, A_HEADS) * (A_HEAD_DIM ** -0.5 * LOG2_E))
    gains = gains.at[1].set(jnp.tile(a_k_gain, A_HEADS))
    gains = gains.at[2].set(jnp.tile(x_q_gain, X_HEADS) * X_HEAD_DIM ** -0.5)

    x2d = x.reshape(t, d)
    proj, dt_raw = _in_proj(x2d, norm_mix.reshape(1, d), w_a, w_b, w_dt, gains, roles,
                            tm=min(1024, t), tn=2 * d)
    proj3 = proj.reshape(b, s, proj.shape[1])

    y_a = _attention(proj3, _attn_bias(a_rel_bias), q_tile=0, k_tile=1, v_tile=2)

    pad_h = lambda v: jnp.pad(v.astype(F32), (0, V7X_LANES - SSM_HEADS)).reshape(1, V7X_LANES)
    e_mat = _head_indicator(inner, SSM_HEAD_DIM).T
    y_b = _ssd(proj3, dt_raw.reshape(b, s, V7X_LANES),
               conv_w[:, :inner], conv_b[:inner].reshape(1, inner),
               conv_w[:, inner:], conv_b[inner:].reshape(1, 2 * gs),
               pad_h(dt_bias), pad_h(-jnp.exp(a_log.astype(F32))),
               jnp.repeat(d_skip.astype(F32), SSM_HEAD_DIM).reshape(1, inner),
               ssm_norm.reshape(1, inner), e_mat, z_tile=3, x_tile=5, bc_tile=7)

    k_mem, v_mem = _mem_kv(mem, norm_mem.reshape(1, d), w_mem_kv.astype(BF16),
                           x_k_gain.reshape(1, X_HEAD_DIM))
    y_c = _mem_attn(proj3, k_mem, v_mem, q_blk=8, tq=min(512, s))

    w_r = jnp.pad(w_router, ((0, 0), (0, V7X_LANES - N_EXPERTS)))
    b_r = jnp.pad(b_router, (0, V7X_LANES - N_EXPERTS)).reshape(1, V7X_LANES)
    tm_moe = min(MOE_TILE, t)
    n_tiles = t // tm_moe
    x1, h2, route, tile_cnt = _merge(
        x2d, y_a.reshape(t, a_width), y_b.reshape(t, inner), y_c.reshape(t, x_width), proj,
        w_br_a.astype(BF16), w_br_b.astype(BF16), w_br_c.astype(BF16), w_out.astype(BF16),
        norm_ffn.reshape(1, d), w_r, b_r, gate_blk=9, tm=min(MERGE_TM, t), rt=tm_moe)

    ns = -(-(tm_moe * TOP_K + N_EXPERTS * (RUN - 1)) // V7X_LANES) * V7X_LANES
    units = ns // RUN
    n_blocks = -(-(t * TOP_K + n_tiles * N_EXPERTS * (RUN - 1) + N_EXPERTS * (MOE_ROWS - 1))
                 // MOE_ROWS)
    n8 = tile_cnt[:, 0, :N_EXPERTS].astype(jnp.int32) // RUN
    total = jnp.sum(n8, axis=0)
    padded = (total + BLOCK_PIECES - 1) // BLOCK_PIECES * BLOCK_PIECES
    pad_ends = jnp.cumsum(padded)
    pad_starts = pad_ends - padded
    n_used = (pad_ends[-1] // BLOCK_PIECES).reshape(1).astype(jnp.int32)
    blk = jnp.minimum(jnp.arange(n_blocks, dtype=jnp.int32), n_used[0] - 1)
    block_exp = jnp.minimum(jnp.sum(pad_ends[None, :] <= (blk * BLOCK_PIECES)[:, None], axis=1),
                            N_EXPERTS - 1).astype(jnp.int32)
    slot_j = jnp.arange(BLOCK_PIECES, dtype=jnp.int32)[None, :]
    onehot_e = (block_exp[:, None] == jnp.arange(N_EXPERTS)[None, :]).astype(jnp.int32)
    q = blk[:, None] * BLOCK_PIECES + slot_j - (onehot_e @ pad_starts)[:, None]
    real = q < (onehot_e @ total)[:, None]
    ends_b = onehot_e @ jnp.cumsum(n8, axis=0).T
    tile_of = jnp.minimum(jnp.sum(ends_b[:, None, :] <= q[:, :, None], axis=-1), n_tiles - 1)
    tile_1h = (tile_of[:, :, None] == jnp.arange(n_tiles)[None, None, :]).astype(jnp.int32)
    starts_b = ends_b - onehot_e @ n8.T
    in_tile_b = onehot_e @ (jnp.cumsum(n8, axis=1) - n8).T
    piece = (tile_of * units + jnp.sum(tile_1h * (in_tile_b - starts_b)[:, None, :], axis=-1) + q)
    zero_piece = (tm_moe * TOP_K + N_EXPERTS * (RUN - 1)) // RUN
    spare = n_tiles * units + (blk % 2)[:, None] * BLOCK_PIECES + slot_j
    tab = jnp.concatenate([jnp.where(real, piece, zero_piece), jnp.where(real, piece, spare)], axis=1)
    tab = tab.reshape(n_blocks, 1, 2 * BLOCK_PIECES).astype(jnp.int32)
    used = jnp.sum(n8, axis=1)
    tail = (units - used).astype(jnp.int32)
    extra = (used - tm_moe * TOP_K // RUN).astype(jnp.int32)

    xs = _dispatch(h2, route, tm_moe, ns)
    yb = _experts(block_exp, n_used, tail, tab, xs,
                  w_gate, b_gate.reshape(N_EXPERTS, 1, -1),
                  w_up, b_up.reshape(N_EXPERTS, 1, -1),
                  w_down, b_down.reshape(N_EXPERTS, 1, -1), ns)
    out = _combine(extra, route, x1, yb, tm_moe, ns)
    return out.reshape(b, s, d)


def kernel(x, mem, norm_mix, w_in, a_q_gain, a_k_gain, a_rel_bias, conv_w, conv_b, dt_bias, a_log, d_skip, ssm_norm, norm_mem, w_mem_kv, x_q_gain, x_k_gain, w_br_a, w_br_b, w_br_c, w_out, norm_ffn, w_router, b_router, w_gate, b_gate, w_up, b_up, w_down, b_down):
    for l in range(norm_mix.shape[0]):
        x = _layer(x, mem, norm_mix[l], w_in[l], a_q_gain[l], a_k_gain[l], a_rel_bias[l], conv_w[l],
                   conv_b[l], dt_bias[l], a_log[l], d_skip[l], ssm_norm[l], norm_mem[l], w_mem_kv[l],
                   x_q_gain[l], x_k_gain[l], w_br_a[l], w_br_b[l], w_br_c[l], w_out[l], norm_ffn[l],
                   w_router[l], b_router[l], w_gate[l], b_gate[l], w_up[l], b_up[l], w_down[l],
                   b_down[l])
    return x
```

```python
import functools

import jax
import jax.numpy as jnp
from jax import lax
from jax.experimental import pallas as pl
from jax.experimental.pallas import tpu as pltpu

F32 = jnp.float32
BF16 = jnp.bfloat16
HIGHEST = lax.Precision.HIGHEST

V7X_LANES = 128
V7X_SUBLANES = 8
V7X_VMEM_LIMIT_BYTES = 56 * 1024 * 1024

EPS = 1e-6
LOG2_E = 1.4426950408889634
NEG = -1e30

CHUNK = 64
A_HEADS = 16
A_HEAD_DIM = 64
LEFT_CHUNKS = 8
REL_CLIP = 128
SSM_HEADS = 32
SSM_HEAD_DIM = 64
SSM_GROUPS = 4
SSM_STATE = 128
CONV_WIDTH = 4
X_HEADS = 4
X_HEAD_DIM = 256
N_EXPERTS = 32
TOP_K = 4
SWIGLU_LIMIT = 7.0
SWIGLU_ALPHA = 1.702

ATTN_TQ = 256
SSD_L = 256
MOE_ROWS = 512
MOE_TILE = 512
MERGE_TM = 512


def _cparams(*sem):
    return pltpu.CompilerParams(dimension_semantics=sem, vmem_limit_bytes=V7X_VMEM_LIMIT_BYTES)


def _split_bf16(v):
    hi = v.astype(BF16)
    lo = (v - hi.astype(F32)).astype(BF16)
    return hi, lo


def _in_proj_kernel(x_ref, nw_ref, wa_ref, wb_ref, wdt_ref, gains_ref, o_ref, dt_ref, h_ref, *,
                    roles, steps_a):
    j = pl.program_id(1)

    @pl.when(j == 0)
    def _():
        x = x_ref[...]
        ms = jnp.mean(x * x, axis=-1, keepdims=True)
        hb = (x * lax.rsqrt(ms + EPS) * nw_ref[...]).astype(BF16)
        h_ref[...] = hb
        dt_ref[...] = jnp.dot(hb, wdt_ref[...], preferred_element_type=F32)

    def head_norm(acc, head_dim, gain_row):
        outs = []
        if head_dim >= V7X_LANES:
            for c0 in range(0, acc.shape[1], head_dim):
                blk = acc[:, c0:c0 + head_dim]
                s = jnp.sum(blk * blk, axis=-1, keepdims=True)
                scale = lax.rsqrt(s * (1.0 / head_dim) + EPS)
                outs.append(blk * scale * gain_row[:, c0:c0 + head_dim])
        else:
            assert 2 * head_dim == V7X_LANES
            lo = lax.broadcasted_iota(jnp.int32, (1, V7X_LANES), 1) < head_dim
            for c0 in range(0, acc.shape[1], V7X_LANES):
                blk = acc[:, c0:c0 + V7X_LANES]
                sq = blk * blk
                s_lo = jnp.sum(jnp.where(lo, sq, 0.0), axis=-1, keepdims=True)
                s_hi = jnp.sum(jnp.where(lo, 0.0, sq), axis=-1, keepdims=True)
                scale = jnp.where(lo, lax.rsqrt(s_lo * (1.0 / head_dim) + EPS),
                                  lax.rsqrt(s_hi * (1.0 / head_dim) + EPS))
                outs.append(blk * scale * gain_row[:, c0:c0 + V7X_LANES])
        return jnp.concatenate(outs, axis=1)

    def cond_for(role):
        c = None
        for jj, r in enumerate(roles):
            if r == role:
                c = (j == jj) if c is None else jnp.logical_or(c, j == jj)
        return c

    def epilogue(acc, role):
        if role == "qa":
            return head_norm(acc, A_HEAD_DIM, gains_ref[0:1, :])
        if role == "ka":
            return head_norm(acc, A_HEAD_DIM, gains_ref[1:2, :])
        if role == "qx":
            return head_norm(acc, X_HEAD_DIM, gains_ref[2:3, :])
        if role == "sig":
            return 0.5 * jnp.tanh(0.5 * acc) + 0.5
        return acc

    d = x_ref.shape[1]
    for step_roles in sorted(set(roles)):
        in_a = {jj < steps_a for jj, r in enumerate(roles) if r == step_roles}
        assert len(in_a) == 1
        w_ref = wa_ref if in_a.pop() else wb_ref

        @pl.when(cond_for(step_roles))
        def _(step_roles=step_roles, w_ref=w_ref):
            acc = jnp.dot(h_ref[...], w_ref[...], preferred_element_type=F32)
            for k, role in enumerate(step_roles):
                cs = slice(k * d, (k + 1) * d)
                o_ref[:, cs] = epilogue(acc[:, cs], role).astype(o_ref.dtype)


def _in_proj(x2d, norm_w, w_a, w_b, w_dt, gains, roles, tm, tn):
    t, d = x2d.shape
    n = w_a.shape[1] + w_b.shape[1]
    assert t % tm == 0 and w_a.shape[1] % tn == 0 and w_b.shape[1] % tn == 0
    assert len(roles) == n // tn
    assert all(len(r) * d == tn for r in roles)
    steps_a, steps_b = w_a.shape[1] // tn, w_b.shape[1] // tn
    const = lambda i, j: (0, 0)
    return pl.pallas_call(
        functools.partial(_in_proj_kernel, roles=roles, steps_a=steps_a),
        grid=(t // tm, n // tn),
        in_specs=[
            pl.BlockSpec((tm, d), lambda i, j: (i, 0)),
            pl.BlockSpec((1, d), const),
            pl.BlockSpec((d, tn), lambda i, j: (0, jnp.minimum(j, steps_a - 1))),
            pl.BlockSpec((d, tn), lambda i, j: (0, jnp.where(j < steps_a, steps_b - 1, j - steps_a))),
            pl.BlockSpec((d, V7X_LANES), const),
            pl.BlockSpec(gains.shape, const),
        ],
        out_specs=[
            pl.BlockSpec((tm, tn), lambda i, j: (i, j)),
            pl.BlockSpec((tm, V7X_LANES), lambda i, j: (i, 0)),
        ],
        out_shape=[
            jax.ShapeDtypeStruct((t, n), BF16),
            jax.ShapeDtypeStruct((t, V7X_LANES), F32),
        ],
        scratch_shapes=[pltpu.VMEM((tm, d), BF16)],
        compiler_params=_cparams("parallel", "arbitrary"),
        name="in_proj",
    )(x2d, norm_w, w_a, w_b, w_dt, gains)


def _attn_kernel(q_ref, *refs, tq, nprev):
    k_refs = refs[:nprev + 1]
    v_refs = refs[nprev + 1:2 * nprev + 2]
    bias_ref, o_ref = refs[2 * nprev + 2:]
    qb = pl.program_id(1)
    nk = (nprev + 1) * tq
    lane = lax.broadcasted_iota(jnp.int32, (1, V7X_LANES), 1)
    col = lax.broadcasted_iota(jnp.int32, (1, nk), 1)
    before_start = col < (nprev - qb) * tq
    for hp in range(A_HEADS // 2):
        ls = slice(hp * V7X_LANES, (hp + 1) * V7X_LANES)
        q2 = q_ref[:, ls]
        kk = jnp.concatenate([r[:, ls] for r in k_refs], axis=0)
        vv = jnp.concatenate([r[:, ls] for r in v_refs], axis=0)
        outs = []
        for hh in range(2):
            sel = (lane < A_HEAD_DIM) if hh == 0 else (lane >= A_HEAD_DIM)
            qm = jnp.where(sel, q2, jnp.zeros_like(q2))
            s = lax.dot_general(qm, kk, (((1,), (1,)), ((), ())), preferred_element_type=F32)
            s = jnp.where(before_start, NEG, s + bias_ref[2 * hp + hh])
            m = jnp.max(s, axis=-1, keepdims=True)
            p = jnp.exp2(s - m)
            l = jnp.sum(p, axis=-1, keepdims=True)
            o = jnp.dot(p.astype(BF16), vv, preferred_element_type=F32)
            outs.append(o / l)
        o_ref[:, ls] = jnp.where(lane < A_HEAD_DIM, outs[0], outs[1]).astype(o_ref.dtype)


def _attention(proj3, bias, q_tile, k_tile, v_tile):
    b, s, _ = proj3.shape
    tq = ATTN_TQ
    width = A_HEADS * A_HEAD_DIM
    left = LEFT_CHUNKS * CHUNK
    assert left % tq == 0 and s % tq == 0
    nprev = left // tq

    def kv_spec(tile, back):
        return pl.BlockSpec((None, tq, width), lambda bi, qi: (bi, jnp.maximum(qi - back, 0), tile))

    in_specs = [pl.BlockSpec((None, tq, width), lambda bi, qi: (bi, qi, q_tile))]
    in_specs += [kv_spec(k_tile, nprev - i) for i in range(nprev + 1)]
    in_specs += [kv_spec(v_tile, nprev - i) for i in range(nprev + 1)]
    in_specs += [pl.BlockSpec(bias.shape, lambda bi, qi: (0, 0, 0), pipeline_mode=pl.Buffered(1))]
    return pl.pallas_call(
        functools.partial(_attn_kernel, tq=tq, nprev=nprev),
        grid=(b, s // tq),
        in_specs=in_specs,
        out_specs=pl.BlockSpec((None, tq, width), lambda bi, qi: (bi, qi, 0)),
        out_shape=jax.ShapeDtypeStruct((b, s, width), BF16),
        compiler_params=_cparams("parallel", "parallel"),
        name="chunk_attn",
    )(proj3, *([proj3] * (2 * nprev + 2)), bias)


def _attn_bias_kernel(v_ref, o_ref, *, tq, nk):
    x = jnp.broadcast_to(v_ref[...], (tq, v_ref.shape[-1]))
    toeplitz = pltpu.roll(x, 0, 1, stride=1, stride_axis=0)[:, :nk]
    qc = lax.broadcasted_iota(jnp.int32, (tq, nk), 0) // CHUNK
    kc = lax.broadcasted_iota(jnp.int32, (tq, nk), 1) // CHUNK
    in_band = jnp.where(kc >= qc, kc - qc, LEFT_CHUNKS + 1) <= LEFT_CHUNKS
    o_ref[...] = jnp.where(in_band, toeplitz * LOG2_E, NEG)


def _attn_bias(rel_bias):
    tq = ATTN_TQ
    left = LEFT_CHUNKS * CHUNK
    nk = left + tq
    m_len = 1 << (tq + nk - 1).bit_length()
    m = jnp.arange(m_len)
    diff = jnp.where(m < nk, m, m - m_len)
    v = rel_bias[:, jnp.clip(left - diff, -REL_CLIP, REL_CLIP) + REL_CLIP].astype(F32)
    h = v.shape[0]
    return pl.pallas_call(
        functools.partial(_attn_bias_kernel, tq=tq, nk=nk),
        grid=(h,),
        in_specs=[pl.BlockSpec((None, 1, m_len), lambda i: (i, 0, 0))],
        out_specs=pl.BlockSpec((None, tq, nk), lambda i: (i, 0, 0)),
        out_shape=jax.ShapeDtypeStruct((h, tq, nk), F32),
        compiler_params=_cparams("parallel"),
        name="attn_bias",
    )(v.reshape(h, 1, m_len))


def _ssd_kernel(z0_ref, z1_ref, x0_ref, x1_ref, bc_ref, dt_ref, cwx_ref, cbx_ref, cwbc_ref, cbbc_ref,
                dtb_ref, aneg_ref, dskip_ref, gain_ref, e_ref, o_ref,
                xf_ref, bcf_ref, st_ref, xs_ref, y_ref, *, L):
    c = pl.program_id(1)
    inner = SSM_HEADS * SSM_HEAD_DIM
    gw = inner // SSM_GROUPS
    gs = SSM_GROUPS * SSM_STATE
    tail = V7X_SUBLANES
    half = inner // 2

    @pl.when(c == 0)
    def _():
        xf_ref[0:tail, :] = jnp.zeros((tail, inner), F32)
        bcf_ref[0:tail, :] = jnp.zeros((tail, 2 * gs), F32)
        st_ref[...] = jnp.zeros_like(st_ref)

    xf_ref[tail:, 0:half] = x0_ref[...].astype(F32)
    xf_ref[tail:, half:inner] = x1_ref[...].astype(F32)
    bcf_ref[tail:, :] = bc_ref[...].astype(F32)

    def conv_silu(src_ref, w_ref, b_ref, c0, c1):
        acc = b_ref[:, c0:c1] + w_ref[CONV_WIDTH - 1:CONV_WIDTH, c0:c1] * src_ref[tail:tail + L, c0:c1]
        for k in range(1, CONV_WIDTH):
            acc = acc + (w_ref[CONV_WIDTH - 1 - k:CONV_WIDTH - k, c0:c1]
                         * src_ref[tail - k:tail - k + L, c0:c1])
        return acc * jax.nn.sigmoid(acc)

    for g in range(SSM_GROUPS):
        xs_ref[:, g * gw:(g + 1) * gw] = conv_silu(xf_ref, cwx_ref, cbx_ref, g * gw, (g + 1) * gw)
    bmat = conv_silu(bcf_ref, cwbc_ref, cbbc_ref, 0, gs)
    cmat = conv_silu(bcf_ref, cwbc_ref, cbbc_ref, gs, 2 * gs)
    xf_ref[0:tail, :] = xf_ref[L:L + tail, :]
    bcf_ref[0:tail, :] = bcf_ref[L:L + tail, :]

    pre = dt_ref[...] + dtb_ref[...]
    dt = jnp.maximum(pre, 0.0) + jnp.log1p(jnp.exp(-jnp.abs(pre)))
    a = dt * aneg_ref[...]
    row = lax.broadcasted_iota(jnp.int32, (L, L), 0)
    colm = lax.broadcasted_iota(jnp.int32, (L, L), 1)
    lower = colm <= row
    tri = jnp.where(lower, 1.0, 0.0).astype(F32)
    cs = jnp.dot(tri, a, precision=HIGHEST, preferred_element_type=F32)
    cs2 = cs * LOG2_E
    src_t = (cs2 - jnp.log2(dt)).T
    cs_last = cs[L - 1:L, :]
    w_state = dt * jnp.exp(cs_last - cs)
    e_cs = jnp.exp(cs)
    chunk_decay = jnp.broadcast_to(jnp.exp(cs_last), (tail, V7X_LANES))
    stacked = jnp.concatenate([w_state, e_cs, chunk_decay], axis=0)
    s_hi, s_lo = _split_bf16(stacked)
    expanded = (jnp.dot(s_hi, e_ref[...], preferred_element_type=F32)
                + jnp.dot(s_lo, e_ref[...], preferred_element_type=F32))
    w_state_e = expanded[0:L]
    e_cs_e = expanded[L:2 * L]
    decay_e = expanded[2 * L:2 * L + 1]

    lane = lax.broadcasted_iota(jnp.int32, (1, V7X_LANES), 1)
    pairs_per_group = gw // V7X_LANES
    for g in range(SSM_GROUPS):
        bg = bmat[:, g * SSM_STATE:(g + 1) * SSM_STATE]
        cg = cmat[:, g * SSM_STATE:(g + 1) * SSM_STATE].astype(BF16)
        cb = lax.dot_general(cg, bg.astype(BF16), (((1,), (1,)), ((), ())),
                             preferred_element_type=F32)
        state_b = st_ref[g].astype(BF16)
        y_off = jnp.dot(cg, state_b, preferred_element_type=F32) * e_cs_e[:, g * gw:(g + 1) * gw]
        for pr in range(pairs_per_group):
            c0 = g * gw + pr * V7X_LANES
            xp = xs_ref[:, c0:c0 + V7X_LANES]
            xpb = xp.astype(BF16)
            acc = y_off[:, pr * V7X_LANES:(pr + 1) * V7X_LANES] + dskip_ref[:, c0:c0 + V7X_LANES] * xp
            for hh in range(2):
                h = c0 // SSM_HEAD_DIM + hh
                d = cs2[:, h:h + 1] - src_t[h:h + 1, :]
                m = cb * jnp.exp2(jnp.where(lower, d, NEG))
                sel = (lane < SSM_HEAD_DIM) if hh == 0 else (lane >= SSM_HEAD_DIM)
                xm = jnp.where(sel, xpb, jnp.zeros_like(xpb))
                acc = acc + jnp.dot(m.astype(BF16), xm, preferred_element_type=F32)
            y_ref[:, c0:c0 + V7X_LANES] = acc
        xw = (xs_ref[:, g * gw:(g + 1) * gw] * w_state_e[:, g * gw:(g + 1) * gw]).astype(BF16)
        new = jnp.dot(bg.T.astype(BF16), xw, preferred_element_type=F32)
        st_ref[g] = st_ref[g] * decay_e[:, g * gw:(g + 1) * gw] + new

    for g in range(SSM_GROUPS):
        sl = slice(g * gw, (g + 1) * gw)
        z_ref, z0 = (z0_ref, g * gw) if g * gw < half else (z1_ref, g * gw - half)
        zz = z_ref[:, z0:z0 + gw].astype(F32)
        yz = y_ref[:, sl] * (zz * jax.nn.sigmoid(zz))
        ms = jnp.mean(yz * yz, axis=-1, keepdims=True)
        o_ref[:, sl] = (yz * lax.rsqrt(ms + EPS) * gain_ref[:, sl]).astype(o_ref.dtype)


def _ssd(proj3, dt3, cwx, cbx, cwbc, cbbc, dtb, aneg, dskip_e, gain, e_mat, z_tile, x_tile, bc_tile):
    b, s, _ = proj3.shape
    L = SSD_L
    assert s % L == 0
    inner = SSM_HEADS * SSM_HEAD_DIM
    gs2 = 2 * SSM_GROUPS * SSM_STATE
    assert gs2 == inner // 2
    const = lambda bi, ci: (0, 0)
    full = lambda a: pl.BlockSpec(a.shape, const)
    tile = lambda k: pl.BlockSpec((None, L, gs2), lambda bi, ci: (bi, ci, k))
    return pl.pallas_call(
        functools.partial(_ssd_kernel, L=L),
        grid=(b, s // L),
        in_specs=[
            tile(z_tile), tile(z_tile + 1), tile(x_tile), tile(x_tile + 1), tile(bc_tile),
            pl.BlockSpec((None, L, V7X_LANES), lambda bi, ci: (bi, ci, 0)),
            full(cwx), full(cbx), full(cwbc), full(cbbc), full(dtb), full(aneg), full(dskip_e),
            full(gain), full(e_mat),
        ],
        out_specs=pl.BlockSpec((None, L, inner), lambda bi, ci: (bi, ci, 0)),
        out_shape=jax.ShapeDtypeStruct((b, s, inner), BF16),
        scratch_shapes=[
            pltpu.VMEM((L + V7X_SUBLANES, inner), F32),
            pltpu.VMEM((L + V7X_SUBLANES, gs2), F32),
            pltpu.VMEM((SSM_GROUPS, SSM_STATE, inner // SSM_GROUPS), F32),
            pltpu.VMEM((L, inner), F32),
            pltpu.VMEM((L, inner), F32),
        ],
        compiler_params=_cparams("parallel", "arbitrary"),
        name="ssd",
    )(proj3, proj3, proj3, proj3, proj3, dt3, cwx, cbx, cwbc, cbbc, dtb, aneg, dskip_e, gain, e_mat)


def _mem_kv_kernel(mem_ref, g_ref, w_ref, kg_ref, k_ref, v_ref):
    m = mem_ref[...]
    ms = jnp.mean(m * m, axis=-1, keepdims=True)
    mn = (m * lax.rsqrt(ms + EPS) * g_ref[...]).astype(BF16)
    kv = jnp.dot(mn, w_ref[...], preferred_element_type=F32)
    width = X_HEADS * X_HEAD_DIM
    for h in range(X_HEADS):
        sl = slice(h * X_HEAD_DIM, (h + 1) * X_HEAD_DIM)
        kh = kv[:, sl]
        r = lax.rsqrt(jnp.mean(kh * kh, axis=-1, keepdims=True) + EPS)
        k_ref[:, sl] = (kh * r * kg_ref[...]).astype(k_ref.dtype)
    v_ref[...] = kv[:, width:].astype(v_ref.dtype)


def _mem_kv(mem, norm_mem, w_kv, k_gain):
    b, m, d = mem.shape
    width = X_HEADS * X_HEAD_DIM
    const = lambda bi: (0, 0)
    return pl.pallas_call(
        _mem_kv_kernel,
        grid=(b,),
        in_specs=[
            pl.BlockSpec((None, m, d), lambda bi: (bi, 0, 0)),
            pl.BlockSpec((1, d), const),
            pl.BlockSpec((d, 2 * width), const),
            pl.BlockSpec((1, X_HEAD_DIM), const),
        ],
        out_specs=[pl.BlockSpec((None, m, width), lambda bi: (bi, 0, 0))] * 2,
        out_shape=[jax.ShapeDtypeStruct((b, m, width), BF16)] * 2,
        compiler_params=_cparams("parallel"),
        name="mem_kv",
    )(mem, norm_mem, w_kv, k_gain)


def _mem_attn_kernel(q_ref, k_ref, v_ref, o_ref):
    for h in range(X_HEADS):
        sl = slice(h * X_HEAD_DIM, (h + 1) * X_HEAD_DIM)
        s = lax.dot_general(q_ref[:, sl], k_ref[:, sl], (((1,), (1,)), ((), ())),
                            preferred_element_type=F32)
        m = jnp.max(s, axis=-1, keepdims=True)
        p = jnp.exp(s - m)
        l = jnp.sum(p, axis=-1, keepdims=True)
        o = jnp.dot(p.astype(BF16), v_ref[:, sl], preferred_element_type=F32)
        o_ref[:, sl] = (o / l).astype(o_ref.dtype)


def _mem_attn(proj3, k, v, q_blk, tq):
    b, s, _ = proj3.shape
    m = k.shape[1]
    width = X_HEADS * X_HEAD_DIM
    assert s % tq == 0
    return pl.pallas_call(
        _mem_attn_kernel,
        grid=(b, s // tq),
        in_specs=[
            pl.BlockSpec((None, tq, width), lambda bi, qi: (bi, qi, q_blk)),
            pl.BlockSpec((None, m, width), lambda bi, qi: (bi, 0, 0)),
            pl.BlockSpec((None, m, width), lambda bi, qi: (bi, 0, 0)),
        ],
        out_specs=pl.BlockSpec((None, tq, width), lambda bi, qi: (bi, qi, 0)),
        out_shape=jax.ShapeDtypeStruct((b, s, width), BF16),
        compiler_params=_cparams("parallel", "parallel"),
        name="mem_attn",
    )(proj3, k, v)


def _merge_kernel(x_ref, ya_ref, yb_ref, yc_ref, g0_ref, g1_ref, g2_ref, wa_ref, wb_ref, wc_ref,
                  wo_ref, nf_ref, wrh_ref, wrl_ref, br_ref,
                  x1_ref, h2_ref, route_ref, cnt_ref, *, tm, rt):
    merged = (g0_ref[...].astype(F32) * jnp.dot(ya_ref[...], wa_ref[...], preferred_element_type=F32)
              + g1_ref[...].astype(F32) * jnp.dot(yb_ref[...], wb_ref[...], preferred_element_type=F32)
              + g2_ref[...].astype(F32) * jnp.dot(yc_ref[...], wc_ref[...], preferred_element_type=F32))
    x1 = x_ref[...] + jnp.dot(merged.astype(BF16), wo_ref[...], preferred_element_type=F32)
    x1_ref[...] = x1
    ms = jnp.mean(x1 * x1, axis=-1, keepdims=True)
    h2 = x1 * lax.rsqrt(ms + EPS) * nf_ref[...]
    h2_ref[...] = h2.astype(h2_ref.dtype)

    h_hi, h_lo = _split_bf16(h2)
    logits_all = (jnp.dot(h_hi, wrh_ref[...], preferred_element_type=F32)
                  + jnp.dot(h_lo, wrh_ref[...], preferred_element_type=F32)
                  + jnp.dot(h_hi, wrl_ref[...], preferred_element_type=F32)) + br_ref[...]
    for sub in range(tm // rt):
        _route_tile(logits_all[sub * rt:(sub + 1) * rt], route_ref.at[pl.ds(sub * rt, rt)],
                    cnt_ref.at[sub], rt)


def _route_tile(logits, route_ref, cnt_ref, tm):
    lane = lax.broadcasted_iota(jnp.int32, (tm, V7X_LANES), 1)
    lane_f = lane.astype(F32)
    work = jnp.where(lane < N_EXPERTS, logits, NEG)
    sel_val, sel_oh = [], []
    for _ in range(TOP_K):
        mval = jnp.max(work, axis=-1, keepdims=True)
        ik = jnp.min(jnp.where(work == mval, lane_f, float(V7X_LANES)), axis=-1, keepdims=True)
        oh = lane_f == ik
        work = jnp.where(oh, NEG, work)
        sel_val.append(mval)
        sel_oh.append(oh)
    ex = [jnp.exp(v - sel_val[0]) for v in sel_val]
    denom = ex[0] + ex[1] + ex[2] + ex[3]

    oh_all = jnp.zeros((tm, V7X_LANES), F32)
    for oh in sel_oh:
        oh_all = oh_all + jnp.where(oh, 1.0, 0.0)
    row = lax.broadcasted_iota(jnp.int32, (tm, tm), 0)
    colm = lax.broadcasted_iota(jnp.int32, (tm, tm), 1)
    strict = jnp.where(colm < row, 1.0, 0.0).astype(BF16)
    before = jnp.dot(strict, oh_all.astype(BF16), preferred_element_type=F32)
    cnt = jnp.sum(oh_all, axis=0, keepdims=True)
    cnt8 = jnp.floor((cnt + (V7X_SUBLANES - 1.0)) * (1.0 / V7X_SUBLANES)) * V7X_SUBLANES
    cnt8 = jnp.broadcast_to(cnt8, (V7X_SUBLANES, V7X_LANES))
    er = lax.broadcasted_iota(jnp.int32, (V7X_LANES, V7X_LANES), 0)
    ec = lax.broadcasted_iota(jnp.int32, (V7X_LANES, V7X_LANES), 1)
    earlier = jnp.where(er < ec, 1.0, 0.0).astype(BF16)
    run_start = jnp.dot(cnt8.astype(BF16), earlier, preferred_element_type=F32)[0:1, :]
    slot = before + run_start
    route = jnp.zeros((tm, V7X_LANES), F32)
    for k in range(TOP_K):
        pos = jnp.sum(jnp.where(sel_oh[k], slot, 0.0), axis=-1, keepdims=True)
        route = jnp.where(lane == k, pos, route)
        route = jnp.where(lane == TOP_K + k, ex[k] / denom, route)
    route_ref[...] = route
    cnt_ref[...] = cnt8


def _merge(x2d, ya, yb, yc, proj, wa, wb, wc, wo, nf, wr, br, gate_blk, tm, rt):
    t, d = x2d.shape
    assert t % tm == 0 and tm % rt == 0
    wr_hi, wr_lo = _split_bf16(wr)
    const = lambda i: (0, 0)
    full = lambda a: pl.BlockSpec(a.shape, const)
    rows = lambda w: pl.BlockSpec((tm, w), lambda i: (i, 0))
    return pl.pallas_call(
        functools.partial(_merge_kernel, tm=tm, rt=rt),
        grid=(t // tm,),
        in_specs=[
            rows(d), rows(ya.shape[1]), rows(yb.shape[1]), rows(yc.shape[1]),
            pl.BlockSpec((tm, d), lambda i: (i, gate_blk)),
            pl.BlockSpec((tm, d), lambda i: (i, gate_blk + 1)),
            pl.BlockSpec((tm, d), lambda i: (i, gate_blk + 2)),
            full(wa), full(wb), full(wc), full(wo), full(nf), full(wr_hi), full(wr_lo), full(br),
        ],
        out_specs=[rows(d), rows(d), rows(V7X_LANES),
                   pl.BlockSpec((tm // rt, V7X_SUBLANES, V7X_LANES), lambda i: (i, 0, 0))],
        out_shape=[
            jax.ShapeDtypeStruct((t, d), F32),
            jax.ShapeDtypeStruct((t, d), BF16),
            jax.ShapeDtypeStruct((t, V7X_LANES), F32),
            jax.ShapeDtypeStruct((t // rt, V7X_SUBLANES, V7X_LANES), F32),
        ],
        compiler_params=_cparams("parallel"),
        name="merge_route",
    )(x2d, ya, yb, yc, proj, proj, proj, wa, wb, wc, wo, nf, wr_hi, wr_lo, br)


RUN = V7X_SUBLANES
BLOCK_PIECES = MOE_ROWS // RUN
TAIL_SIZES = (32, 16, 8, 4, 2, 1)
assert MOE_TILE * TOP_K % V7X_LANES == 0 and N_EXPERTS * (RUN - 1) < 2 * TAIL_SIZES[0] * RUN


def _binary_pieces(n, body):
    for size in TAIL_SIZES:
        @pl.when((n & size) != 0)
        def _(size=size):
            body(n & ~(2 * size - 1), size)


def _dispatch_kernel(h_ref, route_ref, xs_ref, *, tm, ns):
    pos_t = route_ref[...].T
    q = lax.broadcasted_iota(jnp.int32, (ns, tm), 0).astype(F32)
    perm = jnp.zeros((ns, tm), F32)
    for k in range(TOP_K):
        perm = perm + jnp.where(q == pos_t[k:k + 1, :], 1.0, 0.0)
    xs_ref[...] = jnp.dot(perm.astype(BF16), h_ref[...], preferred_element_type=F32)


def _dispatch(h2, route, tm, ns):
    t, d = h2.shape
    return pl.pallas_call(
        functools.partial(_dispatch_kernel, tm=tm, ns=ns),
        grid=(t // tm,),
        in_specs=[
            pl.BlockSpec((tm, d), lambda i: (i, 0)),
            pl.BlockSpec((tm, V7X_LANES), lambda i: (i, 0)),
        ],
        out_specs=pl.BlockSpec((ns, d), lambda i: (i, 0)),
        out_shape=jax.ShapeDtypeStruct((t // tm * ns, d), F32),
        compiler_params=_cparams("parallel"),
        name="moe_dispatch",
    )(h2, route)


def _experts_kernel(bexp_ref, nused_ref, tail_ref, tab_ref, next_tab_ref, xs_ref, wg_ref, bg_ref,
                    wu_ref, bu_ref, wd_ref, bd_ref, yb_ref,
                    wgb_ref, wub_ref, wdb_ref, xbuf_ref, ybuf_ref, zero_ref, gsem, ssem, zsem, *,
                    ns, n_tiles):
    b = pl.program_id(0)
    n_used = nused_ref[0]
    slot = lax.rem(b, 2)

    def gather(t_ref, sl, j):
        src = pl.ds(pl.multiple_of(t_ref[0, j] * RUN, RUN), RUN)
        return pltpu.make_async_copy(xs_ref.at[src], xbuf_ref.at[sl, pl.ds(j * RUN, RUN)], gsem.at[sl])

    def scatter(t_ref, sl, j):
        dst = pl.ds(pl.multiple_of(t_ref[0, BLOCK_PIECES + j] * RUN, RUN), RUN)
        return pltpu.make_async_copy(ybuf_ref.at[sl, pl.ds(j * RUN, RUN)], yb_ref.at[dst], ssem.at[sl])

    def zero_copy(rows):
        return pltpu.make_async_copy(zero_ref.at[pl.ds(0, rows.size)], yb_ref.at[rows], zsem)

    def zero_unwritten(fn):
        for half in range(2):
            fn(zero_copy(pl.ds(n_tiles * ns + half * MOE_ROWS, MOE_ROWS)))

        def per_tile(i, carry):
            n = tail_ref[i]
            first = (i + 1) * ns - n * RUN
            _binary_pieces(n, lambda off, size: fn(zero_copy(
                pl.ds(pl.multiple_of(first + off * RUN, RUN), size * RUN))))
            return carry
        lax.fori_loop(0, n_tiles, per_tile, 0)

    @pl.when(b == 0)
    def _():
        zero_ref[...] = jnp.zeros_like(zero_ref)
        zero_unwritten(lambda cp: cp.start())
        zero_unwritten(lambda cp: cp.wait())
        for j in range(BLOCK_PIECES):
            gather(tab_ref, 0, j).start(priority=j % 2)

    @pl.when(b + 1 < n_used)
    def _():
        for j in range(BLOCK_PIECES):
            gather(next_tab_ref, 1 - slot, j).start(priority=j % 2)

    new_expert = jnp.logical_or(b == 0, bexp_ref[b] != bexp_ref[jnp.maximum(b - 1, 0)])

    @pl.when(new_expert)
    def _():
        wgb_ref[...] = wg_ref[...].astype(BF16)
        wub_ref[...] = wu_ref[...].astype(BF16)
        wdb_ref[...] = wd_ref[...].astype(BF16)

    @pl.when(b < n_used)
    def _():
        for j in range(BLOCK_PIECES):
            gather(tab_ref, slot, j).wait()

        @pl.when(b >= 2)
        def _():
            for j in range(BLOCK_PIECES):
                scatter(tab_ref, slot, j).wait()

        xb = xbuf_ref[slot].astype(BF16)
        g = jnp.dot(xb, wgb_ref[...], preferred_element_type=F32) + bg_ref[...]
        u = jnp.dot(xb, wub_ref[...], preferred_element_type=F32) + bu_ref[...]
        g = jnp.minimum(g, SWIGLU_LIMIT)
        u = jnp.clip(u, -SWIGLU_LIMIT, SWIGLU_LIMIT)
        act = (u + 1.0) * g * jax.nn.sigmoid(SWIGLU_ALPHA * g)
        ybuf_ref[slot] = (jnp.dot(act.astype(BF16), wdb_ref[...], preferred_element_type=F32)
                          + bd_ref[...])
        for j in range(BLOCK_PIECES):
            scatter(tab_ref, slot, j).start(priority=j % 2)

    @pl.when(b == n_used - 1)
    def _():
        for j in range(BLOCK_PIECES):
            scatter(tab_ref, slot, j).wait()

        @pl.when(b >= 1)
        def _():
            for j in range(BLOCK_PIECES):
                scatter(tab_ref, 1 - slot, j).wait()


def _experts(block_exp, n_used, tail, tab, xs, wg, bg, wu, bu, wd, bd, ns):
    d, de = wg.shape[1], wg.shape[2]
    n_blocks = tab.shape[0]
    n_tiles = xs.shape[0] // ns
    assert tab.shape[2] == 2 * BLOCK_PIECES == V7X_LANES
    wmap = lambda b, be, nu, tl: (be[b], 0, 0)
    smem_tab = lambda f: pl.BlockSpec((None, 1, V7X_LANES), f, memory_space=pltpu.SMEM)
    any_spec = pl.BlockSpec(memory_space=pl.ANY)
    return pl.pallas_call(
        functools.partial(_experts_kernel, ns=ns, n_tiles=n_tiles),
        grid_spec=pltpu.PrefetchScalarGridSpec(
            num_scalar_prefetch=3,
            grid=(n_blocks,),
            in_specs=[
                smem_tab(lambda b, be, nu, tl: (b, 0, 0)),
                smem_tab(lambda b, be, nu, tl: (jnp.minimum(b + 1, n_blocks - 1), 0, 0)),
                any_spec,
                pl.BlockSpec((None, d, de), wmap), pl.BlockSpec((None, 1, de), wmap),
                pl.BlockSpec((None, d, de), wmap), pl.BlockSpec((None, 1, de), wmap),
                pl.BlockSpec((None, de, d), wmap), pl.BlockSpec((None, 1, d), wmap),
            ],
            out_specs=any_spec,
            scratch_shapes=[
                pltpu.VMEM((d, de), BF16), pltpu.VMEM((d, de), BF16), pltpu.VMEM((de, d), BF16),
                pltpu.VMEM((2, MOE_ROWS, d), F32), pltpu.VMEM((2, MOE_ROWS, d), F32),
                pltpu.VMEM((MOE_ROWS, d), F32),
                pltpu.SemaphoreType.DMA((2,)), pltpu.SemaphoreType.DMA((2,)), pltpu.SemaphoreType.DMA,
            ],
        ),
        out_shape=jax.ShapeDtypeStruct((n_tiles * ns + 2 * MOE_ROWS, d), F32),
        compiler_params=_cparams("arbitrary"),
        name="moe_experts",
    )(block_exp, n_used, tail, tab, tab, xs, wg, bg, wu, bu, wd, bd)


def _combine_kernel(extra_ref, route_ref, x1_ref, yb_ref, o_ref, sorted_ref, sems, *, tm, ns):
    i = pl.program_id(0)
    slot = lax.rem(i, 2)
    base_rows = tm * TOP_K

    def copies(tile, sl, fn):
        def rows_copy(first, n_rows):
            src = pl.ds(pl.multiple_of(tile * ns + first, RUN), n_rows)
            dst = pl.ds(pl.multiple_of(first, RUN), n_rows)
            return pltpu.make_async_copy(yb_ref.at[src], sorted_ref.at[sl, dst], sems.at[sl])

        fn(rows_copy(0, base_rows))
        _binary_pieces(extra_ref[tile],
                       lambda off, size: fn(rows_copy(base_rows + off * RUN, size * RUN)))

    @pl.when(i == 0)
    def _():
        sorted_ref[...] = jnp.zeros_like(sorted_ref)
        copies(i, slot, lambda cp: cp.start())

    @pl.when(i + 1 < pl.num_programs(0))
    def _():
        copies(i + 1, 1 - slot, lambda cp: cp.start())

    copies(i, slot, lambda cp: cp.wait())

    ys = sorted_ref[slot].astype(BF16)
    route = route_ref[...]
    q = lax.broadcasted_iota(jnp.int32, (tm, ns), 1).astype(F32)
    wmat = jnp.zeros((tm, ns), F32)
    for k in range(TOP_K):
        wmat = wmat + jnp.where(q == route[:, k:k + 1], route[:, TOP_K + k:TOP_K + k + 1], 0.0)
    o_ref[...] = x1_ref[...] + jnp.dot(wmat.astype(BF16), ys, preferred_element_type=F32)


def _combine(extra, route, x1, yb, tm, ns):
    t, d = x1.shape
    return pl.pallas_call(
        functools.partial(_combine_kernel, tm=tm, ns=ns),
        grid=(t // tm,),
        in_specs=[
            pl.BlockSpec(memory_space=pltpu.SMEM),
            pl.BlockSpec((tm, V7X_LANES), lambda i: (i, 0)),
            pl.BlockSpec((tm, d), lambda i: (i, 0)),
            pl.BlockSpec(memory_space=pl.ANY),
        ],
        out_specs=pl.BlockSpec((tm, d), lambda i: (i, 0)),
        out_shape=jax.ShapeDtypeStruct((t, d), F32),
        scratch_shapes=[pltpu.VMEM((2, ns, d), F32), pltpu.SemaphoreType.DMA((2,))],
        compiler_params=_cparams("arbitrary"),
        name="moe_combine",
    )(extra, route, x1, yb)


def _head_indicator(width, head_dim):
    lane_head = jnp.arange(width) // head_dim
    return (lane_head[:, None] == jnp.arange(V7X_LANES)[None, :]).astype(BF16)


def _layer(x, mem, norm_mix, w_in, a_q_gain, a_k_gain, a_rel_bias, conv_w, conv_b, dt_bias, a_log,
           d_skip, ssm_norm, norm_mem, w_mem_kv, x_q_gain, x_k_gain, w_br_a, w_br_b, w_br_c, w_out,
           norm_ffn, w_router, b_router, w_gate, b_gate, w_up, b_up, w_down, b_down):
    b, s, d = x.shape
    t = b * s
    a_width = A_HEADS * A_HEAD_DIM
    inner = SSM_HEADS * SSM_HEAD_DIM
    gs = SSM_GROUPS * SSM_STATE
    x_width = X_HEADS * X_HEAD_DIM
    assert d == a_width == x_width and inner == 2 * d and 2 * gs == d

    o_dt = 3 * a_width + inner + inner + 2 * gs
    assert o_dt % d == 0
    w_a = w_in[:, :o_dt].astype(BF16)
    w_b = w_in[:, o_dt + SSM_HEADS:].astype(BF16)
    plain2 = ("plain", "plain")
    roles = (("qa", "ka"), plain2, plain2, plain2, ("qx", "sig"), ("sig", "sig"))
    w_dt = jnp.pad(w_in[:, o_dt:o_dt + SSM_HEADS], ((0, 0), (0, V7X_LANES - SSM_HEADS))).astype(BF16)
    gains = jnp.zeros((V7X_SUBLANES, d), F32)
    gains = gains.at[0].set(jnp.tile(a_q_gain, A_HEADS) * (A_HEAD_DIM ** -0.5 * LOG2_E))
    gains = gains.at[1].set(jnp.tile(a_k_gain, A_HEADS))
    gains = gains.at[2].set(jnp.tile(x_q_gain, X_HEADS) * X_HEAD_DIM ** -0.5)

    x2d = x.reshape(t, d)
    proj, dt_raw = _in_proj(x2d, norm_mix.reshape(1, d), w_a, w_b, w_dt, gains, roles,
                            tm=min(1024, t), tn=2 * d)
    proj3 = proj.reshape(b, s, proj.shape[1])

    y_a = _attention(proj3, _attn_bias(a_rel_bias), q_tile=0, k_tile=1, v_tile=2)

    pad_h = lambda v: jnp.pad(v.astype(F32), (0, V7X_LANES - SSM_HEADS)).reshape(1, V7X_LANES)
    e_mat = _head_indicator(inner, SSM_HEAD_DIM).T
    y_b = _ssd(proj3, dt_raw.reshape(b, s, V7X_LANES),
               conv_w[:, :inner], conv_b[:inner].reshape(1, inner),
               conv_w[:, inner:], conv_b[inner:].reshape(1, 2 * gs),
               pad_h(dt_bias), pad_h(-jnp.exp(a_log.astype(F32))),
               jnp.repeat(d_skip.astype(F32), SSM_HEAD_DIM).reshape(1, inner),
               ssm_norm.reshape(1, inner), e_mat, z_tile=3, x_tile=5, bc_tile=7)

    k_mem, v_mem = _mem_kv(mem, norm_mem.reshape(1, d), w_mem_kv.astype(BF16),
                           x_k_gain.reshape(1, X_HEAD_DIM))
    y_c = _mem_attn(proj3, k_mem, v_mem, q_blk=8, tq=min(512, s))

    w_r = jnp.pad(w_router, ((0, 0), (0, V7X_LANES - N_EXPERTS)))
    b_r = jnp.pad(b_router, (0, V7X_LANES - N_EXPERTS)).reshape(1, V7X_LANES)
    tm_moe = min(MOE_TILE, t)
    n_tiles = t // tm_moe
    x1, h2, route, tile_cnt = _merge(
        x2d, y_a.reshape(t, a_width), y_b.reshape(t, inner), y_c.reshape(t, x_width), proj,
        w_br_a.astype(BF16), w_br_b.astype(BF16), w_br_c.astype(BF16), w_out.astype(BF16),
        norm_ffn.reshape(1, d), w_r, b_r, gate_blk=9, tm=min(MERGE_TM, t), rt=tm_moe)

    ns = -(-(tm_moe * TOP_K + N_EXPERTS * (RUN - 1)) // V7X_LANES) * V7X_LANES
    units = ns // RUN
    n_blocks = -(-(t * TOP_K + n_tiles * N_EXPERTS * (RUN - 1) + N_EXPERTS * (MOE_ROWS - 1))
                 // MOE_ROWS)
    n8 = tile_cnt[:, 0, :N_EXPERTS].astype(jnp.int32) // RUN
    total = jnp.sum(n8, axis=0)
    padded = (total + BLOCK_PIECES - 1) // BLOCK_PIECES * BLOCK_PIECES
    pad_ends = jnp.cumsum(padded)
    pad_starts = pad_ends - padded
    n_used = (pad_ends[-1] // BLOCK_PIECES).reshape(1).astype(jnp.int32)
    blk = jnp.minimum(jnp.arange(n_blocks, dtype=jnp.int32), n_used[0] - 1)
    block_exp = jnp.minimum(jnp.sum(pad_ends[None, :] <= (blk * BLOCK_PIECES)[:, None], axis=1),
                            N_EXPERTS - 1).astype(jnp.int32)
    slot_j = jnp.arange(BLOCK_PIECES, dtype=jnp.int32)[None, :]
    onehot_e = (block_exp[:, None] == jnp.arange(N_EXPERTS)[None, :]).astype(jnp.int32)
    q = blk[:, None] * BLOCK_PIECES + slot_j - (onehot_e @ pad_starts)[:, None]
    real = q < (onehot_e @ total)[:, None]
    ends_b = onehot_e @ jnp.cumsum(n8, axis=0).T
    tile_of = jnp.minimum(jnp.sum(ends_b[:, None, :] <= q[:, :, None], axis=-1), n_tiles - 1)
    tile_1h = (tile_of[:, :, None] == jnp.arange(n_tiles)[None, None, :]).astype(jnp.int32)
    starts_b = ends_b - onehot_e @ n8.T
    in_tile_b = onehot_e @ (jnp.cumsum(n8, axis=1) - n8).T
    piece = (tile_of * units + jnp.sum(tile_1h * (in_tile_b - starts_b)[:, None, :], axis=-1) + q)
    zero_piece = (tm_moe * TOP_K + N_EXPERTS * (RUN - 1)) // RUN
    spare = n_tiles * units + (blk % 2)[:, None] * BLOCK_PIECES + slot_j
    tab = jnp.concatenate([jnp.where(real, piece, zero_piece), jnp.where(real, piece, spare)], axis=1)
    tab = tab.reshape(n_blocks, 1, 2 * BLOCK_PIECES).astype(jnp.int32)
    used = jnp.sum(n8, axis=1)
    tail = (units - used).astype(jnp.int32)
    extra = (used - tm_moe * TOP_K // RUN).astype(jnp.int32)

    xs = _dispatch(h2, route, tm_moe, ns)
    yb = _experts(block_exp, n_used, tail, tab, xs,
                  w_gate, b_gate.reshape(N_EXPERTS, 1, -1),
                  w_up, b_up.reshape(N_EXPERTS, 1, -1),
                  w_down, b_down.reshape(N_EXPERTS, 1, -1), ns)
    out = _combine(extra, route, x1, yb, tm_moe, ns)
    return out.reshape(b, s, d)


def kernel(x, mem, norm_mix, w_in, a_q_gain, a_k_gain, a_rel_bias, conv_w, conv_b, dt_bias, a_log, d_skip, ssm_norm, norm_mem, w_mem_kv, x_q_gain, x_k_gain, w_br_a, w_br_b, w_br_c, w_out, norm_ffn, w_router, b_router, w_gate, b_gate, w_up, b_up, w_down, b_down):
    for l in range(norm_mix.shape[0]):
        x = _layer(x, mem, norm_mix[l], w_in[l], a_q_gain[l], a_k_gain[l], a_rel_bias[l], conv_w[l],
                   conv_b[l], dt_bias[l], a_log[l], d_skip[l], ssm_norm[l], norm_mem[l], w_mem_kv[l],
                   x_q_gain[l], x_k_gain[l], w_br_a[l], w_br_b[l], w_br_c[l], w_out[l], norm_ffn[l],
                   w_router[l], b_router[l], w_gate[l], b_gate[l], w_up[l], b_up[l], w_down[l],
                   b_down[l])
    return x
```

```python
import functools

import jax
import jax.numpy as jnp
from jax import lax
from jax.experimental import pallas as pl
from jax.experimental.pallas import tpu as pltpu

F32 = jnp.float32
BF16 = jnp.bfloat16
HIGHEST = lax.Precision.HIGHEST

V7X_LANES = 128
V7X_SUBLANES = 8
V7X_VMEM_LIMIT_BYTES = 56 * 1024 * 1024

EPS = 1e-6
LOG2_E = 1.4426950408889634
NEG = -1e30

CHUNK = 64
A_HEADS = 16
A_HEAD_DIM = 64
LEFT_CHUNKS = 8
REL_CLIP = 128
SSM_HEADS = 32
SSM_HEAD_DIM = 64
SSM_GROUPS = 4
SSM_STATE = 128
CONV_WIDTH = 4
X_HEADS = 4
X_HEAD_DIM = 256
N_EXPERTS = 32
TOP_K = 4
SWIGLU_LIMIT = 7.0
SWIGLU_ALPHA = 1.702

ATTN_TQ = 256
SSD_L = 256
MOE_ROWS = 512
MOE_TILE = 512
MERGE_TM = 512


def _cparams(*sem):
    return pltpu.CompilerParams(dimension_semantics=sem, vmem_limit_bytes=V7X_VMEM_LIMIT_BYTES)


def _split_bf16(v):
    hi = v.astype(BF16)
    lo = (v - hi.astype(F32)).astype(BF16)
    return hi, lo


def _in_proj_kernel(x_ref, nw_ref, wa_ref, wb_ref, wdt_ref, gains_ref, o_ref, dt_ref, h_ref, *,
                    roles, steps_a):
    j = pl.program_id(1)

    @pl.when(j == 0)
    def _():
        x = x_ref[...]
        ms = jnp.mean(x * x, axis=-1, keepdims=True)
        hb = (x * lax.rsqrt(ms + EPS) * nw_ref[...]).astype(BF16)
        h_ref[...] = hb
        dt_ref[...] = jnp.dot(hb, wdt_ref[...], preferred_element_type=F32)

    def head_norm(acc, head_dim, gain_row):
        outs = []
        if head_dim >= V7X_LANES:
            for c0 in range(0, acc.shape[1], head_dim):
                blk = acc[:, c0:c0 + head_dim]
                s = jnp.sum(blk * blk, axis=-1, keepdims=True)
                scale = lax.rsqrt(s * (1.0 / head_dim) + EPS)
                outs.append(blk * scale * gain_row[:, c0:c0 + head_dim])
        else:
            assert 2 * head_dim == V7X_LANES
            lo = lax.broadcasted_iota(jnp.int32, (1, V7X_LANES), 1) < head_dim
            for c0 in range(0, acc.shape[1], V7X_LANES):
                blk = acc[:, c0:c0 + V7X_LANES]
                sq = blk * blk
                s_lo = jnp.sum(jnp.where(lo, sq, 0.0), axis=-1, keepdims=True)
                s_hi = jnp.sum(jnp.where(lo, 0.0, sq), axis=-1, keepdims=True)
                scale = jnp.where(lo, lax.rsqrt(s_lo * (1.0 / head_dim) + EPS),
                                  lax.rsqrt(s_hi * (1.0 / head_dim) + EPS))
                outs.append(blk * scale * gain_row[:, c0:c0 + V7X_LANES])
        return jnp.concatenate(outs, axis=1)

    def cond_for(role):
        c = None
        for jj, r in enumerate(roles):
            if r == role:
                c = (j == jj) if c is None else jnp.logical_or(c, j == jj)
        return c

    def epilogue(acc, role):
        if role == "qa":
            return head_norm(acc, A_HEAD_DIM, gains_ref[0:1, :])
        if role == "ka":
            return head_norm(acc, A_HEAD_DIM, gains_ref[1:2, :])
        if role == "qx":
            return head_norm(acc, X_HEAD_DIM, gains_ref[2:3, :])
        if role == "sig":
            return 0.5 * jnp.tanh(0.5 * acc) + 0.5
        return acc

    d = x_ref.shape[1]
    for step_roles in sorted(set(roles)):
        in_a = {jj < steps_a for jj, r in enumerate(roles) if r == step_roles}
        assert len(in_a) == 1
        w_ref = wa_ref if in_a.pop() else wb_ref

        @pl.when(cond_for(step_roles))
        def _(step_roles=step_roles, w_ref=w_ref):
            acc = jnp.dot(h_ref[...], w_ref[...], preferred_element_type=F32)
            for k, role in enumerate(step_roles):
                cs = slice(k * d, (k + 1) * d)
                o_ref[:, cs] = epilogue(acc[:, cs], role).astype(o_ref.dtype)


def _in_proj(x2d, norm_w, w_a, w_b, w_dt, gains, roles, tm, tn):
    t, d = x2d.shape
    n = w_a.shape[1] + w_b.shape[1]
    assert t % tm == 0 and w_a.shape[1] % tn == 0 and w_b.shape[1] % tn == 0
    assert len(roles) == n // tn
    assert all(len(r) * d == tn for r in roles)
    steps_a, steps_b = w_a.shape[1] // tn, w_b.shape[1] // tn
    const = lambda i, j: (0, 0)
    return pl.pallas_call(
        functools.partial(_in_proj_kernel, roles=roles, steps_a=steps_a),
        grid=(t // tm, n // tn),
        in_specs=[
            pl.BlockSpec((tm, d), lambda i, j: (i, 0)),
            pl.BlockSpec((1, d), const),
            pl.BlockSpec((d, tn), lambda i, j: (0, jnp.minimum(j, steps_a - 1))),
            pl.BlockSpec((d, tn), lambda i, j: (0, jnp.where(j < steps_a, steps_b - 1, j - steps_a))),
            pl.BlockSpec((d, V7X_LANES), const),
            pl.BlockSpec(gains.shape, const),
        ],
        out_specs=[
            pl.BlockSpec((tm, tn), lambda i, j: (i, j)),
            pl.BlockSpec((tm, V7X_LANES), lambda i, j: (i, 0)),
        ],
        out_shape=[
            jax.ShapeDtypeStruct((t, n), BF16),
            jax.ShapeDtypeStruct((t, V7X_LANES), F32),
        ],
        scratch_shapes=[pltpu.VMEM((tm, d), BF16)],
        compiler_params=_cparams("parallel", "arbitrary"),
        name="in_proj",
    )(x2d, norm_w, w_a, w_b, w_dt, gains)


def _attn_kernel(q_ref, *refs, tq, nprev):
    k_refs = refs[:nprev + 1]
    v_refs = refs[nprev + 1:2 * nprev + 2]
    bias_ref, o_ref = refs[2 * nprev + 2:]
    qb = pl.program_id(1)
    nk = (nprev + 1) * tq
    lane = lax.broadcasted_iota(jnp.int32, (1, V7X_LANES), 1)
    col = lax.broadcasted_iota(jnp.int32, (1, nk), 1)
    before_start = col < (nprev - qb) * tq
    for hp in range(A_HEADS // 2):
        ls = slice(hp * V7X_LANES, (hp + 1) * V7X_LANES)
        q2 = q_ref[:, ls]
        kk = jnp.concatenate([r[:, ls] for r in k_refs], axis=0)
        vv = jnp.concatenate([r[:, ls] for r in v_refs], axis=0)
        outs = []
        for hh in range(2):
            sel = (lane < A_HEAD_DIM) if hh == 0 else (lane >= A_HEAD_DIM)
            qm = jnp.where(sel, q2, jnp.zeros_like(q2))
            s = lax.dot_general(qm, kk, (((1,), (1,)), ((), ())), preferred_element_type=F32)
            s = jnp.where(before_start, NEG, s + bias_ref[2 * hp + hh])
            m = jnp.max(s, axis=-1, keepdims=True)
            p = jnp.exp2(s - m)
            l = jnp.sum(p, axis=-1, keepdims=True)
            o = jnp.dot(p.astype(BF16), vv, preferred_element_type=F32)
            outs.append(o / l)
        o_ref[:, ls] = jnp.where(lane < A_HEAD_DIM, outs[0], outs[1]).astype(o_ref.dtype)


def _attention(proj3, bias, q_tile, k_tile, v_tile):
    b, s, _ = proj3.shape
    tq = ATTN_TQ
    width = A_HEADS * A_HEAD_DIM
    left = LEFT_CHUNKS * CHUNK
    assert left % tq == 0 and s % tq == 0
    nprev = left // tq

    def kv_spec(tile, back):
        return pl.BlockSpec((None, tq, width), lambda bi, qi: (bi, jnp.maximum(qi - back, 0), tile))

    in_specs = [pl.BlockSpec((None, tq, width), lambda bi, qi: (bi, qi, q_tile))]
    in_specs += [kv_spec(k_tile, nprev - i) for i in range(nprev + 1)]
    in_specs += [kv_spec(v_tile, nprev - i) for i in range(nprev + 1)]
    in_specs += [pl.BlockSpec(bias.shape, lambda bi, qi: (0, 0, 0), pipeline_mode=pl.Buffered(1))]
    return pl.pallas_call(
        functools.partial(_attn_kernel, tq=tq, nprev=nprev),
        grid=(b, s // tq),
        in_specs=in_specs,
        out_specs=pl.BlockSpec((None, tq, width), lambda bi, qi: (bi, qi, 0)),
        out_shape=jax.ShapeDtypeStruct((b, s, width), BF16),
        compiler_params=_cparams("parallel", "parallel"),
        name="chunk_attn",
    )(proj3, *([proj3] * (2 * nprev + 2)), bias)


def _attn_bias_kernel(v_ref, o_ref, *, tq, nk):
    x = jnp.broadcast_to(v_ref[...], (tq, v_ref.shape[-1]))
    toeplitz = pltpu.roll(x, 0, 1, stride=1, stride_axis=0)[:, :nk]
    qc = lax.broadcasted_iota(jnp.int32, (tq, nk), 0) // CHUNK
    kc = lax.broadcasted_iota(jnp.int32, (tq, nk), 1) // CHUNK
    in_band = jnp.where(kc >= qc, kc - qc, LEFT_CHUNKS + 1) <= LEFT_CHUNKS
    o_ref[...] = jnp.where(in_band, toeplitz * LOG2_E, NEG)


def _attn_bias(rel_bias):
    tq = ATTN_TQ
    left = LEFT_CHUNKS * CHUNK
    nk = left + tq
    m_len = 1 << (tq + nk - 1).bit_length()
    m = jnp.arange(m_len)
    diff = jnp.where(m < nk, m, m - m_len)
    v = rel_bias[:, jnp.clip(left - diff, -REL_CLIP, REL_CLIP) + REL_CLIP].astype(F32)
    h = v.shape[0]
    return pl.pallas_call(
        functools.partial(_attn_bias_kernel, tq=tq, nk=nk),
        grid=(h,),
        in_specs=[pl.BlockSpec((None, 1, m_len), lambda i: (i, 0, 0))],
        out_specs=pl.BlockSpec((None, tq, nk), lambda i: (i, 0, 0)),
        out_shape=jax.ShapeDtypeStruct((h, tq, nk), F32),
        compiler_params=_cparams("parallel"),
        name="attn_bias",
    )(v.reshape(h, 1, m_len))


def _ssd_kernel(z0_ref, z1_ref, x0_ref, x1_ref, bc_ref, dt_ref, cwx_ref, cbx_ref, cwbc_ref, cbbc_ref,
                dtb_ref, aneg_ref, dskip_ref, gain_ref, e_ref, o_ref,
                xf_ref, bcf_ref, st_ref, xs_ref, y_ref, *, L):
    c = pl.program_id(1)
    inner = SSM_HEADS * SSM_HEAD_DIM
    gw = inner // SSM_GROUPS
    gs = SSM_GROUPS * SSM_STATE
    tail = V7X_SUBLANES
    half = inner // 2

    @pl.when(c == 0)
    def _():
        xf_ref[0:tail, :] = jnp.zeros((tail, inner), F32)
        bcf_ref[0:tail, :] = jnp.zeros((tail, 2 * gs), F32)
        st_ref[...] = jnp.zeros_like(st_ref)

    xf_ref[tail:, 0:half] = x0_ref[...].astype(F32)
    xf_ref[tail:, half:inner] = x1_ref[...].astype(F32)
    bcf_ref[tail:, :] = bc_ref[...].astype(F32)

    def conv_silu(src_ref, w_ref, b_ref, c0, c1):
        acc = b_ref[:, c0:c1] + w_ref[CONV_WIDTH - 1:CONV_WIDTH, c0:c1] * src_ref[tail:tail + L, c0:c1]
        for k in range(1, CONV_WIDTH):
            acc = acc + (w_ref[CONV_WIDTH - 1 - k:CONV_WIDTH - k, c0:c1]
                         * src_ref[tail - k:tail - k + L, c0:c1])
        return acc * jax.nn.sigmoid(acc)

    for g in range(SSM_GROUPS):
        xs_ref[:, g * gw:(g + 1) * gw] = conv_silu(xf_ref, cwx_ref, cbx_ref, g * gw, (g + 1) * gw)
    bmat = conv_silu(bcf_ref, cwbc_ref, cbbc_ref, 0, gs)
    cmat = conv_silu(bcf_ref, cwbc_ref, cbbc_ref, gs, 2 * gs)
    xf_ref[0:tail, :] = xf_ref[L:L + tail, :]
    bcf_ref[0:tail, :] = bcf_ref[L:L + tail, :]

    pre = dt_ref[...] + dtb_ref[...]
    dt = jnp.maximum(pre, 0.0) + jnp.log1p(jnp.exp(-jnp.abs(pre)))
    a = dt * aneg_ref[...]
    row = lax.broadcasted_iota(jnp.int32, (L, L), 0)
    colm = lax.broadcasted_iota(jnp.int32, (L, L), 1)
    lower = colm <= row
    tri = jnp.where(lower, 1.0, 0.0).astype(F32)
    cs = jnp.dot(tri, a, precision=HIGHEST, preferred_element_type=F32)
    cs2 = cs * LOG2_E
    src_t = (cs2 - jnp.log2(dt)).T
    cs_last = cs[L - 1:L, :]
    w_state = dt * jnp.exp(cs_last - cs)
    e_cs = jnp.exp(cs)
    chunk_decay = jnp.broadcast_to(jnp.exp(cs_last), (tail, V7X_LANES))
    stacked = jnp.concatenate([w_state, e_cs, chunk_decay], axis=0)
    s_hi, s_lo = _split_bf16(stacked)
    expanded = (jnp.dot(s_hi, e_ref[...], preferred_element_type=F32)
                + jnp.dot(s_lo, e_ref[...], preferred_element_type=F32))
    w_state_e = expanded[0:L]
    e_cs_e = expanded[L:2 * L]
    decay_e = expanded[2 * L:2 * L + 1]

    lane = lax.broadcasted_iota(jnp.int32, (1, V7X_LANES), 1)
    pairs_per_group = gw // V7X_LANES
    for g in range(SSM_GROUPS):
        bg = bmat[:, g * SSM_STATE:(g + 1) * SSM_STATE]
        cg = cmat[:, g * SSM_STATE:(g + 1) * SSM_STATE].astype(BF16)
        cb = lax.dot_general(cg, bg.astype(BF16), (((1,), (1,)), ((), ())),
                             preferred_element_type=F32)
        state_b = st_ref[g].astype(BF16)
        y_off = jnp.dot(cg, state_b, preferred_element_type=F32) * e_cs_e[:, g * gw:(g + 1) * gw]
        for pr in range(pairs_per_group):
            c0 = g * gw + pr * V7X_LANES
            xp = xs_ref[:, c0:c0 + V7X_LANES]
            xpb = xp.astype(BF16)
            acc = y_off[:, pr * V7X_LANES:(pr + 1) * V7X_LANES] + dskip_ref[:, c0:c0 + V7X_LANES] * xp
            for hh in range(2):
                h = c0 // SSM_HEAD_DIM + hh
                d = cs2[:, h:h + 1] - src_t[h:h + 1, :]
                m = cb * jnp.exp2(jnp.where(lower, d, NEG))
                sel = (lane < SSM_HEAD_DIM) if hh == 0 else (lane >= SSM_HEAD_DIM)
                xm = jnp.where(sel, xpb, jnp.zeros_like(xpb))
                acc = acc + jnp.dot(m.astype(BF16), xm, preferred_element_type=F32)
            y_ref[:, c0:c0 + V7X_LANES] = acc
        xw = (xs_ref[:, g * gw:(g + 1) * gw] * w_state_e[:, g * gw:(g + 1) * gw]).astype(BF16)
        new = jnp.dot(bg.T.astype(BF16), xw, preferred_element_type=F32)
        st_ref[g] = st_ref[g] * decay_e[:, g * gw:(g + 1) * gw] + new

    for g in range(SSM_GROUPS):
        sl = slice(g * gw, (g + 1) * gw)
        z_ref, z0 = (z0_ref, g * gw) if g * gw < half else (z1_ref, g * gw - half)
        zz = z_ref[:, z0:z0 + gw].astype(F32)
        yz = y_ref[:, sl] * (zz * jax.nn.sigmoid(zz))
        ms = jnp.mean(yz * yz, axis=-1, keepdims=True)
        o_ref[:, sl] = (yz * lax.rsqrt(ms + EPS) * gain_ref[:, sl]).astype(o_ref.dtype)


def _ssd(proj3, dt3, cwx, cbx, cwbc, cbbc, dtb, aneg, dskip_e, gain, e_mat, z_tile, x_tile, bc_tile):
    b, s, _ = proj3.shape
    L = SSD_L
    assert s % L == 0
    inner = SSM_HEADS * SSM_HEAD_DIM
    gs2 = 2 * SSM_GROUPS * SSM_STATE
    assert gs2 == inner // 2
    const = lambda bi, ci: (0, 0)
    full = lambda a: pl.BlockSpec(a.shape, const)
    tile = lambda k: pl.BlockSpec((None, L, gs2), lambda bi, ci: (bi, ci, k))
    return pl.pallas_call(
        functools.partial(_ssd_kernel, L=L),
        grid=(b, s // L),
        in_specs=[
            tile(z_tile), tile(z_tile + 1), tile(x_tile), tile(x_tile + 1), tile(bc_tile),
            pl.BlockSpec((None, L, V7X_LANES), lambda bi, ci: (bi, ci, 0)),
            full(cwx), full(cbx), full(cwbc), full(cbbc), full(dtb), full(aneg), full(dskip_e),
            full(gain), full(e_mat),
        ],
        out_specs=pl.BlockSpec((None, L, inner), lambda bi, ci: (bi, ci, 0)),
        out_shape=jax.ShapeDtypeStruct((b, s, inner), BF16),
        scratch_shapes=[
            pltpu.VMEM((L + V7X_SUBLANES, inner), F32),
            pltpu.VMEM((L + V7X_SUBLANES, gs2), F32),
            pltpu.VMEM((SSM_GROUPS, SSM_STATE, inner // SSM_GROUPS), F32),
            pltpu.VMEM((L, inner), F32),
            pltpu.VMEM((L, inner), F32),
        ],
        compiler_params=_cparams("parallel", "arbitrary"),
        name="ssd",
    )(proj3, proj3, proj3, proj3, proj3, dt3, cwx, cbx, cwbc, cbbc, dtb, aneg, dskip_e, gain, e_mat)


def _mem_attn_kernel(q_ref, mem_ref, g_ref, w_ref, kg_ref, o_ref, k_ref, v_ref):
    width = X_HEADS * X_HEAD_DIM

    @pl.when(pl.program_id(1) == 0)
    def _():
        m = mem_ref[...]
        ms = jnp.mean(m * m, axis=-1, keepdims=True)
        mn = (m * lax.rsqrt(ms + EPS) * g_ref[...]).astype(BF16)
        kv = jnp.dot(mn, w_ref[...], preferred_element_type=F32)
        for h in range(X_HEADS):
            sl = slice(h * X_HEAD_DIM, (h + 1) * X_HEAD_DIM)
            kh = kv[:, sl]
            r = lax.rsqrt(jnp.mean(kh * kh, axis=-1, keepdims=True) + EPS)
            k_ref[:, sl] = (kh * r * kg_ref[...]).astype(k_ref.dtype)
        v_ref[...] = kv[:, width:].astype(v_ref.dtype)

    for h in range(X_HEADS):
        sl = slice(h * X_HEAD_DIM, (h + 1) * X_HEAD_DIM)
        s = lax.dot_general(q_ref[:, sl], k_ref[:, sl], (((1,), (1,)), ((), ())),
                            preferred_element_type=F32)
        m = jnp.max(s, axis=-1, keepdims=True)
        p = jnp.exp(s - m)
        l = jnp.sum(p, axis=-1, keepdims=True)
        o = jnp.dot(p.astype(BF16), v_ref[:, sl], preferred_element_type=F32)
        o_ref[:, sl] = (o / l).astype(o_ref.dtype)


def _mem_attn(proj3, mem, norm_mem, w_kv, k_gain, q_blk, tq):
    b, s, _ = proj3.shape
    m, d = mem.shape[1], mem.shape[2]
    width = X_HEADS * X_HEAD_DIM
    assert s % tq == 0
    const = lambda bi, qi: (0, 0)
    return pl.pallas_call(
        _mem_attn_kernel,
        grid=(b, s // tq),
        in_specs=[
            pl.BlockSpec((None, tq, width), lambda bi, qi: (bi, qi, q_blk)),
            pl.BlockSpec((None, m, d), lambda bi, qi: (bi, 0, 0)),
            pl.BlockSpec((1, d), const),
            pl.BlockSpec((d, 2 * width), const),
            pl.BlockSpec((1, X_HEAD_DIM), const),
        ],
        out_specs=pl.BlockSpec((None, tq, width), lambda bi, qi: (bi, qi, 0)),
        out_shape=jax.ShapeDtypeStruct((b, s, width), BF16),
        scratch_shapes=[pltpu.VMEM((m, width), BF16), pltpu.VMEM((m, width), BF16)],
        compiler_params=_cparams("parallel", "arbitrary"),
        name="mem_attn",
    )(proj3, mem, norm_mem, w_kv, k_gain)


def _merge_kernel(x_ref, ya_ref, yb_ref, yc_ref, g0_ref, g1_ref, g2_ref, wa_ref, wb_ref, wc_ref,
                  wo_ref, nf_ref, wrh_ref, wrl_ref, br_ref,
                  x1_ref, h2_ref, route_ref, cnt_ref, *, tm, rt):
    merged = (g0_ref[...].astype(F32) * jnp.dot(ya_ref[...], wa_ref[...], preferred_element_type=F32)
              + g1_ref[...].astype(F32) * jnp.dot(yb_ref[...], wb_ref[...], preferred_element_type=F32)
              + g2_ref[...].astype(F32) * jnp.dot(yc_ref[...], wc_ref[...], preferred_element_type=F32))
    x1 = x_ref[...] + jnp.dot(merged.astype(BF16), wo_ref[...], preferred_element_type=F32)
    x1_ref[...] = x1
    ms = jnp.mean(x1 * x1, axis=-1, keepdims=True)
    h2 = x1 * lax.rsqrt(ms + EPS) * nf_ref[...]
    h2_ref[...] = h2.astype(h2_ref.dtype)

    h_hi, h_lo = _split_bf16(h2)
    logits_all = (jnp.dot(h_hi, wrh_ref[...], preferred_element_type=F32)
                  + jnp.dot(h_lo, wrh_ref[...], preferred_element_type=F32)
                  + jnp.dot(h_hi, wrl_ref[...], preferred_element_type=F32)) + br_ref[...]
    for sub in range(tm // rt):
        _route_tile(logits_all[sub * rt:(sub + 1) * rt], route_ref.at[pl.ds(sub * rt, rt)],
                    cnt_ref.at[sub], rt)


def _route_tile(logits, route_ref, cnt_ref, tm):
    lane = lax.broadcasted_iota(jnp.int32, (tm, V7X_LANES), 1)
    lane_f = lane.astype(F32)
    work = jnp.where(lane < N_EXPERTS, logits, NEG)
    sel_val, sel_oh = [], []
    for _ in range(TOP_K):
        mval = jnp.max(work, axis=-1, keepdims=True)
        ik = jnp.min(jnp.where(work == mval, lane_f, float(V7X_LANES)), axis=-1, keepdims=True)
        oh = lane_f == ik
        work = jnp.where(oh, NEG, work)
        sel_val.append(mval)
        sel_oh.append(oh)
    ex = [jnp.exp(v - sel_val[0]) for v in sel_val]
    denom = ex[0] + ex[1] + ex[2] + ex[3]

    oh_all = jnp.zeros((tm, V7X_LANES), F32)
    for oh in sel_oh:
        oh_all = oh_all + jnp.where(oh, 1.0, 0.0)
    row = lax.broadcasted_iota(jnp.int32, (tm, tm), 0)
    colm = lax.broadcasted_iota(jnp.int32, (tm, tm), 1)
    strict = jnp.where(colm < row, 1.0, 0.0).astype(BF16)
    before = jnp.dot(strict, oh_all.astype(BF16), preferred_element_type=F32)
    cnt = jnp.sum(oh_all, axis=0, keepdims=True)
    cnt8 = jnp.floor((cnt + (V7X_SUBLANES - 1.0)) * (1.0 / V7X_SUBLANES)) * V7X_SUBLANES
    cnt8 = jnp.broadcast_to(cnt8, (V7X_SUBLANES, V7X_LANES))
    er = lax.broadcasted_iota(jnp.int32, (V7X_LANES, V7X_LANES), 0)
    ec = lax.broadcasted_iota(jnp.int32, (V7X_LANES, V7X_LANES), 1)
    earlier = jnp.where(er < ec, 1.0, 0.0).astype(BF16)
    run_start = jnp.dot(cnt8.astype(BF16), earlier, preferred_element_type=F32)[0:1, :]
    slot = before + run_start
    route = jnp.zeros((tm, V7X_LANES), F32)
    for k in range(TOP_K):
        pos = jnp.sum(jnp.where(sel_oh[k], slot, 0.0), axis=-1, keepdims=True)
        route = jnp.where(lane == k, pos, route)
        route = jnp.where(lane == TOP_K + k, ex[k] / denom, route)
    route_ref[...] = route
    cnt_ref[...] = cnt8


def _merge(x2d, ya, yb, yc, proj, wa, wb, wc, wo, nf, wr, br, gate_blk, tm, rt):
    t, d = x2d.shape
    assert t % tm == 0 and tm % rt == 0
    wr_hi, wr_lo = _split_bf16(wr)
    const = lambda i: (0, 0)
    full = lambda a: pl.BlockSpec(a.shape, const)
    rows = lambda w: pl.BlockSpec((tm, w), lambda i: (i, 0))
    return pl.pallas_call(
        functools.partial(_merge_kernel, tm=tm, rt=rt),
        grid=(t // tm,),
        in_specs=[
            rows(d), rows(ya.shape[1]), rows(yb.shape[1]), rows(yc.shape[1]),
            pl.BlockSpec((tm, d), lambda i: (i, gate_blk)),
            pl.BlockSpec((tm, d), lambda i: (i, gate_blk + 1)),
            pl.BlockSpec((tm, d), lambda i: (i, gate_blk + 2)),
            full(wa), full(wb), full(wc), full(wo), full(nf), full(wr_hi), full(wr_lo), full(br),
        ],
        out_specs=[rows(d), rows(d), rows(V7X_LANES),
                   pl.BlockSpec((tm // rt, V7X_SUBLANES, V7X_LANES), lambda i: (i, 0, 0))],
        out_shape=[
            jax.ShapeDtypeStruct((t, d), F32),
            jax.ShapeDtypeStruct((t, d), BF16),
            jax.ShapeDtypeStruct((t, V7X_LANES), F32),
            jax.ShapeDtypeStruct((t // rt, V7X_SUBLANES, V7X_LANES), F32),
        ],
        compiler_params=_cparams("parallel"),
        name="merge_route",
    )(x2d, ya, yb, yc, proj, proj, proj, wa, wb, wc, wo, nf, wr_hi, wr_lo, br)


RUN = V7X_SUBLANES
BLOCK_PIECES = MOE_ROWS // RUN
TAIL_SIZES = (32, 16, 8, 4, 2, 1)
assert MOE_TILE * TOP_K % V7X_LANES == 0 and N_EXPERTS * (RUN - 1) < 2 * TAIL_SIZES[0] * RUN


def _binary_pieces(n, body):
    for size in TAIL_SIZES:
        @pl.when((n & size) != 0)
        def _(size=size):
            body(n & ~(2 * size - 1), size)


def _dispatch_kernel(h_ref, route_ref, xs_ref, *, tm, ns):
    pos_t = route_ref[...].T
    q = lax.broadcasted_iota(jnp.int32, (ns, tm), 0).astype(F32)
    perm = jnp.zeros((ns, tm), F32)
    for k in range(TOP_K):
        perm = perm + jnp.where(q == pos_t[k:k + 1, :], 1.0, 0.0)
    xs_ref[...] = jnp.dot(perm.astype(BF16), h_ref[...], preferred_element_type=F32)


def _dispatch(h2, route, tm, ns):
    t, d = h2.shape
    return pl.pallas_call(
        functools.partial(_dispatch_kernel, tm=tm, ns=ns),
        grid=(t // tm,),
        in_specs=[
            pl.BlockSpec((tm, d), lambda i: (i, 0)),
            pl.BlockSpec((tm, V7X_LANES), lambda i: (i, 0)),
        ],
        out_specs=pl.BlockSpec((ns, d), lambda i: (i, 0)),
        out_shape=jax.ShapeDtypeStruct((t // tm * ns, d), F32),
        compiler_params=_cparams("parallel"),
        name="moe_dispatch",
    )(h2, route)


def _experts_kernel(bexp_ref, nused_ref, tail_ref, tab_ref, next_tab_ref, xs_ref, wg_ref, bg_ref,
                    wu_ref, bu_ref, wd_ref, bd_ref, yb_ref,
                    wgb_ref, wub_ref, wdb_ref, xbuf_ref, ybuf_ref, zero_ref, gsem, ssem, zsem, *,
                    ns, n_tiles):
    b = pl.program_id(0)
    n_used = nused_ref[0]
    slot = lax.rem(b, 2)

    def gather(t_ref, sl, j):
        src = pl.ds(pl.multiple_of(t_ref[0, j] * RUN, RUN), RUN)
        return pltpu.make_async_copy(xs_ref.at[src], xbuf_ref.at[sl, pl.ds(j * RUN, RUN)], gsem.at[sl])

    def scatter(t_ref, sl, j):
        dst = pl.ds(pl.multiple_of(t_ref[0, BLOCK_PIECES + j] * RUN, RUN), RUN)
        return pltpu.make_async_copy(ybuf_ref.at[sl, pl.ds(j * RUN, RUN)], yb_ref.at[dst], ssem.at[sl])

    def zero_copy(rows):
        return pltpu.make_async_copy(zero_ref.at[pl.ds(0, rows.size)], yb_ref.at[rows], zsem)

    def zero_unwritten(fn):
        for half in range(2):
            fn(zero_copy(pl.ds(n_tiles * ns + half * MOE_ROWS, MOE_ROWS)))

        def per_tile(i, carry):
            n = tail_ref[i]
            first = (i + 1) * ns - n * RUN
            _binary_pieces(n, lambda off, size: fn(zero_copy(
                pl.ds(pl.multiple_of(first + off * RUN, RUN), size * RUN))))
            return carry
        lax.fori_loop(0, n_tiles, per_tile, 0)

    @pl.when(b == 0)
    def _():
        zero_ref[...] = jnp.zeros_like(zero_ref)
        zero_unwritten(lambda cp: cp.start())
        zero_unwritten(lambda cp: cp.wait())
        for j in range(BLOCK_PIECES):
            gather(tab_ref, 0, j).start(priority=j % 2)

    @pl.when(b + 1 < n_used)
    def _():
        for j in range(BLOCK_PIECES):
            gather(next_tab_ref, 1 - slot, j).start(priority=j % 2)

    new_expert = jnp.logical_or(b == 0, bexp_ref[b] != bexp_ref[jnp.maximum(b - 1, 0)])

    @pl.when(new_expert)
    def _():
        wgb_ref[...] = wg_ref[...].astype(BF16)
        wub_ref[...] = wu_ref[...].astype(BF16)
        wdb_ref[...] = wd_ref[...].astype(BF16)

    @pl.when(b < n_used)
    def _():
        for j in range(BLOCK_PIECES):
            gather(tab_ref, slot, j).wait()

        @pl.when(b >= 2)
        def _():
            for j in range(BLOCK_PIECES):
                scatter(tab_ref, slot, j).wait()

        xb = xbuf_ref[slot].astype(BF16)
        g = jnp.dot(xb, wgb_ref[...], preferred_element_type=F32) + bg_ref[...]
        u = jnp.dot(xb, wub_ref[...], preferred_element_type=F32) + bu_ref[...]
        g = jnp.minimum(g, SWIGLU_LIMIT)
        u = jnp.clip(u, -SWIGLU_LIMIT, SWIGLU_LIMIT)
        act = (u + 1.0) * g * jax.nn.sigmoid(SWIGLU_ALPHA * g)
        ybuf_ref[slot] = (jnp.dot(act.astype(BF16), wdb_ref[...], preferred_element_type=F32)
                          + bd_ref[...])
        for j in range(BLOCK_PIECES):
            scatter(tab_ref, slot, j).start(priority=j % 2)

    @pl.when(b == n_used - 1)
    def _():
        for j in range(BLOCK_PIECES):
            scatter(tab_ref, slot, j).wait()

        @pl.when(b >= 1)
        def _():
            for j in range(BLOCK_PIECES):
                scatter(tab_ref, 1 - slot, j).wait()


def _experts(block_exp, n_used, tail, tab, xs, wg, bg, wu, bu, wd, bd, ns):
    d, de = wg.shape[1], wg.shape[2]
    n_blocks = tab.shape[0]
    n_tiles = xs.shape[0] // ns
    assert tab.shape[2] == 2 * BLOCK_PIECES == V7X_LANES
    wmap = lambda b, be, nu, tl: (be[b], 0, 0)
    smem_tab = lambda f: pl.BlockSpec((None, 1, V7X_LANES), f, memory_space=pltpu.SMEM)
    any_spec = pl.BlockSpec(memory_space=pl.ANY)
    return pl.pallas_call(
        functools.partial(_experts_kernel, ns=ns, n_tiles=n_tiles),
        grid_spec=pltpu.PrefetchScalarGridSpec(
            num_scalar_prefetch=3,
            grid=(n_blocks,),
            in_specs=[
                smem_tab(lambda b, be, nu, tl: (b, 0, 0)),
                smem_tab(lambda b, be, nu, tl: (jnp.minimum(b + 1, n_blocks - 1), 0, 0)),
                any_spec,
                pl.BlockSpec((None, d, de), wmap), pl.BlockSpec((None, 1, de), wmap),
                pl.BlockSpec((None, d, de), wmap), pl.BlockSpec((None, 1, de), wmap),
                pl.BlockSpec((None, de, d), wmap), pl.BlockSpec((None, 1, d), wmap),
            ],
            out_specs=any_spec,
            scratch_shapes=[
                pltpu.VMEM((d, de), BF16), pltpu.VMEM((d, de), BF16), pltpu.VMEM((de, d), BF16),
                pltpu.VMEM((2, MOE_ROWS, d), F32), pltpu.VMEM((2, MOE_ROWS, d), F32),
                pltpu.VMEM((MOE_ROWS, d), F32),
                pltpu.SemaphoreType.DMA((2,)), pltpu.SemaphoreType.DMA((2,)), pltpu.SemaphoreType.DMA,
            ],
        ),
        out_shape=jax.ShapeDtypeStruct((n_tiles * ns + 2 * MOE_ROWS, d), F32),
        compiler_params=_cparams("arbitrary"),
        name="moe_experts",
    )(block_exp, n_used, tail, tab, tab, xs, wg, bg, wu, bu, wd, bd)


def _combine_kernel(extra_ref, route_ref, x1_ref, yb_ref, o_ref, sorted_ref, sems, *, tm, ns):
    i = pl.program_id(0)
    slot = lax.rem(i, 2)
    base_rows = tm * TOP_K

    def copies(tile, sl, fn):
        def rows_copy(first, n_rows):
            src = pl.ds(pl.multiple_of(tile * ns + first, RUN), n_rows)
            dst = pl.ds(pl.multiple_of(first, RUN), n_rows)
            return pltpu.make_async_copy(yb_ref.at[src], sorted_ref.at[sl, dst], sems.at[sl])

        fn(rows_copy(0, base_rows))
        _binary_pieces(extra_ref[tile],
                       lambda off, size: fn(rows_copy(base_rows + off * RUN, size * RUN)))

    @pl.when(i == 0)
    def _():
        sorted_ref[...] = jnp.zeros_like(sorted_ref)
        copies(i, slot, lambda cp: cp.start())

    @pl.when(i + 1 < pl.num_programs(0))
    def _():
        copies(i + 1, 1 - slot, lambda cp: cp.start())

    copies(i, slot, lambda cp: cp.wait())

    ys = sorted_ref[slot].astype(BF16)
    route = route_ref[...]
    q = lax.broadcasted_iota(jnp.int32, (tm, ns), 1).astype(F32)
    wmat = jnp.zeros((tm, ns), F32)
    for k in range(TOP_K):
        wmat = wmat + jnp.where(q == route[:, k:k + 1], route[:, TOP_K + k:TOP_K + k + 1], 0.0)
    o_ref[...] = x1_ref[...] + jnp.dot(wmat.astype(BF16), ys, preferred_element_type=F32)


def _combine(extra, route, x1, yb, tm, ns):
    t, d = x1.shape
    return pl.pallas_call(
        functools.partial(_combine_kernel, tm=tm, ns=ns),
        grid=(t // tm,),
        in_specs=[
            pl.BlockSpec(memory_space=pltpu.SMEM),
            pl.BlockSpec((tm, V7X_LANES), lambda i: (i, 0)),
            pl.BlockSpec((tm, d), lambda i: (i, 0)),
            pl.BlockSpec(memory_space=pl.ANY),
        ],
        out_specs=pl.BlockSpec((tm, d), lambda i: (i, 0)),
        out_shape=jax.ShapeDtypeStruct((t, d), F32),
        scratch_shapes=[pltpu.VMEM((2, ns, d), F32), pltpu.SemaphoreType.DMA((2,))],
        compiler_params=_cparams("arbitrary"),
        name="moe_combine",
    )(extra, route, x1, yb)


def _head_indicator(width, head_dim):
    lane_head = jnp.arange(width) // head_dim
    return (lane_head[:, None] == jnp.arange(V7X_LANES)[None, :]).astype(BF16)


def _layer(x, mem, norm_mix, w_in, a_q_gain, a_k_gain, a_rel_bias, conv_w, conv_b, dt_bias, a_log,
           d_skip, ssm_norm, norm_mem, w_mem_kv, x_q_gain, x_k_gain, w_br_a, w_br_b, w_br_c, w_out,
           norm_ffn, w_router, b_router, w_gate, b_gate, w_up, b_up, w_down, b_down):
    b, s, d = x.shape
    t = b * s
    a_width = A_HEADS * A_HEAD_DIM
    inner = SSM_HEADS * SSM_HEAD_DIM
    gs = SSM_GROUPS * SSM_STATE
    x_width = X_HEADS * X_HEAD_DIM
    assert d == a_width == x_width and inner == 2 * d and 2 * gs == d

    o_dt = 3 * a_width + inner + inner + 2 * gs
    assert o_dt % d == 0
    w_a = w_in[:, :o_dt].astype(BF16)
    w_b = w_in[:, o_dt + SSM_HEADS:].astype(BF16)
    plain2 = ("plain", "plain")
    roles = (("qa", "ka"), plain2, plain2, plain2, ("qx", "sig"), ("sig", "sig"))
    w_dt = jnp.pad(w_in[:, o_dt:o_dt + SSM_HEADS], ((0, 0), (0, V7X_LANES - SSM_HEADS))).astype(BF16)
    gains = jnp.zeros((V7X_SUBLANES, d), F32)
    gains = gains.at[0].set(jnp.tile(a_q_gain, A_HEADS) * (A_HEAD_DIM ** -0.5 * LOG2_E))
    gains = gains.at[1].set(jnp.tile(a_k_gain, A_HEADS))
    gains = gains.at[2].set(jnp.tile(x_q_gain, X_HEADS) * X_HEAD_DIM ** -0.5)

    x2d = x.reshape(t, d)
    proj, dt_raw = _in_proj(x2d, norm_mix.reshape(1, d), w_a, w_b, w_dt, gains, roles,
                            tm=min(1024, t), tn=2 * d)
    proj3 = proj.reshape(b, s, proj.shape[1])

    y_a = _attention(proj3, _attn_bias(a_rel_bias), q_tile=0, k_tile=1, v_tile=2)

    pad_h = lambda v: jnp.pad(v.astype(F32), (0, V7X_LANES - SSM_HEADS)).reshape(1, V7X_LANES)
    e_mat = _head_indicator(inner, SSM_HEAD_DIM).T
    y_b = _ssd(proj3, dt_raw.reshape(b, s, V7X_LANES),
               conv_w[:, :inner], conv_b[:inner].reshape(1, inner),
               conv_w[:, inner:], conv_b[inner:].reshape(1, 2 * gs),
               pad_h(dt_bias), pad_h(-jnp.exp(a_log.astype(F32))),
               jnp.repeat(d_skip.astype(F32), SSM_HEAD_DIM).reshape(1, inner),
               ssm_norm.reshape(1, inner), e_mat, z_tile=3, x_tile=5, bc_tile=7)

    y_c = _mem_attn(proj3, mem, norm_mem.reshape(1, d), w_mem_kv.astype(BF16),
                    x_k_gain.reshape(1, X_HEAD_DIM), q_blk=8, tq=min(512, s))

    w_r = jnp.pad(w_router, ((0, 0), (0, V7X_LANES - N_EXPERTS)))
    b_r = jnp.pad(b_router, (0, V7X_LANES - N_EXPERTS)).reshape(1, V7X_LANES)
    tm_moe = min(MOE_TILE, t)
    n_tiles = t // tm_moe
    x1, h2, route, tile_cnt = _merge(
        x2d, y_a.reshape(t, a_width), y_b.reshape(t, inner), y_c.reshape(t, x_width), proj,
        w_br_a.astype(BF16), w_br_b.astype(BF16), w_br_c.astype(BF16), w_out.astype(BF16),
        norm_ffn.reshape(1, d), w_r, b_r, gate_blk=9, tm=min(MERGE_TM, t), rt=tm_moe)

    ns = -(-(tm_moe * TOP_K + N_EXPERTS * (RUN - 1)) // V7X_LANES) * V7X_LANES
    units = ns // RUN
    n_blocks = -(-(t * TOP_K + n_tiles * N_EXPERTS * (RUN - 1) + N_EXPERTS * (MOE_ROWS - 1))
                 // MOE_ROWS)
    n8 = tile_cnt[:, 0, :N_EXPERTS].astype(jnp.int32) // RUN
    total = jnp.sum(n8, axis=0)
    padded = (total + BLOCK_PIECES - 1) // BLOCK_PIECES * BLOCK_PIECES
    pad_ends = jnp.cumsum(padded)
    pad_starts = pad_ends - padded
    n_used = (pad_ends[-1] // BLOCK_PIECES).reshape(1).astype(jnp.int32)
    blk = jnp.minimum(jnp.arange(n_blocks, dtype=jnp.int32), n_used[0] - 1)
    block_exp = jnp.minimum(jnp.sum(pad_ends[None, :] <= (blk * BLOCK_PIECES)[:, None], axis=1),
                            N_EXPERTS - 1).astype(jnp.int32)
    slot_j = jnp.arange(BLOCK_PIECES, dtype=jnp.int32)[None, :]
    onehot_e = (block_exp[:, None] == jnp.arange(N_EXPERTS)[None, :]).astype(jnp.int32)
    q = blk[:, None] * BLOCK_PIECES + slot_j - (onehot_e @ pad_starts)[:, None]
    real = q < (onehot_e @ total)[:, None]
    ends_b = onehot_e @ jnp.cumsum(n8, axis=0).T
    tile_of = jnp.minimum(jnp.sum(ends_b[:, None, :] <= q[:, :, None], axis=-1), n_tiles - 1)
    tile_1h = (tile_of[:, :, None] == jnp.arange(n_tiles)[None, None, :]).astype(jnp.int32)
    starts_b = ends_b - onehot_e @ n8.T
    in_tile_b = onehot_e @ (jnp.cumsum(n8, axis=1) - n8).T
    piece = (tile_of * units + jnp.sum(tile_1h * (in_tile_b - starts_b)[:, None, :], axis=-1) + q)
    zero_piece = (tm_moe * TOP_K + N_EXPERTS * (RUN - 1)) // RUN
    spare = n_tiles * units + (blk % 2)[:, None] * BLOCK_PIECES + slot_j
    tab = jnp.concatenate([jnp.where(real, piece, zero_piece), jnp.where(real, piece, spare)], axis=1)
    tab = tab.reshape(n_blocks, 1, 2 * BLOCK_PIECES).astype(jnp.int32)
    used = jnp.sum(n8, axis=1)
    tail = (units - used).astype(jnp.int32)
    extra = (used - tm_moe * TOP_K // RUN).astype(jnp.int32)

    xs = _dispatch(h2, route, tm_moe, ns)
    yb = _experts(block_exp, n_used, tail, tab, xs,
                  w_gate, b_gate.reshape(N_EXPERTS, 1, -1),
                  w_up, b_up.reshape(N_EXPERTS, 1, -1),
                  w_down, b_down.reshape(N_EXPERTS, 1, -1), ns)
    out = _combine(extra, route, x1, yb, tm_moe, ns)
    return out.reshape(b, s, d)


def kernel(x, mem, norm_mix, w_in, a_q_gain, a_k_gain, a_rel_bias, conv_w, conv_b, dt_bias, a_log, d_skip, ssm_norm, norm_mem, w_mem_kv, x_q_gain, x_k_gain, w_br_a, w_br_b, w_br_c, w_out, norm_ffn, w_router, b_router, w_gate, b_gate, w_up, b_up, w_down, b_down):
    for l in range(norm_mix.shape[0]):
        x = _layer(x, mem, norm_mix[l], w_in[l], a_q_gain[l], a_k_gain[l], a_rel_bias[l], conv_w[l],
                   conv_b[l], dt_bias[l], a_log[l], d_skip[l], ssm_norm[l], norm_mem[l], w_mem_kv[l],
                   x_q_gain[l], x_k_gain[l], w_br_a[l], w_br_b[l], w_br_c[l], w_out[l], norm_ffn[l],
                   w_router[l], b_router[l], w_gate[l], b_gate[l], w_up[l], b_up[l], w_down[l],
                   b_down[l])
    return x
```
